```python
import jax, jax.numpy as jnp
from jax import lax
import numpy as np

D_MODEL = 2048
BATCH = 2
SEQ = 4096
DEPTH = 2
DEC_BATCH = 128
DEC_SEQ = 4
PAST_LEN = 8192
PAGE_SIZE = 128

MLA_HEADS = 8
Q_LORA = 512
KV_LORA = 512
QK_NOPE = 128
QK_ROPE = 64
V_HEAD = 128
ROPE_THETA = 10000.0
Q_BLOCK = 128
ATTN_SCALE = (QK_NOPE + QK_ROPE) ** -0.5
SSM_HEADS = 16
SSM_HEAD_DIM = 64
SSM_INNER = SSM_HEADS * SSM_HEAD_DIM
SSM_GROUPS = 2
SSM_STATE = 128
SSM_CONV = 4
SSM_CHUNK = 128
SSM_CONV_DIM = SSM_INNER + 2 * SSM_GROUPS * SSM_STATE
SC_WIDTH = 1024
SC_CONV = 3
D_FF = 5632
N_EXPERTS = 8
TOP_K = 2
D_FF_EXPERT = 1408
N_DENSE = (DEPTH + 1) // 2
N_MOE = DEPTH // 2
EPS = 1e-6

IN_SIZES = (Q_LORA, KV_LORA, QK_ROPE, SSM_INNER, SSM_CONV_DIM, SSM_HEADS, SC_WIDTH, SC_WIDTH, SC_WIDTH)
IN_OFFSETS = tuple(int(v) for v in np.cumsum(IN_SIZES)[:-1])
D_IN = int(sum(IN_SIZES))

kernel_name = 'hybrid_mla_ssd_shortconv_gated_decoder_step'


def rmsnorm(x, g):
    xf = x.astype(jnp.float32)
    y = xf * lax.rsqrt(jnp.mean(xf * xf, axis=-1, keepdims=True) + EPS)
    return (y * g.astype(jnp.float32)).astype(x.dtype)


def group_rmsnorm(x, g, groups, out_dtype):
    xf = x.astype(jnp.float32)
    xg = xf.reshape(xf.shape[:-1] + (groups, xf.shape[-1] // groups))
    xg = xg * lax.rsqrt(jnp.mean(xg * xg, axis=-1, keepdims=True) + EPS)
    return (xg.reshape(xf.shape) * g.astype(jnp.float32)).astype(out_dtype)


def rope(x, pos):
    half = x.shape[-1] // 2
    inv = ROPE_THETA ** (-jnp.arange(half, dtype=jnp.float32) / half)
    ang = pos.astype(jnp.float32)[:, None] * inv[None, :]
    shape = (ang.shape[0],) + (1,) * (x.ndim - 3) + (half,)
    cos = jnp.cos(ang).reshape(shape)
    sin = jnp.sin(ang).reshape(shape)
    x1 = x[..., :half].astype(jnp.float32)
    x2 = x[..., half:].astype(jnp.float32)
    return jnp.concatenate([x1 * cos - x2 * sin, x2 * cos + x1 * sin], axis=-1).astype(x.dtype)


def causal_conv(x, prev, w, b):
    k_w = w.shape[0]
    t = x.shape[1]
    xp = jnp.concatenate([prev.astype(x.dtype), x], axis=1)
    y = sum(xp[:, k:k + t] * w[k] for k in range(k_w))
    if b is not None:
        y = y + b
    return y, xp[:, -(k_w - 1):]


def ssd_scan(x, dt, a_coef, bm, cm, h0):
    b, l, nh, p = x.shape
    g, n = bm.shape[2], bm.shape[3]
    r = nh // g
    ln = min(SSM_CHUNK, l)
    c = l // ln
    f32 = jnp.float32
    xd = (x.astype(f32) * dt[..., None]).reshape(b, c, ln, g, r, p)
    acs = jnp.cumsum((dt * a_coef).reshape(b, c, ln, g, r), axis=2)
    bc = bm.astype(f32).reshape(b, c, ln, g, n)
    cc = cm.astype(f32).reshape(b, c, ln, g, n)
    mask = jnp.tril(jnp.ones((ln, ln), bool))[None, None, :, :, None, None]
    diff = acs[:, :, :, None] - acs[:, :, None, :]
    decay = jnp.exp(jnp.where(mask, diff, -jnp.inf))
    cb = jnp.einsum('bctgn,bcsgn->bctsg', cc, bc)
    y_diag = jnp.einsum('bctsgr,bcsgrp->bctgrp', cb[..., None] * decay, xd)
    decay_end = jnp.exp(acs[:, :, -1:] - acs)
    s_chunk = jnp.einsum('bcsgn,bcsgr,bcsgrp->bcgrpn', bc, decay_end, xd)
    chunk_decay = jnp.exp(acs[:, :, -1])

    def step(hc, inp):
        dec, sc = inp
        return dec[..., None, None] * hc + sc, hc

    h_last, h_in = lax.scan(step, h0.astype(f32).reshape(b, g, r, p, n),
                            (jnp.moveaxis(chunk_decay, 1, 0), jnp.moveaxis(s_chunk, 1, 0)))
    h_in = jnp.moveaxis(h_in, 0, 1)
    y_off = jnp.einsum('bctgn,bcgrpn->bctgrp', cc, h_in) * jnp.exp(acs)[..., None]
    y = (y_diag + y_off).reshape(b, l, nh, p)
    return y, h_last.reshape(b, nh, p, n)


def prompt_attend(q_lat, q_pe, ckv, kpe):
    bn, s, nh, _ = q_lat.shape
    nb = s // Q_BLOCK
    qb = jnp.moveaxis(q_lat.reshape(bn, nb, Q_BLOCK, nh, KV_LORA), 1, 0)
    pb = jnp.moveaxis(q_pe.reshape(bn, nb, Q_BLOCK, nh, QK_ROPE), 1, 0)
    k_pos = jnp.arange(s)

    def block(args):
        i, ql, qp = args
        sc = (jnp.einsum('bthc,bsc->bhts', ql, ckv) + jnp.einsum('bthr,bsr->bhts', qp, kpe)).astype(jnp.float32) * ATTN_SCALE
        q_pos = i * Q_BLOCK + jnp.arange(Q_BLOCK)
        sc = jnp.where(q_pos[:, None] >= k_pos[None, :], sc, -jnp.inf)
        pr = jax.nn.softmax(sc, axis=-1).astype(ckv.dtype)
        return jnp.einsum('bhts,bsc->bthc', pr, ckv)

    o = lax.map(block, (jnp.arange(nb), qb, pb))
    return jnp.moveaxis(o, 0, 1).reshape(bn, s, nh, KV_LORA)


def make_sample_attend(past_ckv, past_kpe):
    def attend(q_lat, q_pe, ckv, kpe):
        t = q_lat.shape[1]
        n_past = past_ckv.shape[1]
        s_past = jnp.einsum('bthc,bsc->bhts', q_lat, past_ckv) + jnp.einsum('bthr,bsr->bhts', q_pe, past_kpe)
        s_new = jnp.einsum('bthc,bsc->bhts', q_lat, ckv) + jnp.einsum('bthr,bsr->bhts', q_pe, kpe)
        s_new = jnp.where(jnp.tril(jnp.ones((t, t), bool)), s_new.astype(jnp.float32), -jnp.inf)
        sc = jnp.concatenate([s_past.astype(jnp.float32), s_new], axis=-1) * ATTN_SCALE
        pr = jax.nn.softmax(sc, axis=-1).astype(ckv.dtype)
        return (jnp.einsum('bhts,bsc->bthc', pr[..., :n_past], past_ckv)
                + jnp.einsum('bhts,bsc->bthc', pr[..., n_past:], ckv))
    return attend


def token_mixers(h, pos, mconv_prev, sconv_prev, ssm_prev, attend,
                 w_in, g_q_a, w_q_b, g_kv_a, w_kv_b, ssm_conv_w, ssm_conv_b, ssm_dt_bias, ssm_a_log,
                 ssm_d, g_ssm_norm, sc_conv_w, w_gate, b_gate, w_br_attn, w_br_ssm, w_br_sc, w_o):
    bn, t, _ = h.shape
    proj = h @ w_in
    q_c, kv_c, k_pe, z, xbc, dt_raw, sc_b, sc_c, sc_v = jnp.split(proj, IN_OFFSETS, axis=-1)
    q = jnp.einsum('btq,qhd->bthd', rmsnorm(q_c, g_q_a), w_q_b)
    q_nope = q[..., :QK_NOPE]
    q_pe = rope(q[..., QK_NOPE:], pos)
    ckv = rmsnorm(kv_c, g_kv_a)
    kpe = rope(k_pe, pos)
    w_uk = w_kv_b[..., :QK_NOPE]
    w_uv = w_kv_b[..., QK_NOPE:]
    q_lat = jnp.einsum('bthn,chn->bthc', q_nope, w_uk)
    o_lat = attend(q_lat, q_pe, ckv, kpe)
    y_attn = jnp.einsum('bthc,chv->bthv', o_lat, w_uv).reshape(bn, t, MLA_HEADS * V_HEAD)
    xbc_c, mconv_new = causal_conv(xbc, mconv_prev, ssm_conv_w, ssm_conv_b)
    xbc_c = jax.nn.silu(xbc_c)
    xs, bm, cm = jnp.split(xbc_c, [SSM_INNER, SSM_INNER + SSM_GROUPS * SSM_STATE], axis=-1)
    xs = xs.reshape(bn, t, SSM_HEADS, SSM_HEAD_DIM)
    bm = bm.reshape(bn, t, SSM_GROUPS, SSM_STATE)
    cm = cm.reshape(bn, t, SSM_GROUPS, SSM_STATE)
    dt = jax.nn.softplus((dt_raw + ssm_dt_bias).astype(jnp.float32))
    a_coef = -jnp.exp(ssm_a_log.astype(jnp.float32))
    y_ssd, ssm_new = ssd_scan(xs, dt, a_coef, bm, cm, ssm_prev)
    y_ssd = y_ssd + ssm_d.astype(jnp.float32)[:, None] * xs.astype(jnp.float32)
    y_ssd = y_ssd.reshape(bn, t, SSM_INNER) * jax.nn.silu(z.astype(jnp.float32))
    y_ssm = group_rmsnorm(y_ssd, g_ssm_norm, SSM_GROUPS, h.dtype)
    sc_conv, sconv_new = causal_conv(sc_c * sc_v, sconv_prev, sc_conv_w, None)
    y_sc = sc_b * sc_conv
    gates = jax.nn.sigmoid((h @ w_gate + b_gate).astype(jnp.float32)).astype(h.dtype)
    g_a, g_b, g_c = jnp.split(gates, 3, axis=-1)
    merged = g_a * (y_attn @ w_br_attn) + g_b * (y_ssm @ w_br_ssm) + g_c * (y_sc @ w_br_sc)
    return merged @ w_o, (ckv, kpe, ssm_new, mconv_new, sconv_new)


def swiglu(h, wg, wu, wd):
    return (jax.nn.silu(h @ wg) * (h @ wu)) @ wd


def moe_swiglu(h, w_router, wg, wu, wd):
    logits = (h @ w_router).astype(jnp.float32)
    top_v, top_i = lax.top_k(logits, TOP_K)
    top_w = jax.nn.softmax(top_v, axis=-1)
    gate = jnp.sum(jax.nn.one_hot(top_i, N_EXPERTS, dtype=jnp.float32) * top_w[..., None], axis=-2)
    hidden = jax.nn.silu(jnp.einsum('btd,edf->btef', h, wg)) * jnp.einsum('btd,edf->btef', h, wu)
    return jnp.einsum('btef,efd->btd', hidden * gate[..., None].astype(h.dtype), wd)


def setup_inputs(seed: int = 0) -> dict:
    key = jax.random.key(seed)
    ks = iter(jax.random.split(key, 64))
    f32 = jnp.float32

    def nrm(shape, scale):
        return jax.random.normal(next(ks), shape, f32) * scale

    def gain(shape):
        return 1.0 + nrm(shape, 0.05)

    n_pages = PAST_LEN // PAGE_SIZE
    n_used = DEC_BATCH * n_pages
    n_pool = n_used + max(1, n_used // 4)
    page_table = jax.random.permutation(next(ks), n_pool)[:n_used].reshape(DEC_BATCH, n_pages).astype(jnp.int32)
    dt0 = jnp.exp(jax.random.uniform(next(ks), (DEPTH, SSM_HEADS), f32) * (np.log(0.1) - np.log(0.001)) + np.log(0.001))
    dt_bias = dt0 + jnp.log(-jnp.expm1(-dt0))
    a_log = jnp.log(jax.random.uniform(next(ks), (DEPTH, SSM_HEADS), f32, 1.0, 16.0))
    return {
        'x_prompt': nrm((BATCH, SEQ, D_MODEL), 1.0),
        'x_sample': nrm((DEC_BATCH, DEC_SEQ, D_MODEL), 1.0),
        'cache_ckv': nrm((DEPTH, n_pool, PAGE_SIZE, KV_LORA), 1.0),
        'cache_kpe': nrm((DEPTH, n_pool, PAGE_SIZE, QK_ROPE), 1.0),
        'state_ssm': nrm((DEPTH, DEC_BATCH, SSM_HEADS, SSM_HEAD_DIM, SSM_STATE), 0.1),
        'state_mconv': nrm((DEPTH, DEC_BATCH, SSM_CONV - 1, SSM_CONV_DIM), 1.0),
        'state_sconv': nrm((DEPTH, DEC_BATCH, SC_CONV - 1, SC_WIDTH), 1.0),
        'page_table': page_table,
        'g_attn_norm': gain((DEPTH, D_MODEL)),
        'w_in': nrm((DEPTH, D_MODEL, D_IN), D_MODEL ** -0.5),
        'g_q_a': gain((DEPTH, Q_LORA)),
        'w_q_b': nrm((DEPTH, Q_LORA, MLA_HEADS, QK_NOPE + QK_ROPE), Q_LORA ** -0.5),
        'g_kv_a': gain((DEPTH, KV_LORA)),
        'w_kv_b': nrm((DEPTH, KV_LORA, MLA_HEADS, QK_NOPE + V_HEAD), KV_LORA ** -0.5),
        'ssm_conv_w': nrm((DEPTH, SSM_CONV, SSM_CONV_DIM), SSM_CONV ** -0.5),
        'ssm_conv_b': nrm((DEPTH, SSM_CONV_DIM), 0.02),
        'ssm_dt_bias': dt_bias,
        'ssm_a_log': a_log,
        'ssm_d': 1.0 + nrm((DEPTH, SSM_HEADS), 0.1),
        'g_ssm_norm': gain((DEPTH, SSM_INNER)),
        'sc_conv_w': nrm((DEPTH, SC_CONV, SC_WIDTH), SC_CONV ** -0.5),
        'w_gate': nrm((DEPTH, D_MODEL, 3 * D_MODEL), D_MODEL ** -0.5),
        'b_gate': nrm((DEPTH, 3 * D_MODEL), 0.02),
        'w_br_attn': nrm((DEPTH, MLA_HEADS * V_HEAD, D_MODEL), (MLA_HEADS * V_HEAD) ** -0.5),
        'w_br_ssm': nrm((DEPTH, SSM_INNER, D_MODEL), SSM_INNER ** -0.5),
        'w_br_sc': nrm((DEPTH, SC_WIDTH, D_MODEL), SC_WIDTH ** -0.5),
        'w_o': nrm((DEPTH, D_MODEL, D_MODEL), D_MODEL ** -0.5),
        'g_ffn_norm': gain((DEPTH, D_MODEL)),
        'w_ff_gate': nrm((N_DENSE, D_MODEL, D_FF), D_MODEL ** -0.5),
        'w_ff_up': nrm((N_DENSE, D_MODEL, D_FF), D_MODEL ** -0.5),
        'w_ff_down': nrm((N_DENSE, D_FF, D_MODEL), D_FF ** -0.5),
        'w_router': nrm((N_MOE, D_MODEL, N_EXPERTS), D_MODEL ** -0.5),
        'w_e_gate': nrm((N_MOE, N_EXPERTS, D_MODEL, D_FF_EXPERT), D_MODEL ** -0.5),
        'w_e_up': nrm((N_MOE, N_EXPERTS, D_MODEL, D_FF_EXPERT), D_MODEL ** -0.5),
        'w_e_down': nrm((N_MOE, N_EXPERTS, D_FF_EXPERT, D_MODEL), D_FF_EXPERT ** -0.5),
        'g_final': gain((D_MODEL,)),
    }


def reference(x_prompt, x_sample, cache_ckv, cache_kpe, state_ssm, state_mconv, state_sconv, page_table,
              g_attn_norm, w_in, g_q_a, w_q_b, g_kv_a, w_kv_b, ssm_conv_w, ssm_conv_b, ssm_dt_bias, ssm_a_log,
              ssm_d, g_ssm_norm, sc_conv_w, w_gate, b_gate, w_br_attn, w_br_ssm, w_br_sc, w_o, g_ffn_norm,
              w_ff_gate, w_ff_up, w_ff_down, w_router, w_e_gate, w_e_up, w_e_down, g_final):
    xp, xs = x_prompt, x_sample
    bp, tp = xp.shape[0], xp.shape[1]
    bs, ts = xs.shape[0], xs.shape[1]
    n_past = page_table.shape[1] * cache_ckv.shape[2]
    pos_p = jnp.arange(tp)
    pos_s = n_past + jnp.arange(ts)
    p_states, s_states = [], []
    for l in range(DEPTH):
        lw = dict(w_in=w_in[l], g_q_a=g_q_a[l], w_q_b=w_q_b[l], g_kv_a=g_kv_a[l], w_kv_b=w_kv_b[l],
                  ssm_conv_w=ssm_conv_w[l], ssm_conv_b=ssm_conv_b[l], ssm_dt_bias=ssm_dt_bias[l],
                  ssm_a_log=ssm_a_log[l], ssm_d=ssm_d[l], g_ssm_norm=g_ssm_norm[l], sc_conv_w=sc_conv_w[l],
                  w_gate=w_gate[l], b_gate=b_gate[l], w_br_attn=w_br_attn[l], w_br_ssm=w_br_ssm[l],
                  w_br_sc=w_br_sc[l], w_o=w_o[l])
        hp = rmsnorm(xp, g_attn_norm[l])
        mp, st_p = token_mixers(hp, pos_p,
                                jnp.zeros((bp, SSM_CONV - 1, SSM_CONV_DIM), xp.dtype),
                                jnp.zeros((bp, SC_CONV - 1, SC_WIDTH), xp.dtype),
                                jnp.zeros((bp, SSM_HEADS, SSM_HEAD_DIM, SSM_STATE), jnp.float32),
                                prompt_attend, **lw)
        past_ckv = cache_ckv[l, page_table].reshape(bs, n_past, KV_LORA)
        past_kpe = cache_kpe[l, page_table].reshape(bs, n_past, QK_ROPE)
        hs = rmsnorm(xs, g_attn_norm[l])
        ms, st_s = token_mixers(hs, pos_s, state_mconv[l], state_sconv[l], state_ssm[l],
                                make_sample_attend(past_ckv, past_kpe), **lw)
        xp = xp + mp
        xs = xs + ms
        p_states.append(st_p)
        s_states.append(st_s)
        hp2 = rmsnorm(xp, g_ffn_norm[l])
        hs2 = rmsnorm(xs, g_ffn_norm[l])
        i = l // 2
        if l % 2 == 0:
            xp = xp + swiglu(hp2, w_ff_gate[i], w_ff_up[i], w_ff_down[i])
            xs = xs + swiglu(hs2, w_ff_gate[i], w_ff_up[i], w_ff_down[i])
        else:
            xp = xp + moe_swiglu(hp2, w_router[i], w_e_gate[i], w_e_up[i], w_e_down[i])
            xs = xs + moe_swiglu(hs2, w_router[i], w_e_gate[i], w_e_up[i], w_e_down[i])
    y_prompt = rmsnorm(xp, g_final)
    y_sample = rmsnorm(xs, g_final)
    p_ckv, p_kpe, p_ssm, p_mconv, p_sconv = [jnp.stack(a, axis=0) for a in zip(*p_states)]
    s_ckv, s_kpe, s_ssm, s_mconv, s_sconv = [jnp.stack(a, axis=0) for a in zip(*s_states)]
    return (y_prompt, y_sample, p_ckv, p_kpe, p_ssm, p_mconv, p_sconv, s_ckv, s_kpe, s_ssm, s_mconv, s_sconv)
```

```python
import functools

import jax
import jax.numpy as jnp
import numpy as np
from jax import lax
from jax.experimental import pallas as pl
from jax.experimental.pallas import tpu as pltpu

F32 = jnp.float32
BF16 = jnp.bfloat16
EPS = 1e-6
ROPE_THETA = 10000.0
LANES = 128
MLA_HEADS = 8
QK_NOPE = 128
QK_ROPE = 64
V_HEAD = 128
KV_LORA = 512
Q_LORA = 512
Q_SLOT = 256
SSM_HEADS = 16
SSM_HEAD_DIM = 64
SSM_INNER = 1024
SSM_GROUPS = 2
SSM_STATE = 128
SSM_CONV = 4
SSM_CHUNK = 128
SC_WIDTH = 1024
SC_CONV = 3
N_EXPERTS = 8
ATTN_SCALE = (QK_NOPE + QK_ROPE) ** -0.5
VMEM_LIMIT = 56 * 1024 * 1024

COL_SCB, COL_SCC, COL_SCV, COL_Z, COL_X, COL_BC, COL_QC, COL_KVC, COL_KPE, COL_DT = (
    0, 1024, 2048, 3072, 4096, 5120, 5632, 6144, 6656, 6784)
D_IN_PAD = 6912


def _cparams(*sem):
    return pltpu.CompilerParams(dimension_semantics=sem, vmem_limit_bytes=VMEM_LIMIT)


def _sigmoid(x):
    return 1.0 / (1.0 + jnp.exp(-x))


def _silu(x):
    return x * _sigmoid(x)


def _softplus(x):
    return jnp.maximum(x, 0.0) + jnp.log1p(jnp.exp(-jnp.abs(x)))


def _dot(a, b, **kw):
    return jnp.dot(a, b, preferred_element_type=F32, **kw)


def _dot_nt(a, b):
    return lax.dot_general(a, b, (((1,), (1,)), ((), ())), preferred_element_type=F32)


def _dot_tn(a, b):
    return lax.dot_general(a, b, (((0,), (0,)), ((), ())), preferred_element_type=F32)


def _rms(x, g):
    r = lax.rsqrt(jnp.mean(x * x, axis=-1, keepdims=True) + EPS)
    return (x * r) * g


def _rmsnorm_body(x_ref, g_ref, o_ref):
    o_ref[...] = _rms(x_ref[...], g_ref[...]).astype(o_ref.dtype)


def rmsnorm(x, g, out_dtype, tm):
    m, d = x.shape
    return pl.pallas_call(
        _rmsnorm_body,
        grid=(m // tm,),
        in_specs=[pl.BlockSpec((tm, d), lambda i: (i, 0)), pl.BlockSpec((1, d), lambda i: (0, 0))],
        out_specs=pl.BlockSpec((tm, d), lambda i: (i, 0)),
        out_shape=jax.ShapeDtypeStruct((m, d), out_dtype),
        compiler_params=_cparams("parallel"),
        name="rmsnorm",
    )(x, g.reshape(1, d))


def _rmsnorm_router_body(x_ref, g_ref, wr_ref, h_ref, gate_ref):
    h = _rms(x_ref[...], g_ref[...])
    h_ref[...] = h.astype(h_ref.dtype)
    lg = _dot(h, wr_ref[...], precision=lax.Precision.HIGHEST)
    lane = lax.broadcasted_iota(jnp.int32, lg.shape, 1).astype(F32)
    lg = jnp.where(lane < N_EXPERTS, lg, -jnp.inf)
    m1 = jnp.max(lg, axis=1, keepdims=True)
    i1 = jnp.min(jnp.where(lg == m1, lane, float(LANES)), axis=1, keepdims=True)
    oh1 = lane == i1
    lg2 = jnp.where(oh1, -jnp.inf, lg)
    m2 = jnp.max(lg2, axis=1, keepdims=True)
    i2 = jnp.min(jnp.where(lg2 == m2, lane, float(LANES)), axis=1, keepdims=True)
    oh2 = lane == i2
    e = jnp.exp(m2 - m1)
    w1 = 1.0 / (1.0 + e)
    w2 = e / (1.0 + e)
    gate_ref[...] = jnp.where(oh1, w1, 0.0) + jnp.where(oh2, w2, 0.0)


def rmsnorm_router(x, g, w_router, tm):
    m, d = x.shape
    wr = jnp.pad(w_router, ((0, 0), (0, LANES - w_router.shape[1])))
    return pl.pallas_call(
        _rmsnorm_router_body,
        grid=(m // tm,),
        in_specs=[pl.BlockSpec((tm, d), lambda i: (i, 0)), pl.BlockSpec((1, d), lambda i: (0, 0)),
                  pl.BlockSpec((d, LANES), lambda i: (0, 0))],
        out_specs=[pl.BlockSpec((tm, d), lambda i: (i, 0)), pl.BlockSpec((tm, LANES), lambda i: (i, 0))],
        out_shape=[jax.ShapeDtypeStruct((m, d), BF16), jax.ShapeDtypeStruct((m, LANES), F32)],
        compiler_params=_cparams("parallel"),
        name="rmsnorm_router",
    )(x, g.reshape(1, d), wr)


def _mm_body(x_ref, w_ref, *rest, nk, has_res):
    if has_res:
        r_ref, o_ref = rest
    else:
        (o_ref,) = rest
    acc = _dot(x_ref[...], w_ref[...])
    if nk == 1:
        o_ref[...] = ((r_ref[...] + acc) if has_res else acc).astype(o_ref.dtype)
    else:
        k = pl.program_id(2)

        @pl.when(k == 0)
        def _():
            o_ref[...] = (r_ref[...] + acc) if has_res else acc

        @pl.when(k > 0)
        def _():
            o_ref[...] += acc


def matmul(x, w, *, tm, tn, tk=None, res=None, out_dtype=F32):
    m, kd = x.shape
    n = w.shape[1]
    tk = kd if tk is None else tk
    nk = kd // tk
    if nk > 1:
        assert out_dtype == F32
    in_specs = [pl.BlockSpec((tm, tk), lambda j, i, k: (i, k)), pl.BlockSpec((tk, tn), lambda j, i, k: (k, j))]
    args = [x, w]
    if res is not None:
        in_specs.append(pl.BlockSpec((tm, tn), lambda j, i, k: (i, j)))
        args.append(res)
    return pl.pallas_call(
        functools.partial(_mm_body, nk=nk, has_res=res is not None),
        grid=(n // tn, m // tm, nk),
        in_specs=in_specs,
        out_specs=pl.BlockSpec((tm, tn), lambda j, i, k: (i, j)),
        out_shape=jax.ShapeDtypeStruct((m, n), out_dtype),
        compiler_params=_cparams("parallel", "parallel", "arbitrary"),
        name="matmul",
    )(*args)


def _swiglu_body(x_ref, wg_ref, wu_ref, *rest, scaled):
    if scaled:
        gate_ref, o_ref = rest
    else:
        (o_ref,) = rest
    x = x_ref[...]
    g = _dot(x, wg_ref[...])
    u = _dot(x, wu_ref[...])
    hdn = _silu(g) * u
    if scaled:
        gate = gate_ref[...]
        lane = lax.broadcasted_iota(jnp.int32, gate.shape, 1)
        sc = jnp.sum(jnp.where(lane == pl.program_id(0), gate, 0.0), axis=1, keepdims=True)
        hdn = hdn * sc
    o_ref[...] = hdn.astype(o_ref.dtype)


def swiglu_up(x, wg, wu, *, tm, tf):
    m, d = x.shape
    f = wg.shape[1]
    return pl.pallas_call(
        functools.partial(_swiglu_body, scaled=False),
        grid=(f // tf, m // tm),
        in_specs=[pl.BlockSpec((tm, d), lambda j, i: (i, 0)), pl.BlockSpec((d, tf), lambda j, i: (0, j)),
                  pl.BlockSpec((d, tf), lambda j, i: (0, j))],
        out_specs=pl.BlockSpec((tm, tf), lambda j, i: (i, j)),
        out_shape=jax.ShapeDtypeStruct((m, f), BF16),
        compiler_params=_cparams("parallel", "parallel"),
        name="swiglu_up",
    )(x, wg, wu)


def moe_up(x, wg, wu, gate, *, tm):
    m, d = x.shape
    ne, _, f = wg.shape
    return pl.pallas_call(
        functools.partial(_swiglu_body, scaled=True),
        grid=(ne, m // tm),
        in_specs=[pl.BlockSpec((tm, d), lambda j, i: (i, 0)), pl.BlockSpec((None, d, f), lambda j, i: (j, 0, 0)),
                  pl.BlockSpec((None, d, f), lambda j, i: (j, 0, 0)), pl.BlockSpec((tm, LANES), lambda j, i: (i, 0))],
        out_specs=pl.BlockSpec((tm, f), lambda j, i: (i, j)),
        out_shape=jax.ShapeDtypeStruct((m, ne * f), BF16),
        compiler_params=_cparams("parallel", "parallel"),
        name="moe_up",
    )(x, wg, wu, gate)


def _blockdiag_body(x_ref, w_ref, o_ref):
    o_ref[...] = _dot(x_ref[...], w_ref[...]).astype(o_ref.dtype)


def blockdiag_matmul(x, w, *, tm, row_block, col_block0, col_stride, out_dtype=BF16):
    nh, ki, no = w.shape
    return pl.pallas_call(
        _blockdiag_body,
        grid=(nh,),
        in_specs=[pl.BlockSpec((tm, ki), lambda h: (row_block, col_block0 + h * col_stride)),
                  pl.BlockSpec((None, ki, no), lambda h: (h, 0, 0))],
        out_specs=pl.BlockSpec((tm, no), lambda h: (0, h)),
        out_shape=jax.ShapeDtypeStruct((tm, nh * no), out_dtype),
        compiler_params=_cparams("parallel"),
        name="blockdiag_matmul",
    )(x, w)


def _rope_slab(x, cos, sin):
    half = QK_ROPE // 2
    lane = lax.broadcasted_iota(jnp.int32, x.shape, 1)
    swapped = jnp.where(lane < half, pltpu.roll(x, LANES - half, 1), pltpu.roll(x, half, 1))
    return x * cos + swapped * sin


def _mla_prep_body(qc_ref, kvc_ref, kpe_ref, gq_ref, gkv_ref, cos_ref, sin_ref, qn_ref, ckv_ref, kpe_out_ref):
    qn_ref[...] = _rms(qc_ref[...], gq_ref[...]).astype(qn_ref.dtype)
    ckv_ref[...] = _rms(kvc_ref[...], gkv_ref[...])
    kpe_out_ref[...] = _rope_slab(kpe_ref[...], cos_ref[...], sin_ref[...])


def mla_prep(proj, g_q, g_kv, cos, sin, *, tm):
    m = proj.shape[0]
    return pl.pallas_call(
        _mla_prep_body,
        grid=(m // tm,),
        in_specs=[pl.BlockSpec((tm, Q_LORA), lambda i: (i, COL_QC // Q_LORA)),
                  pl.BlockSpec((tm, KV_LORA), lambda i: (i, COL_KVC // KV_LORA)),
                  pl.BlockSpec((tm, LANES), lambda i: (i, COL_KPE // LANES)),
                  pl.BlockSpec((1, Q_LORA), lambda i: (0, 0)), pl.BlockSpec((1, KV_LORA), lambda i: (0, 0)),
                  pl.BlockSpec((tm, LANES), lambda i: (i, 0)), pl.BlockSpec((tm, LANES), lambda i: (i, 0))],
        out_specs=[pl.BlockSpec((tm, Q_LORA), lambda i: (i, 0)), pl.BlockSpec((tm, KV_LORA), lambda i: (i, 0)),
                   pl.BlockSpec((tm, LANES), lambda i: (i, 0))],
        out_shape=[jax.ShapeDtypeStruct((m, Q_LORA), BF16), jax.ShapeDtypeStruct((m, KV_LORA), F32),
                   jax.ShapeDtypeStruct((m, LANES), F32)],
        compiler_params=_cparams("parallel"),
        name="mla_prep",
    )(proj, proj, proj, g_q.reshape(1, -1), g_kv.reshape(1, -1), cos, sin)


def _qproj_body(x_ref, w_ref, cos_ref, sin_ref, o_ref):
    acc = _dot(x_ref[...], w_ref[...])
    cos = cos_ref[...] * ATTN_SCALE
    sin = sin_ref[...] * ATTN_SCALE
    for h in range(MLA_HEADS):
        base = h * Q_SLOT
        o_ref[:, base:base + QK_NOPE] = (acc[:, base:base + QK_NOPE] * ATTN_SCALE).astype(o_ref.dtype)
        o_ref[:, base + QK_NOPE:base + Q_SLOT] = _rope_slab(acc[:, base + QK_NOPE:base + Q_SLOT], cos, sin).astype(o_ref.dtype)


def qproj(qn, w_q_slots, cos, sin, *, tm):
    m = qn.shape[0]
    n = w_q_slots.shape[1]
    return pl.pallas_call(
        _qproj_body,
        grid=(m // tm,),
        in_specs=[pl.BlockSpec((tm, Q_LORA), lambda i: (i, 0)), pl.BlockSpec((Q_LORA, n), lambda i: (0, 0)),
                  pl.BlockSpec((tm, LANES), lambda i: (i, 0)), pl.BlockSpec((tm, LANES), lambda i: (i, 0))],
        out_specs=pl.BlockSpec((tm, n), lambda i: (i, 0)),
        out_shape=jax.ShapeDtypeStruct((m, n), BF16),
        compiler_params=_cparams("parallel"),
        name="qproj",
    )(qn, w_q_slots, cos, sin)


def _kv_expand_body(ckv_ref, kpe_ref, w_ref, k_ref, v_ref):
    acc = _dot(ckv_ref[...].astype(BF16), w_ref[...])
    kpe = kpe_ref[...].astype(k_ref.dtype)
    for h in range(MLA_HEADS):
        base = h * Q_SLOT
        k_ref[:, base:base + QK_NOPE] = acc[:, h * QK_NOPE:(h + 1) * QK_NOPE].astype(k_ref.dtype)
        k_ref[:, base + QK_NOPE:base + Q_SLOT] = kpe
    v_ref[...] = acc[:, MLA_HEADS * QK_NOPE:].astype(v_ref.dtype)


def kv_expand(ckv, kpe, w_kv_flat, *, rows, tm):
    n = w_kv_flat.shape[1]
    return pl.pallas_call(
        _kv_expand_body,
        grid=(rows // tm,),
        in_specs=[pl.BlockSpec((tm, KV_LORA), lambda i: (i, 0)), pl.BlockSpec((tm, LANES), lambda i: (i, 0)),
                  pl.BlockSpec((KV_LORA, n), lambda i: (0, 0))],
        out_specs=[pl.BlockSpec((tm, MLA_HEADS * Q_SLOT), lambda i: (i, 0)),
                   pl.BlockSpec((tm, MLA_HEADS * V_HEAD), lambda i: (i, 0))],
        out_shape=[jax.ShapeDtypeStruct((rows, MLA_HEADS * Q_SLOT), BF16),
                   jax.ShapeDtypeStruct((rows, MLA_HEADS * V_HEAD), BF16)],
        compiler_params=_cparams("parallel"),
        name="kv_expand",
    )(ckv, kpe, w_kv_flat)


def _flash_body(q_ref, k_ref, v_ref, o_ref, *, blk):
    qi = pl.program_id(2)
    q = q_ref[...]

    def step(j, carry, diagonal):
        m, l, acc = carry
        start = pl.multiple_of(j * blk, blk)
        k = k_ref[pl.ds(start, blk), :]
        v = v_ref[pl.ds(start, blk), :]
        s = _dot_nt(q, k)
        if diagonal:
            row = lax.broadcasted_iota(jnp.int32, s.shape, 0)
            col = lax.broadcasted_iota(jnp.int32, s.shape, 1)
            s = jnp.where(row >= col, s, -jnp.inf)
        m_new = jnp.maximum(m, jnp.max(s, axis=1, keepdims=True))
        alpha = jnp.exp(m - m_new)
        p = jnp.exp(s - m_new)
        l = alpha * l + jnp.sum(p, axis=1, keepdims=True)
        acc = alpha * acc + _dot(p.astype(BF16), v)
        return m_new, l, acc

    init = (jnp.full((blk, 1), -jnp.inf, F32), jnp.zeros((blk, 1), F32), jnp.zeros((blk, V_HEAD), F32))
    carry = lax.fori_loop(0, qi, lambda j, c: step(j, c, False), init)
    _, l, acc = step(qi, carry, True)
    o_ref[...] = (acc / l).astype(o_ref.dtype)


def flash_attention(q, k, v, *, batch, seq, blk):
    nq = seq // blk
    return pl.pallas_call(
        functools.partial(_flash_body, blk=blk),
        grid=(batch, MLA_HEADS, nq),
        in_specs=[pl.BlockSpec((blk, Q_SLOT), lambda b, h, i: (b * nq + i, h)),
                  pl.BlockSpec((seq, Q_SLOT), lambda b, h, i: (b, h)),
                  pl.BlockSpec((seq, V_HEAD), lambda b, h, i: (b, h))],
        out_specs=pl.BlockSpec((blk, V_HEAD), lambda b, h, i: (b * nq + i, h)),
        out_shape=jax.ShapeDtypeStruct((batch * seq, MLA_HEADS * V_HEAD), BF16),
        compiler_params=_cparams("parallel", "parallel", "parallel"),
        name="flash_attention",
    )(q, k, v)


def _decode_body(pt_ref, q_ref, qpe_ref, *refs, pages_per_step, n_steps, n_new):
    del pt_ref
    pps = pages_per_step
    ckv_refs = refs[:pps]
    kpe_refs = refs[pps:2 * pps]
    newc_ref, newk_ref, o_ref, m_sc, l_sc, acc_sc = refs[2 * pps:]
    c = pl.program_id(1)

    @pl.when(c == 0)
    def _():
        m_sc[...] = jnp.full(m_sc.shape, -jnp.inf, F32)
        l_sc[...] = jnp.zeros(l_sc.shape, F32)
        acc_sc[...] = jnp.zeros(acc_sc.shape, F32)

    q = q_ref[...]
    qp = qpe_ref[:, :QK_ROPE]
    ks, ss = [], []
    for i in range(pps):
        k = ckv_refs[i][...].astype(BF16)
        kp = kpe_refs[i][...].astype(BF16)
        ks.append(k)
        ss.append(_dot_nt(q, k) + _dot_nt(qp, kp))
    s = jnp.concatenate(ss, axis=1)
    m_prev = m_sc[:, :1]
    l_prev = l_sc[:, :1]
    m_new = jnp.maximum(m_prev, jnp.max(s, axis=1, keepdims=True))
    alpha = jnp.exp(m_prev - m_new)
    p = jnp.exp(s - m_new)
    l_new = alpha * l_prev + jnp.sum(p, axis=1, keepdims=True)
    page = ks[0].shape[0]
    pv = _dot(p[:, :page].astype(BF16), ks[0])
    for i in range(1, pps):
        pv = pv + _dot(p[:, i * page:(i + 1) * page].astype(BF16), ks[i])
    acc_new = alpha * acc_sc[...] + pv
    m_sc[...] = jnp.broadcast_to(m_new, m_sc.shape)
    l_sc[...] = jnp.broadcast_to(l_new, l_sc.shape)
    acc_sc[...] = acc_new

    @pl.when(c == n_steps - 1)
    def _():
        qf = q.astype(F32)
        qpf = qp.astype(F32)
        kn = newc_ref[...]
        kpn = newk_ref[:, :QK_ROPE]
        row = lax.broadcasted_iota(jnp.int32, (q.shape[0], 1), 0)
        sj = []
        for j in range(n_new):
            v = (jnp.sum(qf * kn[j:j + 1, :], axis=1, keepdims=True)
                 + jnp.sum(qpf * kpn[j:j + 1, :], axis=1, keepdims=True))
            sj.append(jnp.where(row >= j * MLA_HEADS, v, -jnp.inf))
        m_fin = m_new
        for v in sj:
            m_fin = jnp.maximum(m_fin, v)
        a2 = jnp.exp(m_new - m_fin)
        l_fin = a2 * l_new
        acc_fin = a2 * acc_new
        for j in range(n_new):
            pj = jnp.exp(sj[j] - m_fin)
            l_fin = l_fin + pj
            acc_fin = acc_fin + pj * kn[j:j + 1, :]
        o_ref[...] = (acc_fin / l_fin).astype(o_ref.dtype)


def decode_attention(q_lat, q_pe, cache_ckv, cache_kpe, layer, page_table, new_ckv, new_kpe, *, pages_per_step):
    nb, rows, _ = q_lat.shape
    n_pages = page_table.shape[1]
    page = cache_ckv.shape[2]
    pps = pages_per_step
    n_steps = n_pages // pps
    n_new = rows // MLA_HEADS

    def page_spec(width, i):
        return pl.BlockSpec((None, None, page, width),
                            lambda b, c, pt: (layer, pt[b * n_pages + c * pps + i], 0, 0))

    in_specs = ([pl.BlockSpec((None, rows, KV_LORA), lambda b, c, pt: (b, 0, 0)),
                 pl.BlockSpec((None, rows, LANES), lambda b, c, pt: (b, 0, 0))]
                + [page_spec(KV_LORA, i) for i in range(pps)]
                + [page_spec(QK_ROPE, i) for i in range(pps)]
                + [pl.BlockSpec((None, 8, KV_LORA), lambda b, c, pt: (b, 0, 0)),
                   pl.BlockSpec((None, 8, LANES), lambda b, c, pt: (b, 0, 0))])
    grid_spec = pltpu.PrefetchScalarGridSpec(
        num_scalar_prefetch=1,
        grid=(nb, n_steps),
        in_specs=in_specs,
        out_specs=pl.BlockSpec((None, rows, KV_LORA), lambda b, c, pt: (b, 0, 0)),
        scratch_shapes=[pltpu.VMEM((rows, LANES), F32), pltpu.VMEM((rows, LANES), F32),
                        pltpu.VMEM((rows, KV_LORA), F32)],
    )
    return pl.pallas_call(
        functools.partial(_decode_body, pages_per_step=pps, n_steps=n_steps, n_new=n_new),
        grid_spec=grid_spec,
        out_shape=jax.ShapeDtypeStruct((nb, rows, KV_LORA), BF16),
        compiler_params=_cparams("parallel", "arbitrary"),
        name="decode_attention",
    )(page_table.reshape(-1), q_lat, q_pe, *([cache_ckv] * pps), *([cache_kpe] * pps), new_ckv, new_kpe)


def _group_rmsnorm(y, g):
    gw = SSM_INNER // SSM_GROUPS
    parts = []
    for i in range(SSM_GROUPS):
        yg = y[:, i * gw:(i + 1) * gw]
        parts.append(yg * lax.rsqrt(jnp.mean(yg * yg, axis=-1, keepdims=True) + EPS))
    return jnp.concatenate(parts, axis=1) * g


def _ssd_state_update(st, bm, xd, acs_x):
    last = acs_x.shape[0] - 1
    xde = (xd * jnp.exp(acs_x[last:last + 1, :] - acs_x)).astype(BF16)
    gw = SSM_INNER // SSM_GROUPS
    upd = [_dot_tn(bm[:, g * SSM_STATE:(g + 1) * SSM_STATE].astype(BF16), xde[:, g * gw:(g + 1) * gw])
           for g in range(SSM_GROUPS)]
    return st * jnp.exp(acs_x[last:last + 1, :]) + jnp.concatenate(upd, axis=1)


def _ssd_y_off(st, cm, acs_x):
    gw = SSM_INNER // SSM_GROUPS
    parts = [_dot(cm[:, g * SSM_STATE:(g + 1) * SSM_STATE].astype(BF16), st[:, g * gw:(g + 1) * gw].astype(BF16))
             for g in range(SSM_GROUPS)]
    return jnp.concatenate(parts, axis=1) * jnp.exp(acs_x)


def _ssd_prompt_body(z_ref, x_ref, bc_ref, dt_ref, cw_ref, cb_ref, dtb_ref, alog_ref, alogx_ref, dvec_ref, gn_ref,
                     e_ref, tril_ref, y_ref, st_ref, xp_sc, st_sc, *, n_chunks):
    L = SSM_CHUNK
    c = pl.program_id(1)

    @pl.when(c == 0)
    def _():
        xp_sc[0:8, :] = jnp.zeros((8, xp_sc.shape[1]), F32)
        st_sc[...] = jnp.zeros(st_sc.shape, F32)

    @pl.when(c > 0)
    def _():
        xp_sc[0:8, :] = xp_sc[L:L + 8, :]

    xp_sc[8:8 + L, 0:SSM_INNER] = x_ref[...]
    xp_sc[8:8 + L, SSM_INNER:] = bc_ref[...]
    conv = xp_sc[pl.ds(8 - (SSM_CONV - 1), L), :] * cw_ref[0:1, :]
    for k in range(1, SSM_CONV):
        conv = conv + xp_sc[pl.ds(8 - (SSM_CONV - 1) + k, L), :] * cw_ref[k:k + 1, :]
    xbc = _silu(conv + cb_ref[...])
    xs = xbc[:, :SSM_INNER]
    bm = xbc[:, SSM_INNER:SSM_INNER + SSM_GROUPS * SSM_STATE]
    cm = xbc[:, SSM_INNER + SSM_GROUPS * SSM_STATE:]

    hi = lax.Precision.HIGHEST
    dt = _softplus(dt_ref[...] + dtb_ref[...])
    tril = tril_ref[...]
    acs = _dot(tril, dt * (-jnp.exp(alog_ref[...])), precision=hi)
    acs_t = acs.T
    dt_x = _dot(dt, e_ref[...], precision=hi)
    acs_x = _dot(tril, dt_x * (-jnp.exp(alogx_ref[...])), precision=hi)
    xd = xs * dt_x
    xd_b = xd.astype(BF16)

    row = lax.broadcasted_iota(jnp.int32, (L, L), 0)
    col = lax.broadcasted_iota(jnp.int32, (L, L), 1)
    causal = row >= col
    lane = lax.broadcasted_iota(jnp.int32, (L, LANES), 1)
    heads_per_group = SSM_HEADS // SSM_GROUPS
    y_parts = []
    cb = [_dot_nt(cm[:, g * SSM_STATE:(g + 1) * SSM_STATE].astype(BF16),
                  bm[:, g * SSM_STATE:(g + 1) * SSM_STATE].astype(BF16)) for g in range(SSM_GROUPS)]
    for pair in range(SSM_HEADS // 2):
        xd_pair = xd_b[:, pair * LANES:(pair + 1) * LANES]
        outs = []
        for h in (2 * pair, 2 * pair + 1):
            decay = jnp.exp(jnp.where(causal, acs[:, h:h + 1] - acs_t[h:h + 1, :], -jnp.inf))
            outs.append(_dot((cb[h // heads_per_group] * decay).astype(BF16), xd_pair))
        y_parts.append(jnp.where(lane < SSM_HEAD_DIM, outs[0], outs[1]))
    y_diag = jnp.concatenate(y_parts, axis=1)

    st = st_sc[...]
    y = (y_diag + _ssd_y_off(st, cm, acs_x)) + dvec_ref[...] * xs
    y = y * _silu(z_ref[...])
    y_ref[...] = _group_rmsnorm(y, gn_ref[...]).astype(y_ref.dtype)
    st_new = _ssd_state_update(st, bm, xd, acs_x)
    st_sc[...] = st_new

    @pl.when(c == n_chunks - 1)
    def _():
        st_ref[...] = st_new.T


def _ssd_consts(conv_w, conv_b, dt_bias, a_log, d_vec, g_norm):
    pad = LANES - SSM_HEADS
    e_np = np.zeros((LANES, SSM_INNER), np.float32)
    for hh in range(SSM_HEADS):
        e_np[hh, hh * SSM_HEAD_DIM:(hh + 1) * SSM_HEAD_DIM] = 1.0
    e_mat = jnp.asarray(e_np)
    return dict(
        cw=conv_w, cb=conv_b.reshape(1, -1),
        dtb=jnp.pad(dt_bias, (0, pad)).reshape(1, LANES),
        alog=jnp.pad(a_log, (0, pad)).reshape(1, LANES),
        alogx=jnp.repeat(a_log, SSM_HEAD_DIM).reshape(1, SSM_INNER),
        dvec=jnp.repeat(d_vec, SSM_HEAD_DIM).reshape(1, SSM_INNER),
        gn=g_norm.reshape(1, SSM_INNER), e=e_mat)


def _full(shape):
    nd = len(shape)
    return pl.BlockSpec(shape, lambda *_: (0,) * nd)


def ssd_prompt(proj, consts, *, batch, seq):
    L = SSM_CHUNK
    nc = seq // L
    cdim = SSM_INNER + 2 * SSM_GROUPS * SSM_STATE
    tril = jnp.asarray(np.tril(np.ones((L, L), np.float32)))

    def rows(width, col):
        return pl.BlockSpec((L, width), lambda b, c: (b * nc + c, col // width))

    k = consts
    return pl.pallas_call(
        functools.partial(_ssd_prompt_body, n_chunks=nc),
        grid=(batch, nc),
        in_specs=[rows(SSM_INNER, COL_Z), rows(SSM_INNER, COL_X), rows(2 * SSM_GROUPS * SSM_STATE, COL_BC),
                  rows(LANES, COL_DT),
                  _full((SSM_CONV, cdim)), _full((1, cdim)), _full((1, LANES)), _full((1, LANES)),
                  _full((1, SSM_INNER)), _full((1, SSM_INNER)), _full((1, SSM_INNER)),
                  _full((LANES, SSM_INNER)), _full((L, L))],
        out_specs=[pl.BlockSpec((L, SSM_INNER), lambda b, c: (b * nc + c, 0)),
                   pl.BlockSpec((None, SSM_INNER, SSM_STATE), lambda b, c: (b, 0, 0))],
        out_shape=[jax.ShapeDtypeStruct((batch * seq, SSM_INNER), BF16),
                   jax.ShapeDtypeStruct((batch, SSM_INNER, SSM_STATE), F32)],
        scratch_shapes=[pltpu.VMEM((L + 8, cdim), F32), pltpu.VMEM((SSM_STATE, SSM_INNER), F32)],
        compiler_params=_cparams("parallel", "arbitrary"),
        name="ssd_prompt",
    )(proj, proj, proj, proj, k["cw"], k["cb"], k["dtb"], k["alog"], k["alogx"], k["dvec"], k["gn"], k["e"], tril)


def _sample_mixer_body(z_ref, dt_ref, scb_ref, xp_ref, up_c_ref, up_v_ref, st_in_ref,
                       cw_ref, cb_ref, dtb_ref, alogx_ref, dvec_ref, gn_ref, e_ref, scw_ref,
                       y_ref, ysc_ref, st_ref, u_out_ref, bpad_sc, xdpad_sc, u_sc, *, n_new):
    R = 8
    b = pl.program_id(0)

    @pl.when(b == 0)
    def _():
        bpad_sc[...] = jnp.zeros(bpad_sc.shape, F32)
        xdpad_sc[...] = jnp.zeros(xdpad_sc.shape, F32)

    conv = xp_ref[pl.ds(8 - (SSM_CONV - 1), R), :] * cw_ref[0:1, :]
    for k in range(1, SSM_CONV):
        conv = conv + xp_ref[pl.ds(8 - (SSM_CONV - 1) + k, R), :] * cw_ref[k:k + 1, :]
    xbc = _silu(conv + cb_ref[...])
    xs = xbc[:, :SSM_INNER]
    bm = xbc[:, SSM_INNER:SSM_INNER + SSM_GROUPS * SSM_STATE]
    cm = xbc[:, SSM_INNER + SSM_GROUPS * SSM_STATE:]

    hi = lax.Precision.HIGHEST
    rowl = lax.broadcasted_iota(jnp.int32, (R, LANES), 0)
    dt = jnp.where(rowl < n_new, _softplus(dt_ref[...] + dtb_ref[...]), 0.0)
    dt_x = _dot(dt, e_ref[...], precision=hi)
    da_x = dt_x * (-jnp.exp(alogx_ref[...]))
    rowx = lax.broadcasted_iota(jnp.int32, (R, SSM_INNER), 0)
    acs_x = jnp.zeros((R, SSM_INNER), F32)
    for s in range(n_new):
        acs_x = acs_x + jnp.where(rowx >= s, da_x[s:s + 1, :], 0.0)
    xd = xs * dt_x

    gw = SSM_INNER // SSM_GROUPS
    lanex = lax.broadcasted_iota(jnp.int32, (R, SSM_INNER), 1)
    y_diag = jnp.zeros((R, SSM_INNER), F32)
    for s in range(n_new):
        cbs = [jnp.sum(cm[:, g * SSM_STATE:(g + 1) * SSM_STATE] * bm[s:s + 1, g * SSM_STATE:(g + 1) * SSM_STATE],
                       axis=1, keepdims=True) for g in range(SSM_GROUPS)]
        cb_x = jnp.where(lanex < gw, cbs[0], cbs[1])
        decay = jnp.exp(jnp.where(rowx >= s, acs_x - acs_x[s:s + 1, :], -jnp.inf))
        y_diag = y_diag + (cb_x * decay) * xd[s:s + 1, :]

    st = st_in_ref[...].T
    y_off_parts = [_dot(cm[:, g * SSM_STATE:(g + 1) * SSM_STATE], st[:, g * gw:(g + 1) * gw])
                   for g in range(SSM_GROUPS)]
    y_off = jnp.concatenate(y_off_parts, axis=1) * jnp.exp(acs_x)
    y = (y_diag + y_off) + dvec_ref[...] * xs
    y = y * _silu(z_ref[...])
    y_ref[...] = _group_rmsnorm(y, gn_ref[...])

    bpad_sc[0:R, :] = bm
    xdpad_sc[0:R, :] = xd * jnp.exp(acs_x[R - 1:R, :] - acs_x)
    upd = [_dot_tn(bpad_sc[:, g * SSM_STATE:(g + 1) * SSM_STATE], xdpad_sc[:, g * gw:(g + 1) * gw])
           for g in range(SSM_GROUPS)]
    st_new = st * jnp.exp(acs_x[R - 1:R, :]) + jnp.concatenate(upd, axis=1)
    st_ref[...] = st_new.T

    u_sc[...] = up_c_ref[...] * up_v_ref[...]
    sconv = u_sc[pl.ds(8 - (SC_CONV - 1), R), :] * scw_ref[0:1, :]
    for k in range(1, SC_CONV):
        sconv = sconv + u_sc[pl.ds(8 - (SC_CONV - 1) + k, R), :] * scw_ref[k:k + 1, :]
    ysc_ref[...] = scb_ref[...] * sconv
    u_out_ref[...] = u_sc[8:16, :]


def sample_mixer(proj_s, xp_s, up_c, up_v, st_in, consts, sc_conv_w, *, n_new):
    nb = proj_s.shape[0]
    cdim = SSM_INNER + 2 * SSM_GROUPS * SSM_STATE
    k = consts

    def rows(width, col):
        return pl.BlockSpec((None, 8, width), lambda b: (b, 0, col // width))

    return pl.pallas_call(
        functools.partial(_sample_mixer_body, n_new=n_new),
        grid=(nb,),
        in_specs=[rows(SSM_INNER, COL_Z), rows(LANES, COL_DT), rows(SC_WIDTH, COL_SCB),
                  pl.BlockSpec((None, 16, cdim), lambda b: (b, 0, 0)),
                  pl.BlockSpec((None, 16, SC_WIDTH), lambda b: (b, 0, 0)),
                  pl.BlockSpec((None, 16, SC_WIDTH), lambda b: (b, 0, 0)),
                  pl.BlockSpec((None, SSM_INNER, SSM_STATE), lambda b: (b, 0, 0)),
                  _full((SSM_CONV, cdim)), _full((1, cdim)), _full((1, LANES)),
                  _full((1, SSM_INNER)), _full((1, SSM_INNER)), _full((1, SSM_INNER)),
                  _full((LANES, SSM_INNER)), _full((SC_CONV, SC_WIDTH))],
        out_specs=[pl.BlockSpec((None, 8, SSM_INNER), lambda b: (b, 0, 0)),
                   pl.BlockSpec((None, 8, SC_WIDTH), lambda b: (b, 0, 0)),
                   pl.BlockSpec((None, SSM_INNER, SSM_STATE), lambda b: (b, 0, 0)),
                   pl.BlockSpec((None, 8, SC_WIDTH), lambda b: (b, 0, 0))],
        out_shape=[jax.ShapeDtypeStruct((nb, 8, SSM_INNER), F32), jax.ShapeDtypeStruct((nb, 8, SC_WIDTH), F32),
                   jax.ShapeDtypeStruct((nb, SSM_INNER, SSM_STATE), F32),
                   jax.ShapeDtypeStruct((nb, 8, SC_WIDTH), F32)],
        scratch_shapes=[pltpu.VMEM((LANES, SSM_GROUPS * SSM_STATE), F32), pltpu.VMEM((LANES, SSM_INNER), F32),
                        pltpu.VMEM((16, SC_WIDTH), F32)],
        compiler_params=_cparams("arbitrary"),
        name="sample_mixer",
    )(proj_s, proj_s, proj_s, xp_s, up_c, up_v, st_in,
      k["cw"], k["cb"], k["dtb"], k["alogx"], k["dvec"], k["gn"], k["e"], sc_conv_w)


def _sconv_prompt_body(scb_ref, scc_ref, scv_ref, pc_ref, pv_ref, w_ref, y_ref, tail_ref, u_sc, *, ts):
    i = pl.program_id(1)
    hist = pc_ref[...] * pv_ref[...]
    u_sc[0:8, :] = jnp.where(i == 0, 0.0, hist)
    u_sc[8:8 + ts, :] = scc_ref[...] * scv_ref[...]
    conv = u_sc[pl.ds(8 - (SC_CONV - 1), ts), :] * w_ref[0:1, :]
    for k in range(1, SC_CONV):
        conv = conv + u_sc[pl.ds(8 - (SC_CONV - 1) + k, ts), :] * w_ref[k:k + 1, :]
    y_ref[...] = (scb_ref[...] * conv).astype(y_ref.dtype)
    tail_ref[...] = u_sc[ts:ts + 8, :]


def sconv_prompt(proj, sc_conv_w, *, batch, seq, ts):
    nt = seq // ts
    w = SC_WIDTH

    def rows(col):
        return pl.BlockSpec((ts, w), lambda b, i: (b * nt + i, col // w))

    def prev(col):
        return pl.BlockSpec((8, w), lambda b, i: (jnp.maximum((b * nt + i) * (ts // 8) - 1, 0), col // w))

    return pl.pallas_call(
        functools.partial(_sconv_prompt_body, ts=ts),
        grid=(batch, nt),
        in_specs=[rows(COL_SCB), rows(COL_SCC), rows(COL_SCV), prev(COL_SCC), prev(COL_SCV), _full((SC_CONV, w))],
        out_specs=[pl.BlockSpec((ts, w), lambda b, i: (b * nt + i, 0)),
                   pl.BlockSpec((None, 8, w), lambda b, i: (b, 0, 0))],
        out_shape=[jax.ShapeDtypeStruct((batch * seq, w), BF16), jax.ShapeDtypeStruct((batch, 8, w), F32)],
        scratch_shapes=[pltpu.VMEM((ts + 8, w), F32)],
        compiler_params=_cparams("parallel", "arbitrary"),
        name="sconv_prompt",
    )(proj, proj, proj, proj, proj, sc_conv_w)


def _merge_body(h_ref, ya_ref, ys_ref, yc_ref, wga_ref, wgb_ref, wgc_ref, ba_ref, bb_ref, bc_ref,
                wa_ref, wb_ref, wc_ref, o_ref):
    h = h_ref[...]
    out = _sigmoid(_dot(h, wga_ref[...]) + ba_ref[...]) * _dot(ya_ref[...], wa_ref[...])
    out = out + _sigmoid(_dot(h, wgb_ref[...]) + bb_ref[...]) * _dot(ys_ref[...], wb_ref[...])
    out = out + _sigmoid(_dot(h, wgc_ref[...]) + bc_ref[...]) * _dot(yc_ref[...], wc_ref[...])
    o_ref[...] = out.astype(o_ref.dtype)


def gated_merge(h, ya, ys, yc, w_gate, b_gate, wa, wb, wc, *, tm, tn):
    m, d = h.shape
    kb = ya.shape[1]
    nb = d // tn
    b2 = b_gate.reshape(1, -1)

    def act(width):
        return pl.BlockSpec((tm, width), lambda j, i: (i, 0))

    def wcol(rows_, off):
        return pl.BlockSpec((rows_, tn), lambda j, i: (0, off * nb + j))

    return pl.pallas_call(
        _merge_body,
        grid=(nb, m // tm),
        in_specs=[act(d), act(kb), act(kb), act(kb),
                  wcol(d, 0), wcol(d, 1), wcol(d, 2), wcol(1, 0), wcol(1, 1), wcol(1, 2),
                  wcol(kb, 0), wcol(kb, 0), wcol(kb, 0)],
        out_specs=pl.BlockSpec((tm, tn), lambda j, i: (i, j)),
        out_shape=jax.ShapeDtypeStruct((m, d), BF16),
        compiler_params=_cparams("parallel", "parallel"),
        name="gated_merge",
    )(h, ya, ys, yc, w_gate, w_gate, w_gate, b2, b2, b2, wa, wb, wc)


def _rope_tables(pos):
    half = QK_ROPE // 2
    inv = ROPE_THETA ** (-jnp.arange(half, dtype=F32) / half)
    ang = pos.astype(F32)[:, None] * inv[None, :]
    c, s = jnp.cos(ang), jnp.sin(ang)
    z = jnp.zeros((pos.shape[0], LANES - QK_ROPE), F32)
    return jnp.concatenate([c, c, z], axis=1), jnp.concatenate([-s, s, z], axis=1)


def _reorder_w_in(w):
    sizes = (Q_LORA, KV_LORA, QK_ROPE, SSM_INNER, SSM_INNER + 2 * SSM_GROUPS * SSM_STATE, SSM_HEADS,
             SC_WIDTH, SC_WIDTH, SC_WIDTH)
    q_c, kv_c, k_pe, z, xbc, dt, sc_b, sc_c, sc_v = jnp.split(w, np.cumsum(sizes)[:-1].tolist(), axis=1)
    d = w.shape[0]
    out = jnp.concatenate([sc_b, sc_c, sc_v, z, xbc, q_c, kv_c,
                           k_pe, jnp.zeros((d, LANES - QK_ROPE), w.dtype),
                           dt, jnp.zeros((d, LANES - SSM_HEADS), w.dtype)], axis=1)
    return out.astype(BF16)


def kernel(x_prompt, x_sample, cache_ckv, cache_kpe, state_ssm, state_mconv, state_sconv, page_table, g_attn_norm, w_in, g_q_a, w_q_b, g_kv_a, w_kv_b, ssm_conv_w, ssm_conv_b, ssm_dt_bias, ssm_a_log, ssm_d, g_ssm_norm, sc_conv_w, w_gate, b_gate, w_br_attn, w_br_ssm, w_br_sc, w_o, g_ffn_norm, w_ff_gate, w_ff_up, w_ff_down, w_router, w_e_gate, w_e_up, w_e_down, g_final):
    bp, tp, d = x_prompt.shape
    bs, ts, _ = x_sample.shape
    depth = w_in.shape[0]
    mp, ms = bp * tp, bs * ts
    m = mp + ms
    n_past = page_table.shape[1] * cache_ckv.shape[2]
    tm = m // 8
    tmh = m // 16
    cdim = SSM_INNER + 2 * SSM_GROUPS * SSM_STATE

    x = jnp.concatenate([x_prompt.reshape(mp, d), x_sample.reshape(ms, d)], axis=0)
    pos = jnp.concatenate([jnp.tile(jnp.arange(tp), bp), jnp.tile(n_past + jnp.arange(ts), bs)])
    cos, sin = _rope_tables(pos)

    outs = {k: [] for k in ("p_ckv", "p_kpe", "p_ssm", "p_mconv", "p_sconv", "s_ckv", "s_kpe", "s_ssm", "s_mconv", "s_sconv")}
    for l in range(depth):
        w_in_l = _reorder_w_in(w_in[l])
        wq = w_q_b[l]
        w_q_slots = jnp.concatenate([wq, jnp.zeros(wq.shape[:2] + (Q_SLOT - wq.shape[2],), wq.dtype)], axis=2)
        w_q_slots = w_q_slots.reshape(Q_LORA, MLA_HEADS * Q_SLOT).astype(BF16)
        wkv = w_kv_b[l]
        w_kv_flat = jnp.concatenate([wkv[..., :QK_NOPE].reshape(KV_LORA, -1), wkv[..., QK_NOPE:].reshape(KV_LORA, -1)],
                                    axis=1).astype(BF16)
        w_uk_t = jnp.transpose(wkv[..., :QK_NOPE], (1, 2, 0)).astype(BF16)
        w_uv = jnp.transpose(wkv[..., QK_NOPE:], (1, 0, 2)).astype(BF16)
        consts = _ssd_consts(ssm_conv_w[l], ssm_conv_b[l], ssm_dt_bias[l], ssm_a_log[l], ssm_d[l], g_ssm_norm[l])

        h = rmsnorm(x, g_attn_norm[l], BF16, tm)
        proj = matmul(h, w_in_l, tm=tm, tn=768)
        qn, ckv, kpe = mla_prep(proj, g_q_a[l], g_kv_a[l], cos, sin, tm=tm)
        q = qproj(qn, w_q_slots, cos, sin, tm=tm)

        k_full, v_full = kv_expand(ckv, kpe, w_kv_flat, rows=mp, tm=1024)
        ya_p = flash_attention(q, k_full, v_full, batch=bp, seq=tp, blk=512)

        q_lat = blockdiag_matmul(q, w_uk_t, tm=ms, row_block=mp // ms, col_block0=0, col_stride=2)
        q_pe_s = q[mp:].reshape(ms, MLA_HEADS, Q_SLOT)[:, :, QK_NOPE:].reshape(bs, ts * MLA_HEADS, LANES)
        ckv_s = ckv[mp:].reshape(bs, ts, KV_LORA)
        kpe_s = kpe[mp:].reshape(bs, ts, LANES)
        new_c = jnp.pad(ckv_s, ((0, 0), (0, 8 - ts), (0, 0)))
        new_k = jnp.pad(kpe_s, ((0, 0), (0, 8 - ts), (0, 0)))
        o_lat = decode_attention(q_lat.reshape(bs, ts * MLA_HEADS, KV_LORA), q_pe_s, cache_ckv, cache_kpe, l,
                                 page_table, new_c, new_k, pages_per_step=8)
        ya_s = blockdiag_matmul(o_lat.reshape(ms, MLA_HEADS * KV_LORA), w_uv, tm=ms, row_block=0, col_block0=0,
                                col_stride=1)

        ys_p, ssm_p = ssd_prompt(proj, consts, batch=bp, seq=tp)
        yc_p, u_tail_p = sconv_prompt(proj, sc_conv_w[l], batch=bp, seq=tp, ts=512)

        proj_s = jnp.pad(proj[mp:].reshape(bs, ts, D_IN_PAD), ((0, 0), (0, 8 - ts), (0, 0)))
        xbc_s = proj_s[:, :ts, COL_X:COL_X + cdim]
        xp_s = jnp.concatenate([jnp.zeros((bs, 8 - (SSM_CONV - 1), cdim), F32), state_mconv[l], xbc_s,
                                jnp.zeros((bs, 8 - ts, cdim), F32)], axis=1)
        zpad = jnp.zeros((bs, 8 - (SC_CONV - 1), SC_WIDTH), F32)
        zend = jnp.zeros((bs, 8 - ts, SC_WIDTH), F32)
        up_c = jnp.concatenate([zpad, state_sconv[l], proj_s[:, :ts, COL_SCC:COL_SCC + SC_WIDTH], zend], axis=1)
        up_v = jnp.concatenate([zpad, jnp.ones_like(state_sconv[l]), proj_s[:, :ts, COL_SCV:COL_SCV + SC_WIDTH], zend], axis=1)
        ys_s, yc_s, ssm_s, u_new_s = sample_mixer(proj_s, xp_s, up_c, up_v,
                                                  state_ssm[l].reshape(bs, SSM_INNER, SSM_STATE), consts, sc_conv_w[l],
                                                  n_new=ts)

        ya = jnp.concatenate([ya_p, ya_s], axis=0)
        ys = jnp.concatenate([ys_p, ys_s[:, :ts].reshape(ms, SSM_INNER).astype(BF16)], axis=0)
        yc = jnp.concatenate([yc_p, yc_s[:, :ts].reshape(ms, SC_WIDTH).astype(BF16)], axis=0)
        merged = gated_merge(h, ya, ys, yc, w_gate[l].astype(BF16), b_gate[l], w_br_attn[l].astype(BF16),
                             w_br_ssm[l].astype(BF16), w_br_sc[l].astype(BF16), tm=tmh, tn=512)
        x = matmul(merged, w_o[l].astype(BF16), tm=tm, tn=512, res=x)

        outs["p_ckv"].append(ckv[:mp].reshape(bp, tp, KV_LORA))
        outs["p_kpe"].append(kpe[:mp, :QK_ROPE].reshape(bp, tp, QK_ROPE))
        outs["p_ssm"].append(ssm_p.reshape(bp, SSM_HEADS, SSM_HEAD_DIM, SSM_STATE))
        xbc_p = proj[:mp, COL_X:COL_X + cdim].reshape(bp, tp, cdim)
        outs["p_mconv"].append(xbc_p[:, tp - (SSM_CONV - 1):])
        outs["p_sconv"].append(u_tail_p[:, 8 - (SC_CONV - 1):])
        outs["s_ckv"].append(ckv_s)
        outs["s_kpe"].append(kpe_s[:, :, :QK_ROPE])
        outs["s_ssm"].append(ssm_s.reshape(bs, SSM_HEADS, SSM_HEAD_DIM, SSM_STATE))
        outs["s_mconv"].append(xp_s[:, 8 + ts - (SSM_CONV - 1):8 + ts])
        outs["s_sconv"].append(u_new_s[:, ts - (SC_CONV - 1):ts])

        i = l // 2
        if l % 2 == 0:
            h2 = rmsnorm(x, g_ffn_norm[l], BF16, tm)
            hdn = swiglu_up(h2, w_ff_gate[i].astype(BF16), w_ff_up[i].astype(BF16), tm=tm, tf=512)
            x = matmul(hdn, w_ff_down[i].astype(BF16), tm=tmh, tn=512, tk=w_ff_down.shape[1] // 2, res=x)
        else:
            h2, gate = rmsnorm_router(x, g_ffn_norm[l], w_router[i], tm)
            hdn = moe_up(h2, w_e_gate[i].astype(BF16), w_e_up[i].astype(BF16), gate, tm=tmh)
            wd = w_e_down[i].reshape(-1, d).astype(BF16)
            x = matmul(hdn, wd, tm=tmh, tn=512, tk=wd.shape[0] // 4, res=x)

    y = rmsnorm(x, g_final, F32, tm)
    st = {k: jnp.stack(v, axis=0) for k, v in outs.items()}
    return (y[:mp].reshape(bp, tp, d), y[mp:].reshape(bs, ts, d),
            st["p_ckv"], st["p_kpe"], st["p_ssm"], st["p_mconv"], st["p_sconv"],
            st["s_ckv"], st["s_kpe"], st["s_ssm"], st["s_mconv"], st["s_sconv"])
```

```python
import functools

import jax
import jax.numpy as jnp
import numpy as np
from jax import lax
from jax.experimental import pallas as pl
from jax.experimental.pallas import tpu as pltpu

F32 = jnp.float32
BF16 = jnp.bfloat16
EPS = 1e-6
ROPE_THETA = 10000.0
LANES = 128
MLA_HEADS = 8
QK_NOPE = 128
QK_ROPE = 64
V_HEAD = 128
KV_LORA = 512
Q_LORA = 512
Q_SLOT = 256
SSM_HEADS = 16
SSM_HEAD_DIM = 64
SSM_INNER = 1024
SSM_GROUPS = 2
SSM_STATE = 128
SSM_CONV = 4
SSM_CHUNK = 128
SC_WIDTH = 1024
SC_CONV = 3
N_EXPERTS = 8
ATTN_SCALE = (QK_NOPE + QK_ROPE) ** -0.5
VMEM_LIMIT = 56 * 1024 * 1024

COL_SCB, COL_SCC, COL_SCV, COL_Z, COL_X, COL_BC, COL_QC, COL_KVC, COL_KPE, COL_DT = (
    0, 1024, 2048, 3072, 4096, 5120, 5632, 6144, 6656, 6784)
D_IN_PAD = 6912


def _cparams(*sem):
    return pltpu.CompilerParams(dimension_semantics=sem, vmem_limit_bytes=VMEM_LIMIT)


def _sigmoid(x):
    return 1.0 / (1.0 + jnp.exp(-x))


def _silu(x):
    return x * _sigmoid(x)


def _softplus(x):
    return jnp.maximum(x, 0.0) + jnp.log1p(jnp.exp(-jnp.abs(x)))


def _dot(a, b, **kw):
    return jnp.dot(a, b, preferred_element_type=F32, **kw)


def _dot_nt(a, b):
    return lax.dot_general(a, b, (((1,), (1,)), ((), ())), preferred_element_type=F32)


def _dot_tn(a, b):
    return lax.dot_general(a, b, (((0,), (0,)), ((), ())), preferred_element_type=F32)


def _rms(x, g):
    r = lax.rsqrt(jnp.mean(x * x, axis=-1, keepdims=True) + EPS)
    return (x * r) * g


def _rmsnorm_body(x_ref, g_ref, o_ref):
    o_ref[...] = _rms(x_ref[...], g_ref[...]).astype(o_ref.dtype)


def rmsnorm(x, g, out_dtype, tm):
    m, d = x.shape
    return pl.pallas_call(
        _rmsnorm_body,
        grid=(m // tm,),
        in_specs=[pl.BlockSpec((tm, d), lambda i: (i, 0)), pl.BlockSpec((1, d), lambda i: (0, 0))],
        out_specs=pl.BlockSpec((tm, d), lambda i: (i, 0)),
        out_shape=jax.ShapeDtypeStruct((m, d), out_dtype),
        compiler_params=_cparams("parallel"),
        name="rmsnorm",
    )(x, g.reshape(1, d))


def _rmsnorm_router_body(x_ref, g_ref, wr_ref, h_ref, gate_ref):
    h = _rms(x_ref[...], g_ref[...])
    h_ref[...] = h.astype(h_ref.dtype)
    lg = _dot(h, wr_ref[...], precision=lax.Precision.HIGHEST)
    lane = lax.broadcasted_iota(jnp.int32, lg.shape, 1).astype(F32)
    lg = jnp.where(lane < N_EXPERTS, lg, -jnp.inf)
    m1 = jnp.max(lg, axis=1, keepdims=True)
    i1 = jnp.min(jnp.where(lg == m1, lane, float(LANES)), axis=1, keepdims=True)
    oh1 = lane == i1
    lg2 = jnp.where(oh1, -jnp.inf, lg)
    m2 = jnp.max(lg2, axis=1, keepdims=True)
    i2 = jnp.min(jnp.where(lg2 == m2, lane, float(LANES)), axis=1, keepdims=True)
    oh2 = lane == i2
    e = jnp.exp(m2 - m1)
    w1 = 1.0 / (1.0 + e)
    w2 = e / (1.0 + e)
    gate_ref[...] = jnp.where(oh1, w1, 0.0) + jnp.where(oh2, w2, 0.0)


def rmsnorm_router(x, g, w_router, tm):
    m, d = x.shape
    wr = jnp.pad(w_router, ((0, 0), (0, LANES - w_router.shape[1])))
    return pl.pallas_call(
        _rmsnorm_router_body,
        grid=(m // tm,),
        in_specs=[pl.BlockSpec((tm, d), lambda i: (i, 0)), pl.BlockSpec((1, d), lambda i: (0, 0)),
                  pl.BlockSpec((d, LANES), lambda i: (0, 0))],
        out_specs=[pl.BlockSpec((tm, d), lambda i: (i, 0)), pl.BlockSpec((tm, LANES), lambda i: (i, 0))],
        out_shape=[jax.ShapeDtypeStruct((m, d), BF16), jax.ShapeDtypeStruct((m, LANES), F32)],
        compiler_params=_cparams("parallel"),
        name="rmsnorm_router",
    )(x, g.reshape(1, d), wr)


def _mm_body(x_ref, w_ref, *rest, nk, has_res):
    if has_res:
        r_ref, o_ref = rest
    else:
        (o_ref,) = rest
    acc = _dot(x_ref[...], w_ref[...])
    if nk == 1:
        o_ref[...] = ((r_ref[...] + acc) if has_res else acc).astype(o_ref.dtype)
    else:
        k = pl.program_id(2)

        @pl.when(k == 0)
        def _():
            o_ref[...] = (r_ref[...] + acc) if has_res else acc

        @pl.when(k > 0)
        def _():
            o_ref[...] += acc


def matmul(x, w, *, tm, tn, tk=None, res=None, out_dtype=F32, rows_outer=False):
    m, kd = x.shape
    n = w.shape[1]
    tk = kd if tk is None else tk
    nk = kd // tk
    if nk > 1:
        assert out_dtype == F32

    def ij(a, b):
        return (a, b) if rows_outer else (b, a)

    in_specs = [pl.BlockSpec((tm, tk), lambda a, b, k: (ij(a, b)[0], k)),
                pl.BlockSpec((tk, tn), lambda a, b, k: (k, ij(a, b)[1]))]
    args = [x, w]
    if res is not None:
        in_specs.append(pl.BlockSpec((tm, tn), lambda a, b, k: ij(a, b)))
        args.append(res)
    return pl.pallas_call(
        functools.partial(_mm_body, nk=nk, has_res=res is not None),
        grid=(m // tm, n // tn, nk) if rows_outer else (n // tn, m // tm, nk),
        in_specs=in_specs,
        out_specs=pl.BlockSpec((tm, tn), lambda a, b, k: ij(a, b)),
        out_shape=jax.ShapeDtypeStruct((m, n), out_dtype),
        compiler_params=_cparams("parallel", "parallel", "arbitrary"),
        name="matmul",
    )(*args)


def _swiglu_body(x_ref, wg_ref, wu_ref, *rest, scaled):
    if scaled:
        gate_ref, o_ref = rest
    else:
        (o_ref,) = rest
    x = x_ref[...]
    g = _dot(x, wg_ref[...])
    u = _dot(x, wu_ref[...])
    hdn = _silu(g) * u
    if scaled:
        gate = gate_ref[...]
        lane = lax.broadcasted_iota(jnp.int32, gate.shape, 1)
        sc = jnp.sum(jnp.where(lane == pl.program_id(0), gate, 0.0), axis=1, keepdims=True)
        hdn = hdn * sc
    o_ref[...] = hdn.astype(o_ref.dtype)


def swiglu_up(x, wg, wu, *, tm, tf):
    m, d = x.shape
    f = wg.shape[1]
    return pl.pallas_call(
        functools.partial(_swiglu_body, scaled=False),
        grid=(f // tf, m // tm),
        in_specs=[pl.BlockSpec((tm, d), lambda j, i: (i, 0)), pl.BlockSpec((d, tf), lambda j, i: (0, j)),
                  pl.BlockSpec((d, tf), lambda j, i: (0, j))],
        out_specs=pl.BlockSpec((tm, tf), lambda j, i: (i, j)),
        out_shape=jax.ShapeDtypeStruct((m, f), BF16),
        compiler_params=_cparams("parallel", "parallel"),
        name="swiglu_up",
    )(x, wg, wu)


def moe_up(x, wg, wu, gate, *, tm):
    m, d = x.shape
    ne, _, f = wg.shape
    return pl.pallas_call(
        functools.partial(_swiglu_body, scaled=True),
        grid=(ne, m // tm),
        in_specs=[pl.BlockSpec((tm, d), lambda j, i: (i, 0)), pl.BlockSpec((None, d, f), lambda j, i: (j, 0, 0)),
                  pl.BlockSpec((None, d, f), lambda j, i: (j, 0, 0)), pl.BlockSpec((tm, LANES), lambda j, i: (i, 0))],
        out_specs=pl.BlockSpec((tm, f), lambda j, i: (i, j)),
        out_shape=jax.ShapeDtypeStruct((m, ne * f), BF16),
        compiler_params=_cparams("parallel", "parallel"),
        name="moe_up",
    )(x, wg, wu, gate)


def _blockdiag_body(x_ref, w_ref, o_ref):
    o_ref[...] = _dot(x_ref[...], w_ref[...]).astype(o_ref.dtype)


def blockdiag_matmul(x, w, *, tm, row_block, col_block0, col_stride, out_dtype=BF16):
    nh, ki, no = w.shape
    return pl.pallas_call(
        _blockdiag_body,
        grid=(nh,),
        in_specs=[pl.BlockSpec((tm, ki), lambda h: (row_block, col_block0 + h * col_stride)),
                  pl.BlockSpec((None, ki, no), lambda h: (h, 0, 0))],
        out_specs=pl.BlockSpec((tm, no), lambda h: (0, h)),
        out_shape=jax.ShapeDtypeStruct((tm, nh * no), out_dtype),
        compiler_params=_cparams("parallel"),
        name="blockdiag_matmul",
    )(x, w)


def _rope_slab(x, cos, sin):
    half = QK_ROPE // 2
    lane = lax.broadcasted_iota(jnp.int32, x.shape, 1)
    swapped = jnp.where(lane < half, pltpu.roll(x, LANES - half, 1), pltpu.roll(x, half, 1))
    return x * cos + swapped * sin


def _mla_prep_body(qc_ref, kvc_ref, kpe_ref, gq_ref, gkv_ref, cos_ref, sin_ref, qn_ref, ckv_ref, kpe_out_ref):
    qn_ref[...] = _rms(qc_ref[...], gq_ref[...]).astype(qn_ref.dtype)
    ckv_ref[...] = _rms(kvc_ref[...], gkv_ref[...])
    kpe_out_ref[...] = _rope_slab(kpe_ref[...], cos_ref[...], sin_ref[...])


def mla_prep(proj, g_q, g_kv, cos, sin, *, tm):
    m = proj.shape[0]
    return pl.pallas_call(
        _mla_prep_body,
        grid=(m // tm,),
        in_specs=[pl.BlockSpec((tm, Q_LORA), lambda i: (i, COL_QC // Q_LORA)),
                  pl.BlockSpec((tm, KV_LORA), lambda i: (i, COL_KVC // KV_LORA)),
                  pl.BlockSpec((tm, LANES), lambda i: (i, COL_KPE // LANES)),
                  pl.BlockSpec((1, Q_LORA), lambda i: (0, 0)), pl.BlockSpec((1, KV_LORA), lambda i: (0, 0)),
                  pl.BlockSpec((tm, LANES), lambda i: (i, 0)), pl.BlockSpec((tm, LANES), lambda i: (i, 0))],
        out_specs=[pl.BlockSpec((tm, Q_LORA), lambda i: (i, 0)), pl.BlockSpec((tm, KV_LORA), lambda i: (i, 0)),
                   pl.BlockSpec((tm, LANES), lambda i: (i, 0))],
        out_shape=[jax.ShapeDtypeStruct((m, Q_LORA), BF16), jax.ShapeDtypeStruct((m, KV_LORA), F32),
                   jax.ShapeDtypeStruct((m, LANES), F32)],
        compiler_params=_cparams("parallel"),
        name="mla_prep",
    )(proj, proj, proj, g_q.reshape(1, -1), g_kv.reshape(1, -1), cos, sin)


def _qproj_body(x_ref, w_ref, cos_ref, sin_ref, o_ref):
    acc = _dot(x_ref[...], w_ref[...])
    cos = cos_ref[...] * ATTN_SCALE
    sin = sin_ref[...] * ATTN_SCALE
    for h in range(MLA_HEADS):
        base = h * Q_SLOT
        o_ref[:, base:base + QK_NOPE] = (acc[:, base:base + QK_NOPE] * ATTN_SCALE).astype(o_ref.dtype)
        o_ref[:, base + QK_NOPE:base + Q_SLOT] = _rope_slab(acc[:, base + QK_NOPE:base + Q_SLOT], cos, sin).astype(o_ref.dtype)


def qproj(qn, w_q_slots, cos, sin, *, tm):
    m = qn.shape[0]
    n = w_q_slots.shape[1]
    return pl.pallas_call(
        _qproj_body,
        grid=(m // tm,),
        in_specs=[pl.BlockSpec((tm, Q_LORA), lambda i: (i, 0)), pl.BlockSpec((Q_LORA, n), lambda i: (0, 0)),
                  pl.BlockSpec((tm, LANES), lambda i: (i, 0)), pl.BlockSpec((tm, LANES), lambda i: (i, 0))],
        out_specs=pl.BlockSpec((tm, n), lambda i: (i, 0)),
        out_shape=jax.ShapeDtypeStruct((m, n), BF16),
        compiler_params=_cparams("parallel"),
        name="qproj",
    )(qn, w_q_slots, cos, sin)


def _kv_expand_body(ckv_ref, kpe_ref, w_ref, k_ref, v_ref):
    acc = _dot(ckv_ref[...].astype(BF16), w_ref[...])
    kpe = kpe_ref[...].astype(k_ref.dtype)
    for h in range(MLA_HEADS):
        base = h * Q_SLOT
        k_ref[:, base:base + QK_NOPE] = acc[:, h * QK_NOPE:(h + 1) * QK_NOPE].astype(k_ref.dtype)
        k_ref[:, base + QK_NOPE:base + Q_SLOT] = kpe
    v_ref[...] = acc[:, MLA_HEADS * QK_NOPE:].astype(v_ref.dtype)


def kv_expand(ckv, kpe, w_kv_flat, *, rows, tm):
    n = w_kv_flat.shape[1]
    return pl.pallas_call(
        _kv_expand_body,
        grid=(rows // tm,),
        in_specs=[pl.BlockSpec((tm, KV_LORA), lambda i: (i, 0)), pl.BlockSpec((tm, LANES), lambda i: (i, 0)),
                  pl.BlockSpec((KV_LORA, n), lambda i: (0, 0))],
        out_specs=[pl.BlockSpec((tm, MLA_HEADS * Q_SLOT), lambda i: (i, 0)),
                   pl.BlockSpec((tm, MLA_HEADS * V_HEAD), lambda i: (i, 0))],
        out_shape=[jax.ShapeDtypeStruct((rows, MLA_HEADS * Q_SLOT), BF16),
                   jax.ShapeDtypeStruct((rows, MLA_HEADS * V_HEAD), BF16)],
        compiler_params=_cparams("parallel"),
        name="kv_expand",
    )(ckv, kpe, w_kv_flat)


def _flash_body(q_ref, k_ref, v_ref, o_ref, *, blk):
    qi = pl.program_id(2)
    q = q_ref[...]

    def step(j, carry, diagonal):
        m, l, acc = carry
        start = pl.multiple_of(j * blk, blk)
        k = k_ref[pl.ds(start, blk), :]
        v = v_ref[pl.ds(start, blk), :]
        s = _dot_nt(q, k)
        if diagonal:
            row = lax.broadcasted_iota(jnp.int32, s.shape, 0)
            col = lax.broadcasted_iota(jnp.int32, s.shape, 1)
            s = jnp.where(row >= col, s, -jnp.inf)
        m_new = jnp.maximum(m, jnp.max(s, axis=1, keepdims=True))
        alpha = jnp.exp(m - m_new)
        p = jnp.exp(s - m_new)
        l = alpha * l + jnp.sum(p, axis=1, keepdims=True)
        acc = alpha * acc + _dot(p.astype(BF16), v)
        return m_new, l, acc

    init = (jnp.full((blk, 1), -jnp.inf, F32), jnp.zeros((blk, 1), F32), jnp.zeros((blk, V_HEAD), F32))
    carry = lax.fori_loop(0, qi, lambda j, c: step(j, c, False), init)
    _, l, acc = step(qi, carry, True)
    o_ref[...] = (acc / l).astype(o_ref.dtype)


def flash_attention(q, k, v, *, batch, seq, blk):
    nq = seq // blk
    return pl.pallas_call(
        functools.partial(_flash_body, blk=blk),
        grid=(batch, MLA_HEADS, nq),
        in_specs=[pl.BlockSpec((blk, Q_SLOT), lambda b, h, i: (b * nq + i, h)),
                  pl.BlockSpec((seq, Q_SLOT), lambda b, h, i: (b, h)),
                  pl.BlockSpec((seq, V_HEAD), lambda b, h, i: (b, h))],
        out_specs=pl.BlockSpec((blk, V_HEAD), lambda b, h, i: (b * nq + i, h)),
        out_shape=jax.ShapeDtypeStruct((batch * seq, MLA_HEADS * V_HEAD), BF16),
        compiler_params=_cparams("parallel", "parallel", "parallel"),
        name="flash_attention",
    )(q, k, v)


def _decode_body(pt_ref, q_ref, qpe_ref, *refs, pages_per_step, n_steps, n_new):
    del pt_ref
    pps = pages_per_step
    ckv_refs = refs[:pps]
    kpe_refs = refs[pps:2 * pps]
    newc_ref, newk_ref, o_ref, m_sc, l_sc, acc_sc = refs[2 * pps:]
    c = pl.program_id(1)

    @pl.when(c == 0)
    def _():
        m_sc[...] = jnp.full(m_sc.shape, -jnp.inf, F32)
        l_sc[...] = jnp.zeros(l_sc.shape, F32)
        acc_sc[...] = jnp.zeros(acc_sc.shape, F32)

    q = q_ref[...]
    qp = qpe_ref[:, :QK_ROPE]
    ks, ss = [], []
    for i in range(pps):
        k = ckv_refs[i][...].astype(BF16)
        kp_t = kpe_refs[i][...].astype(BF16)
        ks.append(k)
        ss.append(_dot_nt(q, k) + _dot(qp, kp_t))
    s = jnp.concatenate(ss, axis=1)
    m_prev = m_sc[:, :1]
    l_prev = l_sc[:, :1]
    m_new = jnp.maximum(m_prev, jnp.max(s, axis=1, keepdims=True))
    alpha = jnp.exp(m_prev - m_new)
    p = jnp.exp(s - m_new)
    l_new = alpha * l_prev + jnp.sum(p, axis=1, keepdims=True)
    page = ks[0].shape[0]
    pv = _dot(p[:, :page].astype(BF16), ks[0])
    for i in range(1, pps):
        pv = pv + _dot(p[:, i * page:(i + 1) * page].astype(BF16), ks[i])
    acc_new = alpha * acc_sc[...] + pv
    m_sc[...] = jnp.broadcast_to(m_new, m_sc.shape)
    l_sc[...] = jnp.broadcast_to(l_new, l_sc.shape)
    acc_sc[...] = acc_new

    @pl.when(c == n_steps - 1)
    def _():
        qf = q.astype(F32)
        qpf = qp.astype(F32)
        kn = newc_ref[...]
        kpn = newk_ref[:, :QK_ROPE]
        row = lax.broadcasted_iota(jnp.int32, (q.shape[0], 1), 0)
        sj = []
        for j in range(n_new):
            v = (jnp.sum(qf * kn[j:j + 1, :], axis=1, keepdims=True)
                 + jnp.sum(qpf * kpn[j:j + 1, :], axis=1, keepdims=True))
            sj.append(jnp.where(row >= j * MLA_HEADS, v, -jnp.inf))
        m_fin = m_new
        for v in sj:
            m_fin = jnp.maximum(m_fin, v)
        a2 = jnp.exp(m_new - m_fin)
        l_fin = a2 * l_new
        acc_fin = a2 * acc_new
        for j in range(n_new):
            pj = jnp.exp(sj[j] - m_fin)
            l_fin = l_fin + pj
            acc_fin = acc_fin + pj * kn[j:j + 1, :]
        o_ref[...] = (acc_fin / l_fin).astype(o_ref.dtype)


def decode_attention(q_lat, q_pe, cache_ckv, cache_kpe, layer, page_table, new_ckv, new_kpe, *, pages_per_step):
    nb, rows, _ = q_lat.shape
    n_pages = page_table.shape[1]
    page = cache_ckv.shape[2]
    pps = pages_per_step
    n_steps = n_pages // pps
    n_new = rows // MLA_HEADS

    def page_spec(shape, i):
        return pl.BlockSpec((None, None) + shape, lambda b, c, pt: (layer, pt[b * n_pages + c * pps + i], 0, 0))

    in_specs = ([pl.BlockSpec((None, rows, KV_LORA), lambda b, c, pt: (b, 0, 0)),
                 pl.BlockSpec((None, rows, LANES), lambda b, c, pt: (b, 0, 0))]
                + [page_spec((page, KV_LORA), i) for i in range(pps)]
                + [page_spec((QK_ROPE, page), i) for i in range(pps)]
                + [pl.BlockSpec((None, 8, KV_LORA), lambda b, c, pt: (b, 0, 0)),
                   pl.BlockSpec((None, 8, LANES), lambda b, c, pt: (b, 0, 0))])
    grid_spec = pltpu.PrefetchScalarGridSpec(
        num_scalar_prefetch=1,
        grid=(nb, n_steps),
        in_specs=in_specs,
        out_specs=pl.BlockSpec((None, rows, KV_LORA), lambda b, c, pt: (b, 0, 0)),
        scratch_shapes=[pltpu.VMEM((rows, LANES), F32), pltpu.VMEM((rows, LANES), F32),
                        pltpu.VMEM((rows, KV_LORA), F32)],
    )
    return pl.pallas_call(
        functools.partial(_decode_body, pages_per_step=pps, n_steps=n_steps, n_new=n_new),
        grid_spec=grid_spec,
        out_shape=jax.ShapeDtypeStruct((nb, rows, KV_LORA), BF16),
        compiler_params=_cparams("parallel", "arbitrary"),
        name="decode_attention",
    )(page_table.reshape(-1), q_lat, q_pe, *([cache_ckv] * pps), *([cache_kpe] * pps), new_ckv, new_kpe)


def _group_rmsnorm(y, g):
    gw = SSM_INNER // SSM_GROUPS
    parts = []
    for i in range(SSM_GROUPS):
        yg = y[:, i * gw:(i + 1) * gw]
        parts.append(yg * lax.rsqrt(jnp.mean(yg * yg, axis=-1, keepdims=True) + EPS))
    return jnp.concatenate(parts, axis=1) * g


def _ssd_state_update(st, bm, xd, acs_x):
    last = acs_x.shape[0] - 1
    xde = (xd * jnp.exp(acs_x[last:last + 1, :] - acs_x)).astype(BF16)
    gw = SSM_INNER // SSM_GROUPS
    upd = [_dot_tn(bm[:, g * SSM_STATE:(g + 1) * SSM_STATE].astype(BF16), xde[:, g * gw:(g + 1) * gw])
           for g in range(SSM_GROUPS)]
    return st * jnp.exp(acs_x[last:last + 1, :]) + jnp.concatenate(upd, axis=1)


def _ssd_y_off(st, cm, acs_x):
    gw = SSM_INNER // SSM_GROUPS
    parts = [_dot(cm[:, g * SSM_STATE:(g + 1) * SSM_STATE].astype(BF16), st[:, g * gw:(g + 1) * gw].astype(BF16))
             for g in range(SSM_GROUPS)]
    return jnp.concatenate(parts, axis=1) * jnp.exp(acs_x)


def _ssd_prompt_body(z_ref, x_ref, bc_ref, dt_ref, cw_ref, cb_ref, dtb_ref, alog_ref, alogx_ref, dvec_ref, gn_ref,
                     e_ref, tril_ref, y_ref, st_ref, xp_sc, st_sc, *, n_chunks):
    L = SSM_CHUNK
    c = pl.program_id(1)

    @pl.when(c == 0)
    def _():
        xp_sc[0:8, :] = jnp.zeros((8, xp_sc.shape[1]), F32)
        st_sc[...] = jnp.zeros(st_sc.shape, F32)

    @pl.when(c > 0)
    def _():
        xp_sc[0:8, :] = xp_sc[L:L + 8, :]

    xp_sc[8:8 + L, 0:SSM_INNER] = x_ref[...]
    xp_sc[8:8 + L, SSM_INNER:] = bc_ref[...]
    conv = xp_sc[pl.ds(8 - (SSM_CONV - 1), L), :] * cw_ref[0:1, :]
    for k in range(1, SSM_CONV):
        conv = conv + xp_sc[pl.ds(8 - (SSM_CONV - 1) + k, L), :] * cw_ref[k:k + 1, :]
    xbc = _silu(conv + cb_ref[...])
    xs = xbc[:, :SSM_INNER]
    bm = xbc[:, SSM_INNER:SSM_INNER + SSM_GROUPS * SSM_STATE]
    cm = xbc[:, SSM_INNER + SSM_GROUPS * SSM_STATE:]

    hi = lax.Precision.HIGHEST
    dt = _softplus(dt_ref[...] + dtb_ref[...])
    tril = tril_ref[...]
    acs = _dot(tril, dt * (-jnp.exp(alog_ref[...])), precision=hi)
    acs_t = acs.T
    dt_x = _dot(dt, e_ref[...], precision=hi)
    acs_x = _dot(tril, dt_x * (-jnp.exp(alogx_ref[...])), precision=hi)
    xd = xs * dt_x
    xd_b = xd.astype(BF16)

    row = lax.broadcasted_iota(jnp.int32, (L, L), 0)
    col = lax.broadcasted_iota(jnp.int32, (L, L), 1)
    causal = row >= col
    lane = lax.broadcasted_iota(jnp.int32, (L, LANES), 1)
    heads_per_group = SSM_HEADS // SSM_GROUPS
    y_parts = []
    cb = [_dot_nt(cm[:, g * SSM_STATE:(g + 1) * SSM_STATE].astype(BF16),
                  bm[:, g * SSM_STATE:(g + 1) * SSM_STATE].astype(BF16)) for g in range(SSM_GROUPS)]
    for pair in range(SSM_HEADS // 2):
        xd_pair = xd_b[:, pair * LANES:(pair + 1) * LANES]
        outs = []
        for h in (2 * pair, 2 * pair + 1):
            decay = jnp.exp(jnp.where(causal, acs[:, h:h + 1] - acs_t[h:h + 1, :], -jnp.inf))
            outs.append(_dot((cb[h // heads_per_group] * decay).astype(BF16), xd_pair))
        y_parts.append(jnp.where(lane < SSM_HEAD_DIM, outs[0], outs[1]))
    y_diag = jnp.concatenate(y_parts, axis=1)

    st = st_sc[...]
    y = (y_diag + _ssd_y_off(st, cm, acs_x)) + dvec_ref[...] * xs
    y = y * _silu(z_ref[...])
    y_ref[...] = _group_rmsnorm(y, gn_ref[...]).astype(y_ref.dtype)
    st_new = _ssd_state_update(st, bm, xd, acs_x)
    st_sc[...] = st_new

    @pl.when(c == n_chunks - 1)
    def _():
        st_ref[...] = st_new.T


def _ssd_consts(conv_w, conv_b, dt_bias, a_log, d_vec, g_norm):
    pad = LANES - SSM_HEADS
    e_np = np.zeros((LANES, SSM_INNER), np.float32)
    for hh in range(SSM_HEADS):
        e_np[hh, hh * SSM_HEAD_DIM:(hh + 1) * SSM_HEAD_DIM] = 1.0
    e_mat = jnp.asarray(e_np)
    return dict(
        cw=conv_w, cb=conv_b.reshape(1, -1),
        dtb=jnp.pad(dt_bias, (0, pad)).reshape(1, LANES),
        alog=jnp.pad(a_log, (0, pad)).reshape(1, LANES),
        alogx=jnp.repeat(a_log, SSM_HEAD_DIM).reshape(1, SSM_INNER),
        dvec=jnp.repeat(d_vec, SSM_HEAD_DIM).reshape(1, SSM_INNER),
        gn=g_norm.reshape(1, SSM_INNER), e=e_mat)


def _full(shape):
    nd = len(shape)
    return pl.BlockSpec(shape, lambda *_: (0,) * nd)


def ssd_prompt(proj, consts, *, batch, seq):
    L = SSM_CHUNK
    nc = seq // L
    cdim = SSM_INNER + 2 * SSM_GROUPS * SSM_STATE
    tril = jnp.asarray(np.tril(np.ones((L, L), np.float32)))

    def rows(width, col):
        return pl.BlockSpec((L, width), lambda b, c: (b * nc + c, col // width))

    k = consts
    return pl.pallas_call(
        functools.partial(_ssd_prompt_body, n_chunks=nc),
        grid=(batch, nc),
        in_specs=[rows(SSM_INNER, COL_Z), rows(SSM_INNER, COL_X), rows(2 * SSM_GROUPS * SSM_STATE, COL_BC),
                  rows(LANES, COL_DT),
                  _full((SSM_CONV, cdim)), _full((1, cdim)), _full((1, LANES)), _full((1, LANES)),
                  _full((1, SSM_INNER)), _full((1, SSM_INNER)), _full((1, SSM_INNER)),
                  _full((LANES, SSM_INNER)), _full((L, L))],
        out_specs=[pl.BlockSpec((L, SSM_INNER), lambda b, c: (b * nc + c, 0)),
                   pl.BlockSpec((None, SSM_INNER, SSM_STATE), lambda b, c: (b, 0, 0))],
        out_shape=[jax.ShapeDtypeStruct((batch * seq, SSM_INNER), BF16),
                   jax.ShapeDtypeStruct((batch, SSM_INNER, SSM_STATE), F32)],
        scratch_shapes=[pltpu.VMEM((L + 8, cdim), F32), pltpu.VMEM((SSM_STATE, SSM_INNER), F32)],
        compiler_params=_cparams("parallel", "arbitrary"),
        name="ssd_prompt",
    )(proj, proj, proj, proj, k["cw"], k["cb"], k["dtb"], k["alog"], k["alogx"], k["dvec"], k["gn"], k["e"], tril)


def _sample_mixer_body(z_ref, dt_ref, scb_ref, xp_ref, up_c_ref, up_v_ref, st_in_ref,
                       cw_ref, cb_ref, dtb_ref, alogx_ref, dvec_ref, gn_ref, e_ref, scw_ref,
                       y_ref, ysc_ref, st_ref, u_out_ref, bpad_sc, xdpad_sc, u_sc, *, n_new):
    R = 8
    b = pl.program_id(0)

    @pl.when(b == 0)
    def _():
        bpad_sc[...] = jnp.zeros(bpad_sc.shape, F32)
        xdpad_sc[...] = jnp.zeros(xdpad_sc.shape, F32)

    conv = xp_ref[pl.ds(8 - (SSM_CONV - 1), R), :] * cw_ref[0:1, :]
    for k in range(1, SSM_CONV):
        conv = conv + xp_ref[pl.ds(8 - (SSM_CONV - 1) + k, R), :] * cw_ref[k:k + 1, :]
    xbc = _silu(conv + cb_ref[...])
    xs = xbc[:, :SSM_INNER]
    bm = xbc[:, SSM_INNER:SSM_INNER + SSM_GROUPS * SSM_STATE]
    cm = xbc[:, SSM_INNER + SSM_GROUPS * SSM_STATE:]

    hi = lax.Precision.HIGHEST
    rowl = lax.broadcasted_iota(jnp.int32, (R, LANES), 0)
    dt = jnp.where(rowl < n_new, _softplus(dt_ref[...] + dtb_ref[...]), 0.0)
    dt_x = _dot(dt, e_ref[...], precision=hi)
    da_x = dt_x * (-jnp.exp(alogx_ref[...]))
    rowx = lax.broadcasted_iota(jnp.int32, (R, SSM_INNER), 0)
    acs_x = jnp.zeros((R, SSM_INNER), F32)
    for s in range(n_new):
        acs_x = acs_x + jnp.where(rowx >= s, da_x[s:s + 1, :], 0.0)
    xd = xs * dt_x

    gw = SSM_INNER // SSM_GROUPS
    lanex = lax.broadcasted_iota(jnp.int32, (R, SSM_INNER), 1)
    y_diag = jnp.zeros((R, SSM_INNER), F32)
    for s in range(n_new):
        cbs = [jnp.sum(cm[:, g * SSM_STATE:(g + 1) * SSM_STATE] * bm[s:s + 1, g * SSM_STATE:(g + 1) * SSM_STATE],
                       axis=1, keepdims=True) for g in range(SSM_GROUPS)]
        cb_x = jnp.where(lanex < gw, cbs[0], cbs[1])
        decay = jnp.exp(jnp.where(rowx >= s, acs_x - acs_x[s:s + 1, :], -jnp.inf))
        y_diag = y_diag + (cb_x * decay) * xd[s:s + 1, :]

    st = st_in_ref[...].T
    y_off_parts = [_dot(cm[:, g * SSM_STATE:(g + 1) * SSM_STATE], st[:, g * gw:(g + 1) * gw])
                   for g in range(SSM_GROUPS)]
    y_off = jnp.concatenate(y_off_parts, axis=1) * jnp.exp(acs_x)
    y = (y_diag + y_off) + dvec_ref[...] * xs
    y = y * _silu(z_ref[...])
    y_ref[...] = _group_rmsnorm(y, gn_ref[...])

    bpad_sc[0:R, :] = bm
    xdpad_sc[0:R, :] = xd * jnp.exp(acs_x[R - 1:R, :] - acs_x)
    upd = [_dot_tn(bpad_sc[:, g * SSM_STATE:(g + 1) * SSM_STATE], xdpad_sc[:, g * gw:(g + 1) * gw])
           for g in range(SSM_GROUPS)]
    st_new = st * jnp.exp(acs_x[R - 1:R, :]) + jnp.concatenate(upd, axis=1)
    st_ref[...] = st_new.T

    u_sc[...] = up_c_ref[...] * up_v_ref[...]
    sconv = u_sc[pl.ds(8 - (SC_CONV - 1), R), :] * scw_ref[0:1, :]
    for k in range(1, SC_CONV):
        sconv = sconv + u_sc[pl.ds(8 - (SC_CONV - 1) + k, R), :] * scw_ref[k:k + 1, :]
    ysc_ref[...] = scb_ref[...] * sconv
    u_out_ref[...] = u_sc[8:16, :]


def sample_mixer(proj_s, xp_s, up_c, up_v, st_in, consts, sc_conv_w, *, n_new):
    nb = proj_s.shape[0]
    cdim = SSM_INNER + 2 * SSM_GROUPS * SSM_STATE
    k = consts

    def rows(width, col):
        return pl.BlockSpec((None, 8, width), lambda b: (b, 0, col // width))

    return pl.pallas_call(
        functools.partial(_sample_mixer_body, n_new=n_new),
        grid=(nb,),
        in_specs=[rows(SSM_INNER, COL_Z), rows(LANES, COL_DT), rows(SC_WIDTH, COL_SCB),
                  pl.BlockSpec((None, 16, cdim), lambda b: (b, 0, 0)),
                  pl.BlockSpec((None, 16, SC_WIDTH), lambda b: (b, 0, 0)),
                  pl.BlockSpec((None, 16, SC_WIDTH), lambda b: (b, 0, 0)),
                  pl.BlockSpec((None, SSM_INNER, SSM_STATE), lambda b: (b, 0, 0)),
                  _full((SSM_CONV, cdim)), _full((1, cdim)), _full((1, LANES)),
                  _full((1, SSM_INNER)), _full((1, SSM_INNER)), _full((1, SSM_INNER)),
                  _full((LANES, SSM_INNER)), _full((SC_CONV, SC_WIDTH))],
        out_specs=[pl.BlockSpec((None, 8, SSM_INNER), lambda b: (b, 0, 0)),
                   pl.BlockSpec((None, 8, SC_WIDTH), lambda b: (b, 0, 0)),
                   pl.BlockSpec((None, SSM_INNER, SSM_STATE), lambda b: (b, 0, 0)),
                   pl.BlockSpec((None, 8, SC_WIDTH), lambda b: (b, 0, 0))],
        out_shape=[jax.ShapeDtypeStruct((nb, 8, SSM_INNER), F32), jax.ShapeDtypeStruct((nb, 8, SC_WIDTH), F32),
                   jax.ShapeDtypeStruct((nb, SSM_INNER, SSM_STATE), F32),
                   jax.ShapeDtypeStruct((nb, 8, SC_WIDTH), F32)],
        scratch_shapes=[pltpu.VMEM((LANES, SSM_GROUPS * SSM_STATE), F32), pltpu.VMEM((LANES, SSM_INNER), F32),
                        pltpu.VMEM((16, SC_WIDTH), F32)],
        compiler_params=_cparams("arbitrary"),
        name="sample_mixer",
    )(proj_s, proj_s, proj_s, xp_s, up_c, up_v, st_in,
      k["cw"], k["cb"], k["dtb"], k["alogx"], k["dvec"], k["gn"], k["e"], sc_conv_w)


def _sconv_prompt_body(scb_ref, scc_ref, scv_ref, pc_ref, pv_ref, w_ref, y_ref, tail_ref, u_sc, *, ts):
    i = pl.program_id(1)
    hist = pc_ref[...] * pv_ref[...]
    u_sc[0:8, :] = jnp.where(i == 0, 0.0, hist)
    u_sc[8:8 + ts, :] = scc_ref[...] * scv_ref[...]
    conv = u_sc[pl.ds(8 - (SC_CONV - 1), ts), :] * w_ref[0:1, :]
    for k in range(1, SC_CONV):
        conv = conv + u_sc[pl.ds(8 - (SC_CONV - 1) + k, ts), :] * w_ref[k:k + 1, :]
    y_ref[...] = (scb_ref[...] * conv).astype(y_ref.dtype)
    tail_ref[...] = u_sc[ts:ts + 8, :]


def sconv_prompt(proj, sc_conv_w, *, batch, seq, ts):
    nt = seq // ts
    w = SC_WIDTH

    def rows(col):
        return pl.BlockSpec((ts, w), lambda b, i: (b * nt + i, col // w))

    def prev(col):
        return pl.BlockSpec((8, w), lambda b, i: (jnp.maximum((b * nt + i) * (ts // 8) - 1, 0), col // w))

    return pl.pallas_call(
        functools.partial(_sconv_prompt_body, ts=ts),
        grid=(batch, nt),
        in_specs=[rows(COL_SCB), rows(COL_SCC), rows(COL_SCV), prev(COL_SCC), prev(COL_SCV), _full((SC_CONV, w))],
        out_specs=[pl.BlockSpec((ts, w), lambda b, i: (b * nt + i, 0)),
                   pl.BlockSpec((None, 8, w), lambda b, i: (b, 0, 0))],
        out_shape=[jax.ShapeDtypeStruct((batch * seq, w), BF16), jax.ShapeDtypeStruct((batch, 8, w), F32)],
        scratch_shapes=[pltpu.VMEM((ts + 8, w), F32)],
        compiler_params=_cparams("parallel", "arbitrary"),
        name="sconv_prompt",
    )(proj, proj, proj, proj, proj, sc_conv_w)


def _merge_body(h_ref, ya_ref, ys_ref, yc_ref, wga_ref, wgb_ref, wgc_ref, ba_ref, bb_ref, bc_ref,
                wa_ref, wb_ref, wc_ref, o_ref):
    h = h_ref[...]
    out = _sigmoid(_dot(h, wga_ref[...]) + ba_ref[...]) * _dot(ya_ref[...], wa_ref[...])
    out = out + _sigmoid(_dot(h, wgb_ref[...]) + bb_ref[...]) * _dot(ys_ref[...], wb_ref[...])
    out = out + _sigmoid(_dot(h, wgc_ref[...]) + bc_ref[...]) * _dot(yc_ref[...], wc_ref[...])
    o_ref[...] = out.astype(o_ref.dtype)


def gated_merge(h, ya, ys, yc, w_gate, b_gate, wa, wb, wc, *, tm, tn):
    m, d = h.shape
    kb = ya.shape[1]
    nb = d // tn
    b2 = b_gate.reshape(1, -1)

    def act(width):
        return pl.BlockSpec((tm, width), lambda j, i: (i, 0))

    def wcol(rows_, off):
        return pl.BlockSpec((rows_, tn), lambda j, i: (0, off * nb + j))

    return pl.pallas_call(
        _merge_body,
        grid=(nb, m // tm),
        in_specs=[act(d), act(kb), act(kb), act(kb),
                  wcol(d, 0), wcol(d, 1), wcol(d, 2), wcol(1, 0), wcol(1, 1), wcol(1, 2),
                  wcol(kb, 0), wcol(kb, 0), wcol(kb, 0)],
        out_specs=pl.BlockSpec((tm, tn), lambda j, i: (i, j)),
        out_shape=jax.ShapeDtypeStruct((m, d), BF16),
        compiler_params=_cparams("parallel", "parallel"),
        name="gated_merge",
    )(h, ya, ys, yc, w_gate, w_gate, w_gate, b2, b2, b2, wa, wb, wc)


def _rope_tables(pos):
    half = QK_ROPE // 2
    inv = ROPE_THETA ** (-jnp.arange(half, dtype=F32) / half)
    ang = pos.astype(F32)[:, None] * inv[None, :]
    c, s = jnp.cos(ang), jnp.sin(ang)
    z = jnp.zeros((pos.shape[0], LANES - QK_ROPE), F32)
    return jnp.concatenate([c, c, z], axis=1), jnp.concatenate([-s, s, z], axis=1)


def _reorder_w_in(w):
    sizes = (Q_LORA, KV_LORA, QK_ROPE, SSM_INNER, SSM_INNER + 2 * SSM_GROUPS * SSM_STATE, SSM_HEADS,
             SC_WIDTH, SC_WIDTH, SC_WIDTH)
    q_c, kv_c, k_pe, z, xbc, dt, sc_b, sc_c, sc_v = jnp.split(w, np.cumsum(sizes)[:-1].tolist(), axis=1)
    d = w.shape[0]
    out = jnp.concatenate([sc_b, sc_c, sc_v, z, xbc, q_c, kv_c,
                           k_pe, jnp.zeros((d, LANES - QK_ROPE), w.dtype),
                           dt, jnp.zeros((d, LANES - SSM_HEADS), w.dtype)], axis=1)
    return out.astype(BF16)


def kernel(x_prompt, x_sample, cache_ckv, cache_kpe, state_ssm, state_mconv, state_sconv, page_table, g_attn_norm, w_in, g_q_a, w_q_b, g_kv_a, w_kv_b, ssm_conv_w, ssm_conv_b, ssm_dt_bias, ssm_a_log, ssm_d, g_ssm_norm, sc_conv_w, w_gate, b_gate, w_br_attn, w_br_ssm, w_br_sc, w_o, g_ffn_norm, w_ff_gate, w_ff_up, w_ff_down, w_router, w_e_gate, w_e_up, w_e_down, g_final):
    bp, tp, d = x_prompt.shape
    bs, ts, _ = x_sample.shape
    depth = w_in.shape[0]
    mp, ms = bp * tp, bs * ts
    m = mp + ms
    n_past = page_table.shape[1] * cache_ckv.shape[2]
    tm = m // 8
    tmh = m // 16
    cdim = SSM_INNER + 2 * SSM_GROUPS * SSM_STATE

    x = jnp.concatenate([x_prompt.reshape(mp, d), x_sample.reshape(ms, d)], axis=0)
    pos = jnp.concatenate([jnp.tile(jnp.arange(tp), bp), jnp.tile(n_past + jnp.arange(ts), bs)])
    cos, sin = _rope_tables(pos)
    cache_kpe_t = jnp.swapaxes(cache_kpe, 2, 3)

    outs = {k: [] for k in ("p_ckv", "p_kpe", "p_ssm", "p_mconv", "p_sconv", "s_ckv", "s_kpe", "s_ssm", "s_mconv", "s_sconv")}
    for l in range(depth):
        w_in_l = _reorder_w_in(w_in[l])
        wq = w_q_b[l]
        w_q_slots = jnp.concatenate([wq, jnp.zeros(wq.shape[:2] + (Q_SLOT - wq.shape[2],), wq.dtype)], axis=2)
        w_q_slots = w_q_slots.reshape(Q_LORA, MLA_HEADS * Q_SLOT).astype(BF16)
        wkv = w_kv_b[l]
        w_kv_flat = jnp.concatenate([wkv[..., :QK_NOPE].reshape(KV_LORA, -1), wkv[..., QK_NOPE:].reshape(KV_LORA, -1)],
                                    axis=1).astype(BF16)
        w_uk_t = jnp.transpose(wkv[..., :QK_NOPE], (1, 2, 0)).astype(BF16)
        w_uv = jnp.transpose(wkv[..., QK_NOPE:], (1, 0, 2)).astype(BF16)
        consts = _ssd_consts(ssm_conv_w[l], ssm_conv_b[l], ssm_dt_bias[l], ssm_a_log[l], ssm_d[l], g_ssm_norm[l])

        h = rmsnorm(x, g_attn_norm[l], BF16, tm)
        proj = matmul(h, w_in_l, tm=tm, tn=768)
        qn, ckv, kpe = mla_prep(proj, g_q_a[l], g_kv_a[l], cos, sin, tm=tm)
        q = qproj(qn, w_q_slots, cos, sin, tm=tm)

        k_full, v_full = kv_expand(ckv, kpe, w_kv_flat, rows=mp, tm=1024)
        ya_p = flash_attention(q, k_full, v_full, batch=bp, seq=tp, blk=512)

        q_lat = blockdiag_matmul(q, w_uk_t, tm=ms, row_block=mp // ms, col_block0=0, col_stride=2)
        q_pe_s = q[mp:].reshape(ms, MLA_HEADS, Q_SLOT)[:, :, QK_NOPE:].reshape(bs, ts * MLA_HEADS, LANES)
        ckv_s = ckv[mp:].reshape(bs, ts, KV_LORA)
        kpe_s = kpe[mp:].reshape(bs, ts, LANES)
        new_c = jnp.pad(ckv_s, ((0, 0), (0, 8 - ts), (0, 0)))
        new_k = jnp.pad(kpe_s, ((0, 0), (0, 8 - ts), (0, 0)))
        o_lat = decode_attention(q_lat.reshape(bs, ts * MLA_HEADS, KV_LORA), q_pe_s, cache_ckv, cache_kpe_t, l,
                                 page_table, new_c, new_k, pages_per_step=32)
        ya_s = blockdiag_matmul(o_lat.reshape(ms, MLA_HEADS * KV_LORA), w_uv, tm=ms, row_block=0, col_block0=0,
                                col_stride=1)

        ys_p, ssm_p = ssd_prompt(proj, consts, batch=bp, seq=tp)
        yc_p, u_tail_p = sconv_prompt(proj, sc_conv_w[l], batch=bp, seq=tp, ts=512)

        proj_s = jnp.pad(proj[mp:].reshape(bs, ts, D_IN_PAD), ((0, 0), (0, 8 - ts), (0, 0)))
        xbc_s = proj_s[:, :ts, COL_X:COL_X + cdim]
        xp_s = jnp.concatenate([jnp.zeros((bs, 8 - (SSM_CONV - 1), cdim), F32), state_mconv[l], xbc_s,
                                jnp.zeros((bs, 8 - ts, cdim), F32)], axis=1)
        zpad = jnp.zeros((bs, 8 - (SC_CONV - 1), SC_WIDTH), F32)
        zend = jnp.zeros((bs, 8 - ts, SC_WIDTH), F32)
        up_c = jnp.concatenate([zpad, state_sconv[l], proj_s[:, :ts, COL_SCC:COL_SCC + SC_WIDTH], zend], axis=1)
        up_v = jnp.concatenate([zpad, jnp.ones_like(state_sconv[l]), proj_s[:, :ts, COL_SCV:COL_SCV + SC_WIDTH], zend], axis=1)
        ys_s, yc_s, ssm_s, u_new_s = sample_mixer(proj_s, xp_s, up_c, up_v,
                                                  state_ssm[l].reshape(bs, SSM_INNER, SSM_STATE), consts, sc_conv_w[l],
                                                  n_new=ts)

        ya = jnp.concatenate([ya_p, ya_s], axis=0)
        ys = jnp.concatenate([ys_p, ys_s[:, :ts].reshape(ms, SSM_INNER).astype(BF16)], axis=0)
        yc = jnp.concatenate([yc_p, yc_s[:, :ts].reshape(ms, SC_WIDTH).astype(BF16)], axis=0)
        merged = gated_merge(h, ya, ys, yc, w_gate[l].astype(BF16), b_gate[l], w_br_attn[l].astype(BF16),
                             w_br_ssm[l].astype(BF16), w_br_sc[l].astype(BF16), tm=tmh, tn=512)
        x = matmul(merged, w_o[l].astype(BF16), tm=tm, tn=512, res=x)

        outs["p_ckv"].append(ckv[:mp].reshape(bp, tp, KV_LORA))
        outs["p_kpe"].append(kpe[:mp, :QK_ROPE].reshape(bp, tp, QK_ROPE))
        outs["p_ssm"].append(ssm_p.reshape(bp, SSM_HEADS, SSM_HEAD_DIM, SSM_STATE))
        xbc_p = proj[:mp, COL_X:COL_X + cdim].reshape(bp, tp, cdim)
        outs["p_mconv"].append(xbc_p[:, tp - (SSM_CONV - 1):])
        outs["p_sconv"].append(u_tail_p[:, 8 - (SC_CONV - 1):])
        outs["s_ckv"].append(ckv_s)
        outs["s_kpe"].append(kpe_s[:, :, :QK_ROPE])
        outs["s_ssm"].append(ssm_s.reshape(bs, SSM_HEADS, SSM_HEAD_DIM, SSM_STATE))
        outs["s_mconv"].append(xp_s[:, 8 + ts - (SSM_CONV - 1):8 + ts])
        outs["s_sconv"].append(u_new_s[:, ts - (SC_CONV - 1):ts])

        i = l // 2
        if l % 2 == 0:
            h2 = rmsnorm(x, g_ffn_norm[l], BF16, tm)
            hdn = swiglu_up(h2, w_ff_gate[i].astype(BF16), w_ff_up[i].astype(BF16), tm=tm, tf=512)
            x = matmul(hdn, w_ff_down[i].astype(BF16), tm=tm, tn=256, res=x, rows_outer=True)
        else:
            h2, gate = rmsnorm_router(x, g_ffn_norm[l], w_router[i], tm)
            hdn = moe_up(h2, w_e_gate[i].astype(BF16), w_e_up[i].astype(BF16), gate, tm=tmh)
            wd = w_e_down[i].reshape(-1, d).astype(BF16)
            x = matmul(hdn, wd, tm=tmh, tn=256, res=x, rows_outer=True)

    y = rmsnorm(x, g_final, F32, tm)
    st = {k: jnp.stack(v, axis=0) for k, v in outs.items()}
    return (y[:mp].reshape(bp, tp, d), y[mp:].reshape(bs, ts, d),
            st["p_ckv"], st["p_kpe"], st["p_ssm"], st["p_mconv"], st["p_sconv"],
            st["s_ckv"], st["s_kpe"], st["s_ssm"], st["s_mconv"], st["s_sconv"])
```

```python
import functools

import jax
import jax.numpy as jnp
import numpy as np
from jax import lax
from jax.experimental import pallas as pl
from jax.experimental.pallas import tpu as pltpu

F32 = jnp.float32
BF16 = jnp.bfloat16
EPS = 1e-6
ROPE_THETA = 10000.0
LANES = 128
MLA_HEADS = 8
QK_NOPE = 128
QK_ROPE = 64
V_HEAD = 128
KV_LORA = 512
Q_LORA = 512
Q_SLOT = 256
SSM_HEADS = 16
SSM_HEAD_DIM = 64
SSM_INNER = 1024
SSM_GROUPS = 2
SSM_STATE = 128
SSM_CONV = 4
SSM_CHUNK = 128
SC_WIDTH = 1024
SC_CONV = 3
N_EXPERTS = 8
ATTN_SCALE = (QK_NOPE + QK_ROPE) ** -0.5
Q_SCALE = ATTN_SCALE * float(np.log2(np.e))
VMEM_LIMIT = 56 * 1024 * 1024

COL_SCB, COL_SCC, COL_SCV, COL_Z, COL_X, COL_BC, COL_QC, COL_KVC, COL_KPE, COL_DT = (
    0, 1024, 2048, 3072, 4096, 5120, 5632, 6144, 6656, 6784)
D_IN_PAD = 6912


def _cparams(*sem):
    return pltpu.CompilerParams(dimension_semantics=sem, vmem_limit_bytes=VMEM_LIMIT)


def _sigmoid(x):
    return 1.0 / (1.0 + jnp.exp(-x))


def _silu(x):
    return x * _sigmoid(x)


def _softplus(x):
    return jnp.maximum(x, 0.0) + jnp.log1p(jnp.exp(-jnp.abs(x)))


def _dot(a, b, **kw):
    return jnp.dot(a, b, preferred_element_type=F32, **kw)


def _dot_nt(a, b):
    return lax.dot_general(a, b, (((1,), (1,)), ((), ())), preferred_element_type=F32)


def _dot_tn(a, b):
    return lax.dot_general(a, b, (((0,), (0,)), ((), ())), preferred_element_type=F32)


def _rms(x, g):
    r = lax.rsqrt(jnp.mean(x * x, axis=-1, keepdims=True) + EPS)
    return (x * r) * g


def _rmsnorm_body(x_ref, g_ref, o_ref):
    o_ref[...] = _rms(x_ref[...], g_ref[...]).astype(o_ref.dtype)


def rmsnorm(x, g, out_dtype, tm):
    m, d = x.shape
    return pl.pallas_call(
        _rmsnorm_body,
        grid=(m // tm,),
        in_specs=[pl.BlockSpec((tm, d), lambda i: (i, 0)), pl.BlockSpec((1, d), lambda i: (0, 0))],
        out_specs=pl.BlockSpec((tm, d), lambda i: (i, 0)),
        out_shape=jax.ShapeDtypeStruct((m, d), out_dtype),
        compiler_params=_cparams("parallel"),
        name="rmsnorm",
    )(x, g.reshape(1, d))


def _rmsnorm_split_body(x_ref, g_ref, op_ref, os_ref, *, n_prompt_tiles):
    i = pl.program_id(0)
    y = _rms(x_ref[...], g_ref[...])

    @pl.when(i < n_prompt_tiles)
    def _():
        op_ref[...] = y

    @pl.when(i >= n_prompt_tiles)
    def _():
        os_ref[...] = y


def rmsnorm_split(x, g, *, rows_prompt, tm):
    m, d = x.shape
    npt = rows_prompt // tm
    return pl.pallas_call(
        functools.partial(_rmsnorm_split_body, n_prompt_tiles=npt),
        grid=(m // tm,),
        in_specs=[pl.BlockSpec((tm, d), lambda i: (i, 0)), pl.BlockSpec((1, d), lambda i: (0, 0))],
        out_specs=[pl.BlockSpec((tm, d), lambda i: (jnp.minimum(i, npt - 1), 0)),
                   pl.BlockSpec((tm, d), lambda i: (jnp.maximum(i - npt, 0), 0))],
        out_shape=[jax.ShapeDtypeStruct((rows_prompt, d), F32), jax.ShapeDtypeStruct((m - rows_prompt, d), F32)],
        compiler_params=_cparams("arbitrary"),
        name="rmsnorm_split",
    )(x, g.reshape(1, d))


def _rmsnorm_router_body(x_ref, g_ref, wr_ref, h_ref, gate_ref):
    h = _rms(x_ref[...], g_ref[...])
    h_ref[...] = h.astype(h_ref.dtype)
    lg = _dot(h, wr_ref[...], precision=lax.Precision.HIGHEST)
    lane = lax.broadcasted_iota(jnp.int32, lg.shape, 1).astype(F32)
    lg = jnp.where(lane < N_EXPERTS, lg, -jnp.inf)
    m1 = jnp.max(lg, axis=1, keepdims=True)
    i1 = jnp.min(jnp.where(lg == m1, lane, float(LANES)), axis=1, keepdims=True)
    oh1 = lane == i1
    lg2 = jnp.where(oh1, -jnp.inf, lg)
    m2 = jnp.max(lg2, axis=1, keepdims=True)
    i2 = jnp.min(jnp.where(lg2 == m2, lane, float(LANES)), axis=1, keepdims=True)
    oh2 = lane == i2
    e = jnp.exp(m2 - m1)
    w1 = 1.0 / (1.0 + e)
    w2 = e / (1.0 + e)
    gate_ref[...] = jnp.where(oh1, w1, 0.0) + jnp.where(oh2, w2, 0.0)


def rmsnorm_router(x, g, w_router, tm):
    m, d = x.shape
    wr = jnp.pad(w_router, ((0, 0), (0, LANES - w_router.shape[1])))
    return pl.pallas_call(
        _rmsnorm_router_body,
        grid=(m // tm,),
        in_specs=[pl.BlockSpec((tm, d), lambda i: (i, 0)), pl.BlockSpec((1, d), lambda i: (0, 0)),
                  pl.BlockSpec((d, LANES), lambda i: (0, 0))],
        out_specs=[pl.BlockSpec((tm, d), lambda i: (i, 0)), pl.BlockSpec((tm, LANES), lambda i: (i, 0))],
        out_shape=[jax.ShapeDtypeStruct((m, d), BF16), jax.ShapeDtypeStruct((m, LANES), F32)],
        compiler_params=_cparams("parallel"),
        name="rmsnorm_router",
    )(x, g.reshape(1, d), wr)


def _cast_once(src_refs, dst_refs):
    @pl.when(pl.program_id(1) == 0)
    def _():
        for s, d in zip(src_refs, dst_refs):
            d[...] = s[...].astype(d.dtype)


def _mm_body(x_ref, w_ref, *rest, has_res, cast_w):
    rest = list(rest)
    r_ref = rest.pop(0) if has_res else None
    o_ref = rest.pop(0)
    if cast_w:
        (w_sc,) = rest
        _cast_once([w_ref], [w_sc])
        w_ref = w_sc
    acc = _dot(x_ref[...], w_ref[...])
    o_ref[...] = ((r_ref[...] + acc) if has_res else acc).astype(o_ref.dtype)


def matmul(x, w, *, tm, tn, res=None, out_dtype=F32, rows_outer=False):
    m, kd = x.shape
    n = w.shape[1]
    cast_w = w.dtype != BF16
    assert not (cast_w and rows_outer)

    def ij(a, b):
        return (a, b) if rows_outer else (b, a)

    in_specs = [pl.BlockSpec((tm, kd), lambda a, b: (ij(a, b)[0], 0)),
                pl.BlockSpec((kd, tn), lambda a, b: (0, ij(a, b)[1]))]
    args = [x, w]
    if res is not None:
        in_specs.append(pl.BlockSpec((tm, tn), lambda a, b: ij(a, b)))
        args.append(res)
    return pl.pallas_call(
        functools.partial(_mm_body, has_res=res is not None, cast_w=cast_w),
        grid=(m // tm, n // tn) if rows_outer else (n // tn, m // tm),
        in_specs=in_specs,
        out_specs=pl.BlockSpec((tm, tn), lambda a, b: ij(a, b)),
        out_shape=jax.ShapeDtypeStruct((m, n), out_dtype),
        scratch_shapes=[pltpu.VMEM((kd, tn), BF16)] if cast_w else [],
        compiler_params=_cparams("parallel", "arbitrary"),
        name="matmul",
    )(*args)


def _swiglu_body(x_ref, wg_ref, wu_ref, *rest, scaled, cast_w):
    rest = list(rest)
    gate_ref = rest.pop(0) if scaled else None
    o_ref = rest.pop(0)
    if cast_w:
        _cast_once([wg_ref, wu_ref], rest)
        wg_ref, wu_ref = rest
    x = x_ref[...]
    g = _dot(x, wg_ref[...])
    u = _dot(x, wu_ref[...])
    hdn = _silu(g) * u
    if scaled:
        gate = gate_ref[...]
        lane = lax.broadcasted_iota(jnp.int32, gate.shape, 1)
        sc = jnp.sum(jnp.where(lane == pl.program_id(0), gate, 0.0), axis=1, keepdims=True)
        hdn = hdn * sc
    o_ref[...] = hdn.astype(o_ref.dtype)


def swiglu_up(x, wg, wu, *, tm, tf):
    m, d = x.shape
    f = wg.shape[1]
    return pl.pallas_call(
        functools.partial(_swiglu_body, scaled=False, cast_w=True),
        grid=(f // tf, m // tm),
        in_specs=[pl.BlockSpec((tm, d), lambda j, i: (i, 0)), pl.BlockSpec((d, tf), lambda j, i: (0, j)),
                  pl.BlockSpec((d, tf), lambda j, i: (0, j))],
        out_specs=pl.BlockSpec((tm, tf), lambda j, i: (i, j)),
        out_shape=jax.ShapeDtypeStruct((m, f), BF16),
        scratch_shapes=[pltpu.VMEM((d, tf), BF16), pltpu.VMEM((d, tf), BF16)],
        compiler_params=_cparams("parallel", "arbitrary"),
        name="swiglu_up",
    )(x, wg, wu)


def moe_up(x, wg, wu, gate, *, tm):
    m, d = x.shape
    ne, _, f = wg.shape
    return pl.pallas_call(
        functools.partial(_swiglu_body, scaled=True, cast_w=False),
        grid=(ne, m // tm),
        in_specs=[pl.BlockSpec((tm, d), lambda j, i: (i, 0)), pl.BlockSpec((None, d, f), lambda j, i: (j, 0, 0)),
                  pl.BlockSpec((None, d, f), lambda j, i: (j, 0, 0)), pl.BlockSpec((tm, LANES), lambda j, i: (i, 0))],
        out_specs=pl.BlockSpec((tm, f), lambda j, i: (i, j)),
        out_shape=jax.ShapeDtypeStruct((m, ne * f), BF16),
        compiler_params=_cparams("parallel", "parallel"),
        name="moe_up",
    )(x, wg, wu, gate)


def _blockdiag_body(x_ref, w_ref, o_ref):
    o_ref[...] = _dot(x_ref[...], w_ref[...]).astype(o_ref.dtype)


def blockdiag_matmul(x, w, *, tm, row_block, col_block0, col_stride, out_dtype=BF16):
    nh, ki, no = w.shape
    return pl.pallas_call(
        _blockdiag_body,
        grid=(nh,),
        in_specs=[pl.BlockSpec((tm, ki), lambda h: (row_block, col_block0 + h * col_stride)),
                  pl.BlockSpec((None, ki, no), lambda h: (h, 0, 0))],
        out_specs=pl.BlockSpec((tm, no), lambda h: (0, h)),
        out_shape=jax.ShapeDtypeStruct((tm, nh * no), out_dtype),
        compiler_params=_cparams("parallel"),
        name="blockdiag_matmul",
    )(x, w)


def _rope_slab(x, cos, sin):
    half = QK_ROPE // 2
    lane = lax.broadcasted_iota(jnp.int32, x.shape, 1)
    swapped = jnp.where(lane < half, pltpu.roll(x, LANES - half, 1), pltpu.roll(x, half, 1))
    return x * cos + swapped * sin


def _mla_prep_body(qc_ref, kvc_ref, kpe_ref, gq_ref, gkv_ref, cos_ref, sin_ref, qn_ref, ckv_ref, kpe_out_ref):
    qn_ref[...] = _rms(qc_ref[...], gq_ref[...]).astype(qn_ref.dtype)
    ckv_ref[...] = _rms(kvc_ref[...], gkv_ref[...])
    kpe_out_ref[...] = _rope_slab(kpe_ref[...], cos_ref[...], sin_ref[...])


def mla_prep(proj, g_q, g_kv, cos, sin, *, tm):
    m = proj.shape[0]
    return pl.pallas_call(
        _mla_prep_body,
        grid=(m // tm,),
        in_specs=[pl.BlockSpec((tm, Q_LORA), lambda i: (i, COL_QC // Q_LORA)),
                  pl.BlockSpec((tm, KV_LORA), lambda i: (i, COL_KVC // KV_LORA)),
                  pl.BlockSpec((tm, LANES), lambda i: (i, COL_KPE // LANES)),
                  pl.BlockSpec((1, Q_LORA), lambda i: (0, 0)), pl.BlockSpec((1, KV_LORA), lambda i: (0, 0)),
                  pl.BlockSpec((tm, LANES), lambda i: (i, 0)), pl.BlockSpec((tm, LANES), lambda i: (i, 0))],
        out_specs=[pl.BlockSpec((tm, Q_LORA), lambda i: (i, 0)), pl.BlockSpec((tm, KV_LORA), lambda i: (i, 0)),
                   pl.BlockSpec((tm, LANES), lambda i: (i, 0))],
        out_shape=[jax.ShapeDtypeStruct((m, Q_LORA), BF16), jax.ShapeDtypeStruct((m, KV_LORA), F32),
                   jax.ShapeDtypeStruct((m, LANES), F32)],
        compiler_params=_cparams("parallel"),
        name="mla_prep",
    )(proj, proj, proj, g_q.reshape(1, -1), g_kv.reshape(1, -1), cos, sin)


def _qproj_body(x_ref, w_ref, cos_ref, sin_ref, o_ref):
    acc = _dot(x_ref[...], w_ref[...])
    cos = cos_ref[...] * Q_SCALE
    sin = sin_ref[...] * Q_SCALE
    for h in range(MLA_HEADS):
        base = h * Q_SLOT
        o_ref[:, base:base + QK_NOPE] = (acc[:, base:base + QK_NOPE] * Q_SCALE).astype(o_ref.dtype)
        o_ref[:, base + QK_NOPE:base + Q_SLOT] = _rope_slab(acc[:, base + QK_NOPE:base + Q_SLOT], cos, sin).astype(o_ref.dtype)


def qproj(qn, w_q_slots, cos, sin, *, tm):
    m = qn.shape[0]
    n = w_q_slots.shape[1]
    return pl.pallas_call(
        _qproj_body,
        grid=(m // tm,),
        in_specs=[pl.BlockSpec((tm, Q_LORA), lambda i: (i, 0)), pl.BlockSpec((Q_LORA, n), lambda i: (0, 0)),
                  pl.BlockSpec((tm, LANES), lambda i: (i, 0)), pl.BlockSpec((tm, LANES), lambda i: (i, 0))],
        out_specs=pl.BlockSpec((tm, n), lambda i: (i, 0)),
        out_shape=jax.ShapeDtypeStruct((m, n), BF16),
        compiler_params=_cparams("parallel"),
        name="qproj",
    )(qn, w_q_slots, cos, sin)


def _kv_expand_body(ckv_ref, kpe_ref, w_ref, k_ref, v_ref):
    acc = _dot(ckv_ref[...].astype(BF16), w_ref[...])
    kpe = kpe_ref[...].astype(k_ref.dtype)
    for h in range(MLA_HEADS):
        base = h * Q_SLOT
        k_ref[:, base:base + QK_NOPE] = acc[:, h * QK_NOPE:(h + 1) * QK_NOPE].astype(k_ref.dtype)
        k_ref[:, base + QK_NOPE:base + Q_SLOT] = kpe
    v_ref[...] = acc[:, MLA_HEADS * QK_NOPE:].astype(v_ref.dtype)


def kv_expand(ckv, kpe, w_kv_flat, *, rows, tm):
    n = w_kv_flat.shape[1]
    return pl.pallas_call(
        _kv_expand_body,
        grid=(rows // tm,),
        in_specs=[pl.BlockSpec((tm, KV_LORA), lambda i: (i, 0)), pl.BlockSpec((tm, LANES), lambda i: (i, 0)),
                  pl.BlockSpec((KV_LORA, n), lambda i: (0, 0))],
        out_specs=[pl.BlockSpec((tm, MLA_HEADS * Q_SLOT), lambda i: (i, 0)),
                   pl.BlockSpec((tm, MLA_HEADS * V_HEAD), lambda i: (i, 0))],
        out_shape=[jax.ShapeDtypeStruct((rows, MLA_HEADS * Q_SLOT), BF16),
                   jax.ShapeDtypeStruct((rows, MLA_HEADS * V_HEAD), BF16)],
        compiler_params=_cparams("parallel"),
        name="kv_expand",
    )(ckv, kpe, w_kv_flat)


def _flash_body(q_ref, k_ref, v_ref, o_ref, *, blk, heads):
    qi = pl.program_id(2)
    qs = [q_ref[:, h * Q_SLOT:(h + 1) * Q_SLOT] for h in range(heads)]

    def update(j, diagonal, h, m, l, acc):
        start = pl.multiple_of(j * blk, blk)
        k = k_ref[pl.ds(start, blk), h * Q_SLOT:(h + 1) * Q_SLOT]
        v = v_ref[pl.ds(start, blk), h * V_HEAD:(h + 1) * V_HEAD]
        s = _dot_nt(qs[h], k)
        if diagonal:
            row = lax.broadcasted_iota(jnp.int32, s.shape, 0)
            col = lax.broadcasted_iota(jnp.int32, s.shape, 1)
            s = jnp.where(row >= col, s, -jnp.inf)
        m_new = jnp.maximum(m, jnp.max(s, axis=1, keepdims=True))
        alpha = jnp.exp2(m - m_new)
        p = jnp.exp2(s - m_new)
        l = alpha * l + jnp.sum(p, axis=1, keepdims=True)
        acc = alpha * acc + _dot(p.astype(BF16), v)
        return m_new, l, acc

    def step(j, carry, diagonal):
        return tuple(update(j, diagonal, h, *carry[h]) for h in range(heads))

    init = tuple((jnp.full((blk, 1), -jnp.inf, F32), jnp.zeros((blk, 1), F32), jnp.zeros((blk, V_HEAD), F32))
                 for _ in range(heads))
    carry = lax.fori_loop(0, qi, lambda j, c: step(j, c, False), init)
    carry = step(qi, carry, True)
    for h in range(heads):
        _, l, acc = carry[h]
        o_ref[:, h * V_HEAD:(h + 1) * V_HEAD] = (acc / l).astype(o_ref.dtype)


def flash_attention(q, k, v, *, batch, seq, blk, heads):
    nq = seq // blk
    return pl.pallas_call(
        functools.partial(_flash_body, blk=blk, heads=heads),
        grid=(batch, MLA_HEADS // heads, nq),
        in_specs=[pl.BlockSpec((blk, heads * Q_SLOT), lambda b, h, i: (b * nq + i, h)),
                  pl.BlockSpec((seq, heads * Q_SLOT), lambda b, h, i: (b, h)),
                  pl.BlockSpec((seq, heads * V_HEAD), lambda b, h, i: (b, h))],
        out_specs=pl.BlockSpec((blk, heads * V_HEAD), lambda b, h, i: (b * nq + i, h)),
        out_shape=jax.ShapeDtypeStruct((batch * seq, MLA_HEADS * V_HEAD), BF16),
        compiler_params=_cparams("parallel", "parallel", "parallel"),
        name="flash_attention",
    )(q, k, v)


def _decode_body(pt_ref, q_ref, qpe_ref, *refs, pages_per_step, n_steps, n_new):
    del pt_ref
    pps = pages_per_step
    ckv_refs = refs[:pps]
    kpe_refs = refs[pps:2 * pps]
    newc_ref, newk_ref, o_ref, m_sc, l_sc, acc_sc = refs[2 * pps:]
    c = pl.program_id(1)

    @pl.when(c == 0)
    def _():
        m_sc[...] = jnp.full(m_sc.shape, -jnp.inf, F32)
        l_sc[...] = jnp.zeros(l_sc.shape, F32)
        acc_sc[...] = jnp.zeros(acc_sc.shape, F32)

    q = q_ref[...]
    qp = qpe_ref[:, :QK_ROPE]
    ks, ss = [], []
    for i in range(pps):
        k = ckv_refs[i][...].astype(BF16)
        kp_t = kpe_refs[i][...].astype(BF16)
        ks.append(k)
        ss.append(_dot_nt(q, k) + _dot(qp, kp_t))
    s = jnp.concatenate(ss, axis=1)
    m_prev = m_sc[:, :1]
    l_prev = l_sc[:, :1]
    m_new = jnp.maximum(m_prev, jnp.max(s, axis=1, keepdims=True))
    alpha = jnp.exp2(m_prev - m_new)
    p = jnp.exp2(s - m_new)
    l_new = alpha * l_prev + jnp.sum(p, axis=1, keepdims=True)
    page = ks[0].shape[0]
    pv = _dot(p[:, :page].astype(BF16), ks[0])
    for i in range(1, pps):
        pv = pv + _dot(p[:, i * page:(i + 1) * page].astype(BF16), ks[i])
    acc_new = alpha * acc_sc[...] + pv
    m_sc[...] = jnp.broadcast_to(m_new, m_sc.shape)
    l_sc[...] = jnp.broadcast_to(l_new, l_sc.shape)
    acc_sc[...] = acc_new

    @pl.when(c == n_steps - 1)
    def _():
        qf = q.astype(F32)
        qpf = qp.astype(F32)
        kn = newc_ref[...]
        kpn = newk_ref[:, :QK_ROPE]
        row = lax.broadcasted_iota(jnp.int32, (q.shape[0], 1), 0)
        sj = []
        for j in range(n_new):
            v = (jnp.sum(qf * kn[j:j + 1, :], axis=1, keepdims=True)
                 + jnp.sum(qpf * kpn[j:j + 1, :], axis=1, keepdims=True))
            sj.append(jnp.where(row >= j * MLA_HEADS, v, -jnp.inf))
        m_fin = m_new
        for v in sj:
            m_fin = jnp.maximum(m_fin, v)
        a2 = jnp.exp2(m_new - m_fin)
        l_fin = a2 * l_new
        acc_fin = a2 * acc_new
        for j in range(n_new):
            pj = jnp.exp2(sj[j] - m_fin)
            l_fin = l_fin + pj
            acc_fin = acc_fin + pj * kn[j:j + 1, :]
        o_ref[...] = (acc_fin / l_fin).astype(o_ref.dtype)


def decode_attention(q_lat, q_pe, cache_ckv, cache_kpe, layer, page_table, new_ckv, new_kpe, *, pages_per_step):
    nb, rows, _ = q_lat.shape
    n_pages = page_table.shape[1]
    page = cache_ckv.shape[2]
    pps = pages_per_step
    n_steps = n_pages // pps
    n_new = rows // MLA_HEADS

    def page_spec(shape, i):
        return pl.BlockSpec((None, None) + shape, lambda b, c, pt: (layer, pt[b * n_pages + c * pps + i], 0, 0))

    in_specs = ([pl.BlockSpec((None, rows, KV_LORA), lambda b, c, pt: (b, 0, 0)),
                 pl.BlockSpec((None, rows, LANES), lambda b, c, pt: (b, 0, 0))]
                + [page_spec((page, KV_LORA), i) for i in range(pps)]
                + [page_spec((QK_ROPE, page), i) for i in range(pps)]
                + [pl.BlockSpec((None, 8, KV_LORA), lambda b, c, pt: (b, 0, 0)),
                   pl.BlockSpec((None, 8, LANES), lambda b, c, pt: (b, 0, 0))])
    grid_spec = pltpu.PrefetchScalarGridSpec(
        num_scalar_prefetch=1,
        grid=(nb, n_steps),
        in_specs=in_specs,
        out_specs=pl.BlockSpec((None, rows, KV_LORA), lambda b, c, pt: (b, 0, 0)),
        scratch_shapes=[pltpu.VMEM((rows, LANES), F32), pltpu.VMEM((rows, LANES), F32),
                        pltpu.VMEM((rows, KV_LORA), F32)],
    )
    return pl.pallas_call(
        functools.partial(_decode_body, pages_per_step=pps, n_steps=n_steps, n_new=n_new),
        grid_spec=grid_spec,
        out_shape=jax.ShapeDtypeStruct((nb, rows, KV_LORA), BF16),
        compiler_params=_cparams("parallel", "arbitrary"),
        name="decode_attention",
    )(page_table.reshape(-1), q_lat, q_pe, *([cache_ckv] * pps), *([cache_kpe] * pps), new_ckv, new_kpe)


def _group_rmsnorm(y, g):
    gw = SSM_INNER // SSM_GROUPS
    parts = []
    for i in range(SSM_GROUPS):
        yg = y[:, i * gw:(i + 1) * gw]
        parts.append(yg * lax.rsqrt(jnp.mean(yg * yg, axis=-1, keepdims=True) + EPS))
    return jnp.concatenate(parts, axis=1) * g


def _ssd_state_update(st, bm, xd, acs_x):
    last = acs_x.shape[0] - 1
    xde = (xd * jnp.exp(acs_x[last:last + 1, :] - acs_x)).astype(BF16)
    gw = SSM_INNER // SSM_GROUPS
    upd = [_dot_tn(bm[:, g * SSM_STATE:(g + 1) * SSM_STATE].astype(BF16), xde[:, g * gw:(g + 1) * gw])
           for g in range(SSM_GROUPS)]
    return st * jnp.exp(acs_x[last:last + 1, :]) + jnp.concatenate(upd, axis=1)


def _ssd_y_off(st, cm, acs_x):
    gw = SSM_INNER // SSM_GROUPS
    parts = [_dot(cm[:, g * SSM_STATE:(g + 1) * SSM_STATE].astype(BF16), st[:, g * gw:(g + 1) * gw].astype(BF16))
             for g in range(SSM_GROUPS)]
    return jnp.concatenate(parts, axis=1) * jnp.exp(acs_x)


def _ssd_prompt_body(z_ref, x_ref, bc_ref, dt_ref, cw_ref, cb_ref, dtb_ref, alog_ref, alogx_ref, dvec_ref, gn_ref,
                     e_ref, tril_ref, y_ref, st_ref, xp_sc, st_sc, *, n_chunks):
    L = SSM_CHUNK
    c = pl.program_id(1)

    @pl.when(c == 0)
    def _():
        xp_sc[0:8, :] = jnp.zeros((8, xp_sc.shape[1]), F32)
        st_sc[...] = jnp.zeros(st_sc.shape, F32)

    @pl.when(c > 0)
    def _():
        xp_sc[0:8, :] = xp_sc[L:L + 8, :]

    xp_sc[8:8 + L, 0:SSM_INNER] = x_ref[...]
    xp_sc[8:8 + L, SSM_INNER:] = bc_ref[...]
    conv = xp_sc[pl.ds(8 - (SSM_CONV - 1), L), :] * cw_ref[0:1, :]
    for k in range(1, SSM_CONV):
        conv = conv + xp_sc[pl.ds(8 - (SSM_CONV - 1) + k, L), :] * cw_ref[k:k + 1, :]
    xbc = _silu(conv + cb_ref[...])
    xs = xbc[:, :SSM_INNER]
    bm = xbc[:, SSM_INNER:SSM_INNER + SSM_GROUPS * SSM_STATE]
    cm = xbc[:, SSM_INNER + SSM_GROUPS * SSM_STATE:]

    hi = lax.Precision.HIGHEST
    dt = _softplus(dt_ref[...] + dtb_ref[...])
    tril = tril_ref[...]
    acs = _dot(tril, dt * (-jnp.exp(alog_ref[...])), precision=hi)
    acs_t = acs.T
    dt_x = _dot(dt, e_ref[...], precision=hi)
    acs_x = _dot(tril, dt_x * (-jnp.exp(alogx_ref[...])), precision=hi)
    xd = xs * dt_x
    xd_b = xd.astype(BF16)

    row = lax.broadcasted_iota(jnp.int32, (L, L), 0)
    col = lax.broadcasted_iota(jnp.int32, (L, L), 1)
    causal = row >= col
    lane = lax.broadcasted_iota(jnp.int32, (L, LANES), 1)
    heads_per_group = SSM_HEADS // SSM_GROUPS
    y_parts = []
    cb = [_dot_nt(cm[:, g * SSM_STATE:(g + 1) * SSM_STATE].astype(BF16),
                  bm[:, g * SSM_STATE:(g + 1) * SSM_STATE].astype(BF16)) for g in range(SSM_GROUPS)]
    for pair in range(SSM_HEADS // 2):
        xd_pair = xd_b[:, pair * LANES:(pair + 1) * LANES]
        outs = []
        for h in (2 * pair, 2 * pair + 1):
            decay = jnp.exp(jnp.where(causal, acs[:, h:h + 1] - acs_t[h:h + 1, :], -jnp.inf))
            outs.append(_dot((cb[h // heads_per_group] * decay).astype(BF16), xd_pair))
        y_parts.append(jnp.where(lane < SSM_HEAD_DIM, outs[0], outs[1]))
    y_diag = jnp.concatenate(y_parts, axis=1)

    st = st_sc[...]
    y = (y_diag + _ssd_y_off(st, cm, acs_x)) + dvec_ref[...] * xs
    y = y * _silu(z_ref[...])
    y_ref[...] = _group_rmsnorm(y, gn_ref[...]).astype(y_ref.dtype)
    st_new = _ssd_state_update(st, bm, xd, acs_x)
    st_sc[...] = st_new

    @pl.when(c == n_chunks - 1)
    def _():
        st_ref[...] = st_new.T


def _ssd_consts(conv_w, conv_b, dt_bias, a_log, d_vec, g_norm):
    pad = LANES - SSM_HEADS
    e_np = np.zeros((LANES, SSM_INNER), np.float32)
    for hh in range(SSM_HEADS):
        e_np[hh, hh * SSM_HEAD_DIM:(hh + 1) * SSM_HEAD_DIM] = 1.0
    e_mat = jnp.asarray(e_np)
    return dict(
        cw=conv_w, cb=conv_b.reshape(1, -1),
        dtb=jnp.pad(dt_bias, (0, pad)).reshape(1, LANES),
        alog=jnp.pad(a_log, (0, pad)).reshape(1, LANES),
        alogx=jnp.repeat(a_log, SSM_HEAD_DIM).reshape(1, SSM_INNER),
        dvec=jnp.repeat(d_vec, SSM_HEAD_DIM).reshape(1, SSM_INNER),
        gn=g_norm.reshape(1, SSM_INNER), e=e_mat)


def _full(shape):
    nd = len(shape)
    return pl.BlockSpec(shape, lambda *_: (0,) * nd)


def ssd_prompt(proj, consts, *, batch, seq):
    L = SSM_CHUNK
    nc = seq // L
    cdim = SSM_INNER + 2 * SSM_GROUPS * SSM_STATE
    tril = jnp.asarray(np.tril(np.ones((L, L), np.float32)))

    def rows(width, col):
        return pl.BlockSpec((L, width), lambda b, c: (b * nc + c, col // width))

    k = consts
    return pl.pallas_call(
        functools.partial(_ssd_prompt_body, n_chunks=nc),
        grid=(batch, nc),
        in_specs=[rows(SSM_INNER, COL_Z), rows(SSM_INNER, COL_X), rows(2 * SSM_GROUPS * SSM_STATE, COL_BC),
                  rows(LANES, COL_DT),
                  _full((SSM_CONV, cdim)), _full((1, cdim)), _full((1, LANES)), _full((1, LANES)),
                  _full((1, SSM_INNER)), _full((1, SSM_INNER)), _full((1, SSM_INNER)),
                  _full((LANES, SSM_INNER)), _full((L, L))],
        out_specs=[pl.BlockSpec((L, SSM_INNER), lambda b, c: (b * nc + c, 0)),
                   pl.BlockSpec((None, SSM_INNER, SSM_STATE), lambda b, c: (b, 0, 0))],
        out_shape=[jax.ShapeDtypeStruct((batch * seq, SSM_INNER), BF16),
                   jax.ShapeDtypeStruct((batch, SSM_INNER, SSM_STATE), F32)],
        scratch_shapes=[pltpu.VMEM((L + 8, cdim), F32), pltpu.VMEM((SSM_STATE, SSM_INNER), F32)],
        compiler_params=_cparams("parallel", "arbitrary"),
        name="ssd_prompt",
    )(proj, proj, proj, proj, k["cw"], k["cb"], k["dtb"], k["alog"], k["alogx"], k["dvec"], k["gn"], k["e"], tril)


def _sample_mixer_body(z_ref, dt_ref, scb_ref, xp_ref, up_c_ref, up_v_ref, st_in_ref,
                       cw_ref, cb_ref, dtb_ref, alogx_ref, dvec_ref, gn_ref, e_ref, scw_ref,
                       y_ref, ysc_ref, st_ref, u_out_ref, bpad_sc, xdpad_sc, u_sc, *, n_new):
    R = 8
    b = pl.program_id(0)

    @pl.when(b == 0)
    def _():
        bpad_sc[...] = jnp.zeros(bpad_sc.shape, F32)
        xdpad_sc[...] = jnp.zeros(xdpad_sc.shape, F32)

    conv = xp_ref[pl.ds(8 - (SSM_CONV - 1), R), :] * cw_ref[0:1, :]
    for k in range(1, SSM_CONV):
        conv = conv + xp_ref[pl.ds(8 - (SSM_CONV - 1) + k, R), :] * cw_ref[k:k + 1, :]
    xbc = _silu(conv + cb_ref[...])
    xs = xbc[:, :SSM_INNER]
    bm = xbc[:, SSM_INNER:SSM_INNER + SSM_GROUPS * SSM_STATE]
    cm = xbc[:, SSM_INNER + SSM_GROUPS * SSM_STATE:]

    hi = lax.Precision.HIGHEST
    rowl = lax.broadcasted_iota(jnp.int32, (R, LANES), 0)
    dt = jnp.where(rowl < n_new, _softplus(dt_ref[...] + dtb_ref[...]), 0.0)
    dt_x = _dot(dt, e_ref[...], precision=hi)
    da_x = dt_x * (-jnp.exp(alogx_ref[...]))
    rowx = lax.broadcasted_iota(jnp.int32, (R, SSM_INNER), 0)
    acs_x = jnp.zeros((R, SSM_INNER), F32)
    for s in range(n_new):
        acs_x = acs_x + jnp.where(rowx >= s, da_x[s:s + 1, :], 0.0)
    xd = xs * dt_x

    gw = SSM_INNER // SSM_GROUPS
    lanex = lax.broadcasted_iota(jnp.int32, (R, SSM_INNER), 1)
    y_diag = jnp.zeros((R, SSM_INNER), F32)
    for s in range(n_new):
        cbs = [jnp.sum(cm[:, g * SSM_STATE:(g + 1) * SSM_STATE] * bm[s:s + 1, g * SSM_STATE:(g + 1) * SSM_STATE],
                       axis=1, keepdims=True) for g in range(SSM_GROUPS)]
        cb_x = jnp.where(lanex < gw, cbs[0], cbs[1])
        decay = jnp.exp(jnp.where(rowx >= s, acs_x - acs_x[s:s + 1, :], -jnp.inf))
        y_diag = y_diag + (cb_x * decay) * xd[s:s + 1, :]

    st = st_in_ref[...].T
    y_off_parts = [_dot(cm[:, g * SSM_STATE:(g + 1) * SSM_STATE], st[:, g * gw:(g + 1) * gw])
                   for g in range(SSM_GROUPS)]
    y_off = jnp.concatenate(y_off_parts, axis=1) * jnp.exp(acs_x)
    y = (y_diag + y_off) + dvec_ref[...] * xs
    y = y * _silu(z_ref[...])
    y_ref[...] = _group_rmsnorm(y, gn_ref[...])

    bpad_sc[0:R, :] = bm
    xdpad_sc[0:R, :] = xd * jnp.exp(acs_x[R - 1:R, :] - acs_x)
    upd = [_dot_tn(bpad_sc[:, g * SSM_STATE:(g + 1) * SSM_STATE], xdpad_sc[:, g * gw:(g + 1) * gw])
           for g in range(SSM_GROUPS)]
    st_new = st * jnp.exp(acs_x[R - 1:R, :]) + jnp.concatenate(upd, axis=1)
    st_ref[...] = st_new.T

    u_sc[...] = up_c_ref[...] * up_v_ref[...]
    sconv = u_sc[pl.ds(8 - (SC_CONV - 1), R), :] * scw_ref[0:1, :]
    for k in range(1, SC_CONV):
        sconv = sconv + u_sc[pl.ds(8 - (SC_CONV - 1) + k, R), :] * scw_ref[k:k + 1, :]
    ysc_ref[...] = scb_ref[...] * sconv
    u_out_ref[...] = u_sc[8:16, :]


def sample_mixer(proj_s, xp_s, up_c, up_v, st_all, layer, consts, sc_conv_w, *, n_new):
    nb = proj_s.shape[0]
    cdim = SSM_INNER + 2 * SSM_GROUPS * SSM_STATE
    k = consts

    def rows(width, col):
        return pl.BlockSpec((None, 8, width), lambda b: (b, 0, col // width))

    return pl.pallas_call(
        functools.partial(_sample_mixer_body, n_new=n_new),
        grid=(nb,),
        in_specs=[rows(SSM_INNER, COL_Z), rows(LANES, COL_DT), rows(SC_WIDTH, COL_SCB),
                  pl.BlockSpec((None, 16, cdim), lambda b: (b, 0, 0)),
                  pl.BlockSpec((None, 16, SC_WIDTH), lambda b: (b, 0, 0)),
                  pl.BlockSpec((None, 16, SC_WIDTH), lambda b: (b, 0, 0)),
                  pl.BlockSpec((None, None, SSM_INNER, SSM_STATE), lambda b: (layer, b, 0, 0)),
                  _full((SSM_CONV, cdim)), _full((1, cdim)), _full((1, LANES)),
                  _full((1, SSM_INNER)), _full((1, SSM_INNER)), _full((1, SSM_INNER)),
                  _full((LANES, SSM_INNER)), _full((SC_CONV, SC_WIDTH))],
        out_specs=[pl.BlockSpec((None, 8, SSM_INNER), lambda b: (b, 0, 0)),
                   pl.BlockSpec((None, 8, SC_WIDTH), lambda b: (b, 0, 0)),
                   pl.BlockSpec((None, SSM_INNER, SSM_STATE), lambda b: (b, 0, 0)),
                   pl.BlockSpec((None, 8, SC_WIDTH), lambda b: (b, 0, 0))],
        out_shape=[jax.ShapeDtypeStruct((nb, 8, SSM_INNER), F32), jax.ShapeDtypeStruct((nb, 8, SC_WIDTH), F32),
                   jax.ShapeDtypeStruct((nb, SSM_INNER, SSM_STATE), F32),
                   jax.ShapeDtypeStruct((nb, 8, SC_WIDTH), F32)],
        scratch_shapes=[pltpu.VMEM((LANES, SSM_GROUPS * SSM_STATE), F32), pltpu.VMEM((LANES, SSM_INNER), F32),
                        pltpu.VMEM((16, SC_WIDTH), F32)],
        compiler_params=_cparams("arbitrary"),
        name="sample_mixer",
    )(proj_s, proj_s, proj_s, xp_s, up_c, up_v, st_all,
      k["cw"], k["cb"], k["dtb"], k["alogx"], k["dvec"], k["gn"], k["e"], sc_conv_w)


def _sconv_prompt_body(scb_ref, scc_ref, scv_ref, pc_ref, pv_ref, w_ref, y_ref, tail_ref, u_sc, *, ts):
    i = pl.program_id(1)
    hist = pc_ref[...] * pv_ref[...]
    u_sc[0:8, :] = jnp.where(i == 0, 0.0, hist)
    u_sc[8:8 + ts, :] = scc_ref[...] * scv_ref[...]
    conv = u_sc[pl.ds(8 - (SC_CONV - 1), ts), :] * w_ref[0:1, :]
    for k in range(1, SC_CONV):
        conv = conv + u_sc[pl.ds(8 - (SC_CONV - 1) + k, ts), :] * w_ref[k:k + 1, :]
    y_ref[...] = (scb_ref[...] * conv).astype(y_ref.dtype)
    tail_ref[...] = u_sc[ts:ts + 8, :]


def sconv_prompt(proj, sc_conv_w, *, batch, seq, ts):
    nt = seq // ts
    w = SC_WIDTH

    def rows(col):
        return pl.BlockSpec((ts, w), lambda b, i: (b * nt + i, col // w))

    def prev(col):
        return pl.BlockSpec((8, w), lambda b, i: (jnp.maximum((b * nt + i) * (ts // 8) - 1, 0), col // w))

    return pl.pallas_call(
        functools.partial(_sconv_prompt_body, ts=ts),
        grid=(batch, nt),
        in_specs=[rows(COL_SCB), rows(COL_SCC), rows(COL_SCV), prev(COL_SCC), prev(COL_SCV), _full((SC_CONV, w))],
        out_specs=[pl.BlockSpec((ts, w), lambda b, i: (b * nt + i, 0)),
                   pl.BlockSpec((None, 8, w), lambda b, i: (b, 0, 0))],
        out_shape=[jax.ShapeDtypeStruct((batch * seq, w), BF16), jax.ShapeDtypeStruct((batch, 8, w), F32)],
        scratch_shapes=[pltpu.VMEM((ts + 8, w), F32)],
        compiler_params=_cparams("parallel", "arbitrary"),
        name="sconv_prompt",
    )(proj, proj, proj, proj, proj, sc_conv_w)


def _merge_body(h_ref, yap_ref, ysp_ref, ycp_ref, yas_ref, yss_ref, ycs_ref,
                wga_ref, wgb_ref, wgc_ref, ba_ref, bb_ref, bc_ref, wa_ref, wb_ref, wc_ref, o_ref,
                wg_sc, wb_sc, *, n_prompt_tiles):
    i = pl.program_id(1)

    @pl.when(i == 0)
    def _():
        for n, r in enumerate((wga_ref, wgb_ref, wgc_ref)):
            wg_sc[n] = r[...].astype(BF16)
        for n, r in enumerate((wa_ref, wb_ref, wc_ref)):
            wb_sc[n] = r[...].astype(BF16)

    h = h_ref[...]
    is_sample = i >= n_prompt_tiles
    out = None
    for n, (p_ref, s_ref, b_ref) in enumerate(((yap_ref, yas_ref, ba_ref), (ysp_ref, yss_ref, bb_ref),
                                               (ycp_ref, ycs_ref, bc_ref))):
        y = jnp.where(is_sample, s_ref[...], p_ref[...])
        term = _sigmoid(_dot(h, wg_sc[n]) + b_ref[...]) * _dot(y, wb_sc[n])
        out = term if out is None else out + term
    o_ref[...] = out.astype(o_ref.dtype)


def gated_merge(h, y_prompt, y_sample, w_gate, b_gate, w_branches, *, tm, tn):
    m, d = h.shape
    mp, kb = y_prompt[0].shape
    nb = d // tn
    npt = mp // tm
    b2 = b_gate.reshape(1, -1)
    hspec = pl.BlockSpec((tm, d), lambda j, i: (i, 0))
    pspec = pl.BlockSpec((tm, kb), lambda j, i: (jnp.minimum(i, npt - 1), 0))
    sspec = pl.BlockSpec((tm, kb), lambda j, i: (jnp.maximum(i - npt, 0), 0))

    def wcol(rows_, off):
        return pl.BlockSpec((rows_, tn), lambda j, i: (0, off * nb + j))

    return pl.pallas_call(
        functools.partial(_merge_body, n_prompt_tiles=npt),
        grid=(nb, m // tm),
        in_specs=[hspec, pspec, pspec, pspec, sspec, sspec, sspec,
                  wcol(d, 0), wcol(d, 1), wcol(d, 2), wcol(1, 0), wcol(1, 1), wcol(1, 2),
                  wcol(kb, 0), wcol(kb, 0), wcol(kb, 0)],
        out_specs=pl.BlockSpec((tm, tn), lambda j, i: (i, j)),
        out_shape=jax.ShapeDtypeStruct((m, d), BF16),
        scratch_shapes=[pltpu.VMEM((3, d, tn), BF16), pltpu.VMEM((3, kb, tn), BF16)],
        compiler_params=_cparams("parallel", "arbitrary"),
        name="gated_merge",
    )(h, *y_prompt, *y_sample, w_gate, w_gate, w_gate, b2, b2, b2, *w_branches)


def _rope_tables(pos):
    half = QK_ROPE // 2
    inv = ROPE_THETA ** (-jnp.arange(half, dtype=F32) / half)
    ang = pos.astype(F32)[:, None] * inv[None, :]
    c, s = jnp.cos(ang), jnp.sin(ang)
    z = jnp.zeros((pos.shape[0], LANES - QK_ROPE), F32)
    return jnp.concatenate([c, c, z], axis=1), jnp.concatenate([-s, s, z], axis=1)


def _reorder_w_in(w):
    sizes = (Q_LORA, KV_LORA, QK_ROPE, SSM_INNER, SSM_INNER + 2 * SSM_GROUPS * SSM_STATE, SSM_HEADS,
             SC_WIDTH, SC_WIDTH, SC_WIDTH)
    q_c, kv_c, k_pe, z, xbc, dt, sc_b, sc_c, sc_v = jnp.split(w, np.cumsum(sizes)[:-1].tolist(), axis=1)
    d = w.shape[0]
    out = jnp.concatenate([sc_b, sc_c, sc_v, z, xbc, q_c, kv_c,
                           k_pe, jnp.zeros((d, LANES - QK_ROPE), w.dtype),
                           dt, jnp.zeros((d, LANES - SSM_HEADS), w.dtype)], axis=1)
    return out.astype(BF16)


def kernel(x_prompt, x_sample, cache_ckv, cache_kpe, state_ssm, state_mconv, state_sconv, page_table, g_attn_norm, w_in, g_q_a, w_q_b, g_kv_a, w_kv_b, ssm_conv_w, ssm_conv_b, ssm_dt_bias, ssm_a_log, ssm_d, g_ssm_norm, sc_conv_w, w_gate, b_gate, w_br_attn, w_br_ssm, w_br_sc, w_o, g_ffn_norm, w_ff_gate, w_ff_up, w_ff_down, w_router, w_e_gate, w_e_up, w_e_down, g_final):
    bp, tp, d = x_prompt.shape
    bs, ts, _ = x_sample.shape
    depth = w_in.shape[0]
    mp, ms = bp * tp, bs * ts
    m = mp + ms
    n_past = page_table.shape[1] * cache_ckv.shape[2]
    tm = m // 8
    tmh = m // 16
    cdim = SSM_INNER + 2 * SSM_GROUPS * SSM_STATE

    x = jnp.concatenate([x_prompt.reshape(mp, d), x_sample.reshape(ms, d)], axis=0)
    pos = jnp.concatenate([jnp.tile(jnp.arange(tp), bp), jnp.tile(n_past + jnp.arange(ts), bs)])
    cos, sin = _rope_tables(pos)
    cache_kpe_t = jnp.swapaxes(cache_kpe, 2, 3)
    st_all = state_ssm.reshape(depth, bs, SSM_INNER, SSM_STATE)

    outs = {k: [] for k in ("p_ckv", "p_kpe", "p_ssm", "p_mconv", "p_sconv", "s_ckv", "s_kpe", "s_ssm", "s_mconv", "s_sconv")}
    for l in range(depth):
        w_in_l = _reorder_w_in(w_in[l])
        wq = w_q_b[l]
        w_q_slots = jnp.concatenate([wq, jnp.zeros(wq.shape[:2] + (Q_SLOT - wq.shape[2],), wq.dtype)], axis=2)
        w_q_slots = w_q_slots.reshape(Q_LORA, MLA_HEADS * Q_SLOT).astype(BF16)
        wkv = w_kv_b[l]
        w_kv_flat = jnp.concatenate([wkv[..., :QK_NOPE].reshape(KV_LORA, -1), wkv[..., QK_NOPE:].reshape(KV_LORA, -1)],
                                    axis=1).astype(BF16)
        w_uk_t = jnp.transpose(wkv[..., :QK_NOPE], (1, 2, 0)).astype(BF16)
        w_uv = jnp.transpose(wkv[..., QK_NOPE:], (1, 0, 2)).astype(BF16)
        consts = _ssd_consts(ssm_conv_w[l], ssm_conv_b[l], ssm_dt_bias[l], ssm_a_log[l], ssm_d[l], g_ssm_norm[l])

        h = rmsnorm(x, g_attn_norm[l], BF16, tm)
        proj = matmul(h, w_in_l, tm=tm, tn=768)
        qn, ckv, kpe = mla_prep(proj, g_q_a[l], g_kv_a[l], cos, sin, tm=tm)
        q = qproj(qn, w_q_slots, cos, sin, tm=tm)

        k_full, v_full = kv_expand(ckv, kpe, w_kv_flat, rows=mp, tm=1024)
        ya_p = flash_attention(q, k_full, v_full, batch=bp, seq=tp, blk=512, heads=2)

        q_lat = blockdiag_matmul(q, w_uk_t, tm=ms, row_block=mp // ms, col_block0=0, col_stride=2)
        q_pe_s = q[mp:].reshape(ms, MLA_HEADS, Q_SLOT)[:, :, QK_NOPE:].reshape(bs, ts * MLA_HEADS, LANES)
        ckv_s = ckv[mp:].reshape(bs, ts, KV_LORA)
        kpe_s = kpe[mp:].reshape(bs, ts, LANES)
        new_c = jnp.pad(ckv_s, ((0, 0), (0, 8 - ts), (0, 0)))
        new_k = jnp.pad(kpe_s, ((0, 0), (0, 8 - ts), (0, 0)))
        o_lat = decode_attention(q_lat.reshape(bs, ts * MLA_HEADS, KV_LORA), q_pe_s, cache_ckv, cache_kpe_t, l,
                                 page_table, new_c, new_k, pages_per_step=32)
        ya_s = blockdiag_matmul(o_lat.reshape(ms, MLA_HEADS * KV_LORA), w_uv, tm=ms, row_block=0, col_block0=0,
                                col_stride=1)

        ys_p, ssm_p = ssd_prompt(proj, consts, batch=bp, seq=tp)
        yc_p, u_tail_p = sconv_prompt(proj, sc_conv_w[l], batch=bp, seq=tp, ts=512)

        proj_s = jnp.pad(proj[mp:].reshape(bs, ts, D_IN_PAD), ((0, 0), (0, 8 - ts), (0, 0)))
        xbc_s = proj_s[:, :ts, COL_X:COL_X + cdim]
        xp_s = jnp.concatenate([jnp.zeros((bs, 8 - (SSM_CONV - 1), cdim), F32), state_mconv[l], xbc_s,
                                jnp.zeros((bs, 8 - ts, cdim), F32)], axis=1)
        zpad = jnp.zeros((bs, 8 - (SC_CONV - 1), SC_WIDTH), F32)
        zend = jnp.zeros((bs, 8 - ts, SC_WIDTH), F32)
        up_c = jnp.concatenate([zpad, state_sconv[l], proj_s[:, :ts, COL_SCC:COL_SCC + SC_WIDTH], zend], axis=1)
        up_v = jnp.concatenate([zpad, jnp.ones_like(state_sconv[l]), proj_s[:, :ts, COL_SCV:COL_SCV + SC_WIDTH], zend], axis=1)
        ys_s, yc_s, ssm_s, u_new_s = sample_mixer(proj_s, xp_s, up_c, up_v, st_all, l, consts, sc_conv_w[l], n_new=ts)

        y_sample = (ya_s, ys_s[:, :ts].reshape(ms, SSM_INNER).astype(BF16), yc_s[:, :ts].reshape(ms, SC_WIDTH).astype(BF16))
        merged = gated_merge(h, (ya_p, ys_p, yc_p), y_sample, w_gate[l], b_gate[l],
                             (w_br_attn[l], w_br_ssm[l], w_br_sc[l]), tm=ms, tn=256)
        x = matmul(merged, w_o[l], tm=tm, tn=512, res=x)

        outs["p_ckv"].append(ckv[:mp].reshape(bp, tp, KV_LORA))
        outs["p_kpe"].append(kpe[:mp, :QK_ROPE].reshape(bp, tp, QK_ROPE))
        outs["p_ssm"].append(ssm_p.reshape(bp, SSM_HEADS, SSM_HEAD_DIM, SSM_STATE))
        outs["p_mconv"].append(jnp.stack([proj[(b + 1) * tp - (SSM_CONV - 1):(b + 1) * tp, COL_X:COL_X + cdim]
                                          for b in range(bp)]))
        outs["p_sconv"].append(u_tail_p[:, 8 - (SC_CONV - 1):])
        outs["s_ckv"].append(ckv_s)
        outs["s_kpe"].append(kpe_s[:, :, :QK_ROPE])
        outs["s_ssm"].append(ssm_s.reshape(bs, SSM_HEADS, SSM_HEAD_DIM, SSM_STATE))
        outs["s_mconv"].append(xp_s[:, 8 + ts - (SSM_CONV - 1):8 + ts])
        outs["s_sconv"].append(u_new_s[:, ts - (SC_CONV - 1):ts])

        i = l // 2
        if l % 2 == 0:
            h2 = rmsnorm(x, g_ffn_norm[l], BF16, tm)
            hdn = swiglu_up(h2, w_ff_gate[i], w_ff_up[i], tm=tm, tf=512)
            x = matmul(hdn, w_ff_down[i].astype(BF16), tm=tm, tn=256, res=x, rows_outer=True)
        else:
            h2, gate = rmsnorm_router(x, g_ffn_norm[l], w_router[i], tm)
            hdn = moe_up(h2, w_e_gate[i].astype(BF16), w_e_up[i].astype(BF16), gate, tm=tmh)
            wd = w_e_down[i].reshape(-1, d).astype(BF16)
            x = matmul(hdn, wd, tm=tmh, tn=256, res=x, rows_outer=True)

    y_p, y_s = rmsnorm_split(x, g_final, rows_prompt=mp, tm=ms)
    st = {k: jnp.stack(v, axis=0) for k, v in outs.items()}
    return (y_p.reshape(bp, tp, d), y_s.reshape(bs, ts, d),
            st["p_ckv"], st["p_kpe"], st["p_ssm"], st["p_mconv"], st["p_sconv"],
            st["s_ckv"], st["s_kpe"], st["s_ssm"], st["s_mconv"], st["s_sconv"])
```

```python
import functools

import jax
import jax.numpy as jnp
import numpy as np
from jax import lax
from jax.experimental import pallas as pl
from jax.experimental.pallas import tpu as pltpu

F32 = jnp.float32
BF16 = jnp.bfloat16
EPS = 1e-6
ROPE_THETA = 10000.0
LANES = 128
MLA_HEADS = 8
QK_NOPE = 128
QK_ROPE = 64
V_HEAD = 128
KV_LORA = 512
Q_LORA = 512
Q_SLOT = 256
SSM_HEADS = 16
SSM_HEAD_DIM = 64
SSM_INNER = 1024
SSM_GROUPS = 2
SSM_STATE = 128
SSM_CONV = 4
SSM_CHUNK = 128
SC_WIDTH = 1024
SC_CONV = 3
N_EXPERTS = 8
ATTN_SCALE = (QK_NOPE + QK_ROPE) ** -0.5
Q_SCALE = ATTN_SCALE * float(np.log2(np.e))
VMEM_LIMIT = 56 * 1024 * 1024

COL_SCB, COL_SCC, COL_SCV, COL_Z, COL_X, COL_BC, COL_QC, COL_KVC, COL_KPE, COL_DT = (
    0, 1024, 2048, 3072, 4096, 5120, 5632, 6144, 6656, 6784)
D_IN_PAD = 6912


def _cparams(*sem):
    return pltpu.CompilerParams(dimension_semantics=sem, vmem_limit_bytes=VMEM_LIMIT)


def _sigmoid(x):
    return 1.0 / (1.0 + jnp.exp(-x))


def _silu(x):
    return x * _sigmoid(x)


def _softplus(x):
    return jnp.maximum(x, 0.0) + jnp.log1p(jnp.exp(-jnp.abs(x)))


def _dot(a, b, **kw):
    return jnp.dot(a, b, preferred_element_type=F32, **kw)


def _dot_nt(a, b):
    return lax.dot_general(a, b, (((1,), (1,)), ((), ())), preferred_element_type=F32)


def _dot_tn(a, b):
    return lax.dot_general(a, b, (((0,), (0,)), ((), ())), preferred_element_type=F32)


def _rms(x, g):
    r = lax.rsqrt(jnp.mean(x * x, axis=-1, keepdims=True) + EPS)
    return (x * r) * g


def _rmsnorm_body(x_ref, g_ref, o_ref):
    o_ref[...] = _rms(x_ref[...], g_ref[...]).astype(o_ref.dtype)


def rmsnorm(x, g, out_dtype, tm):
    m, d = x.shape
    return pl.pallas_call(
        _rmsnorm_body,
        grid=(m // tm,),
        in_specs=[pl.BlockSpec((tm, d), lambda i: (i, 0)), pl.BlockSpec((1, d), lambda i: (0, 0))],
        out_specs=pl.BlockSpec((tm, d), lambda i: (i, 0)),
        out_shape=jax.ShapeDtypeStruct((m, d), out_dtype),
        compiler_params=_cparams("parallel"),
        name="rmsnorm",
    )(x, g.reshape(1, d))


def _rmsnorm_split_body(x_ref, g_ref, op_ref, os_ref, *, n_prompt_tiles):
    i = pl.program_id(0)
    y = _rms(x_ref[...], g_ref[...])

    @pl.when(i < n_prompt_tiles)
    def _():
        op_ref[...] = y

    @pl.when(i >= n_prompt_tiles)
    def _():
        os_ref[...] = y


def rmsnorm_split(x, g, *, rows_prompt, tm):
    m, d = x.shape
    npt = rows_prompt // tm
    return pl.pallas_call(
        functools.partial(_rmsnorm_split_body, n_prompt_tiles=npt),
        grid=(m // tm,),
        in_specs=[pl.BlockSpec((tm, d), lambda i: (i, 0)), pl.BlockSpec((1, d), lambda i: (0, 0))],
        out_specs=[pl.BlockSpec((tm, d), lambda i: (jnp.minimum(i, npt - 1), 0)),
                   pl.BlockSpec((tm, d), lambda i: (jnp.maximum(i - npt, 0), 0))],
        out_shape=[jax.ShapeDtypeStruct((rows_prompt, d), F32), jax.ShapeDtypeStruct((m - rows_prompt, d), F32)],
        compiler_params=_cparams("arbitrary"),
        name="rmsnorm_split",
    )(x, g.reshape(1, d))


def _rmsnorm_router_body(x_ref, g_ref, wr_ref, h_ref, gate_ref):
    h = _rms(x_ref[...], g_ref[...])
    h_ref[...] = h.astype(h_ref.dtype)
    lg = _dot(h, wr_ref[...], precision=lax.Precision.HIGHEST)
    lane = lax.broadcasted_iota(jnp.int32, lg.shape, 1).astype(F32)
    lg = jnp.where(lane < N_EXPERTS, lg, -jnp.inf)
    m1 = jnp.max(lg, axis=1, keepdims=True)
    i1 = jnp.min(jnp.where(lg == m1, lane, float(LANES)), axis=1, keepdims=True)
    oh1 = lane == i1
    lg2 = jnp.where(oh1, -jnp.inf, lg)
    m2 = jnp.max(lg2, axis=1, keepdims=True)
    i2 = jnp.min(jnp.where(lg2 == m2, lane, float(LANES)), axis=1, keepdims=True)
    oh2 = lane == i2
    e = jnp.exp(m2 - m1)
    w1 = 1.0 / (1.0 + e)
    w2 = e / (1.0 + e)
    gate_ref[...] = jnp.where(oh1, w1, 0.0) + jnp.where(oh2, w2, 0.0)


def rmsnorm_router(x, g, w_router, tm):
    m, d = x.shape
    wr = jnp.pad(w_router, ((0, 0), (0, LANES - w_router.shape[1])))
    return pl.pallas_call(
        _rmsnorm_router_body,
        grid=(m // tm,),
        in_specs=[pl.BlockSpec((tm, d), lambda i: (i, 0)), pl.BlockSpec((1, d), lambda i: (0, 0)),
                  pl.BlockSpec((d, LANES), lambda i: (0, 0))],
        out_specs=[pl.BlockSpec((tm, d), lambda i: (i, 0)), pl.BlockSpec((tm, LANES), lambda i: (i, 0))],
        out_shape=[jax.ShapeDtypeStruct((m, d), BF16), jax.ShapeDtypeStruct((m, LANES), F32)],
        compiler_params=_cparams("parallel"),
        name="rmsnorm_router",
    )(x, g.reshape(1, d), wr)


def _cast_once(src_refs, dst_refs):
    @pl.when(pl.program_id(1) == 0)
    def _():
        for s, d in zip(src_refs, dst_refs):
            d[...] = s[...].astype(d.dtype)


def _mm_body(x_ref, w_ref, *rest, has_res, cast_w):
    rest = list(rest)
    r_ref = rest.pop(0) if has_res else None
    o_ref = rest.pop(0)
    if cast_w:
        (w_sc,) = rest
        _cast_once([w_ref], [w_sc])
        w_ref = w_sc
    acc = _dot(x_ref[...], w_ref[...])
    o_ref[...] = ((r_ref[...] + acc) if has_res else acc).astype(o_ref.dtype)


def matmul(x, w, *, tm, tn, res=None, out_dtype=F32, rows_outer=False, w_index=None):
    m, kd = x.shape
    n = w.shape[-1]
    cast_w = w.dtype != BF16
    assert not (cast_w and rows_outer)

    def ij(a, b):
        return (a, b) if rows_outer else (b, a)

    if w_index is None:
        w_spec = pl.BlockSpec((kd, tn), lambda a, b: (0, ij(a, b)[1]))
    else:
        w_spec = pl.BlockSpec((None, kd, tn), lambda a, b: (w_index, 0, ij(a, b)[1]))
    in_specs = [pl.BlockSpec((tm, kd), lambda a, b: (ij(a, b)[0], 0)), w_spec]
    args = [x, w]
    if res is not None:
        in_specs.append(pl.BlockSpec((tm, tn), lambda a, b: ij(a, b)))
        args.append(res)
    return pl.pallas_call(
        functools.partial(_mm_body, has_res=res is not None, cast_w=cast_w),
        grid=(m // tm, n // tn) if rows_outer else (n // tn, m // tm),
        in_specs=in_specs,
        out_specs=pl.BlockSpec((tm, tn), lambda a, b: ij(a, b)),
        out_shape=jax.ShapeDtypeStruct((m, n), out_dtype),
        scratch_shapes=[pltpu.VMEM((kd, tn), BF16)] if cast_w else [],
        compiler_params=_cparams("parallel", "arbitrary"),
        name="matmul",
    )(*args)


def _swiglu_body(x_ref, wg_ref, wu_ref, *rest, scaled, cast_w):
    rest = list(rest)
    gate_ref = rest.pop(0) if scaled else None
    o_ref = rest.pop(0)
    if cast_w:
        _cast_once([wg_ref, wu_ref], rest)
        wg_ref, wu_ref = rest
    x = x_ref[...]
    g = _dot(x, wg_ref[...])
    u = _dot(x, wu_ref[...])
    hdn = _silu(g) * u
    if scaled:
        gate = gate_ref[...]
        lane = lax.broadcasted_iota(jnp.int32, gate.shape, 1)
        sc = jnp.sum(jnp.where(lane == pl.program_id(0), gate, 0.0), axis=1, keepdims=True)
        hdn = hdn * sc
    o_ref[...] = hdn.astype(o_ref.dtype)


def swiglu_up(x, wg, wu, *, tm, tf):
    m, d = x.shape
    f = wg.shape[1]
    return pl.pallas_call(
        functools.partial(_swiglu_body, scaled=False, cast_w=True),
        grid=(f // tf, m // tm),
        in_specs=[pl.BlockSpec((tm, d), lambda j, i: (i, 0)), pl.BlockSpec((d, tf), lambda j, i: (0, j)),
                  pl.BlockSpec((d, tf), lambda j, i: (0, j))],
        out_specs=pl.BlockSpec((tm, tf), lambda j, i: (i, j)),
        out_shape=jax.ShapeDtypeStruct((m, f), BF16),
        scratch_shapes=[pltpu.VMEM((d, tf), BF16), pltpu.VMEM((d, tf), BF16)],
        compiler_params=_cparams("parallel", "arbitrary"),
        name="swiglu_up",
    )(x, wg, wu)


def moe_up(x, wg, wu, gate, *, tm):
    m, d = x.shape
    ne, _, f = wg.shape
    return pl.pallas_call(
        functools.partial(_swiglu_body, scaled=True, cast_w=False),
        grid=(ne, m // tm),
        in_specs=[pl.BlockSpec((tm, d), lambda j, i: (i, 0)), pl.BlockSpec((None, d, f), lambda j, i: (j, 0, 0)),
                  pl.BlockSpec((None, d, f), lambda j, i: (j, 0, 0)), pl.BlockSpec((tm, LANES), lambda j, i: (i, 0))],
        out_specs=pl.BlockSpec((tm, f), lambda j, i: (i, j)),
        out_shape=jax.ShapeDtypeStruct((m, ne * f), BF16),
        compiler_params=_cparams("parallel", "parallel"),
        name="moe_up",
    )(x, wg, wu, gate)


def _blockdiag_body(x_ref, w_ref, o_ref):
    o_ref[...] = _dot(x_ref[...], w_ref[...]).astype(o_ref.dtype)


def blockdiag_matmul(x, w, *, tm, row_block, col_block0, col_stride, out_dtype=BF16):
    nh, ki, no = w.shape
    return pl.pallas_call(
        _blockdiag_body,
        grid=(nh,),
        in_specs=[pl.BlockSpec((tm, ki), lambda h: (row_block, col_block0 + h * col_stride)),
                  pl.BlockSpec((None, ki, no), lambda h: (h, 0, 0))],
        out_specs=pl.BlockSpec((tm, no), lambda h: (0, h)),
        out_shape=jax.ShapeDtypeStruct((tm, nh * no), out_dtype),
        compiler_params=_cparams("parallel"),
        name="blockdiag_matmul",
    )(x, w)


def _rope_slab(x, cos, sin):
    half = QK_ROPE // 2
    lane = lax.broadcasted_iota(jnp.int32, x.shape, 1)
    swapped = jnp.where(lane < half, pltpu.roll(x, LANES - half, 1), pltpu.roll(x, half, 1))
    return x * cos + swapped * sin


def _mla_prep_body(qc_ref, kvc_ref, kpe_ref, gq_ref, gkv_ref, cos_ref, sin_ref, qn_ref, ckv_ref, kpe_out_ref):
    qn_ref[...] = _rms(qc_ref[...], gq_ref[...]).astype(qn_ref.dtype)
    ckv_ref[...] = _rms(kvc_ref[...], gkv_ref[...])
    kpe_out_ref[...] = _rope_slab(kpe_ref[...], cos_ref[...], sin_ref[...])


def mla_prep(proj, g_q, g_kv, cos, sin, *, tm):
    m = proj.shape[0]
    return pl.pallas_call(
        _mla_prep_body,
        grid=(m // tm,),
        in_specs=[pl.BlockSpec((tm, Q_LORA), lambda i: (i, COL_QC // Q_LORA)),
                  pl.BlockSpec((tm, KV_LORA), lambda i: (i, COL_KVC // KV_LORA)),
                  pl.BlockSpec((tm, LANES), lambda i: (i, COL_KPE // LANES)),
                  pl.BlockSpec((1, Q_LORA), lambda i: (0, 0)), pl.BlockSpec((1, KV_LORA), lambda i: (0, 0)),
                  pl.BlockSpec((tm, LANES), lambda i: (i, 0)), pl.BlockSpec((tm, LANES), lambda i: (i, 0))],
        out_specs=[pl.BlockSpec((tm, Q_LORA), lambda i: (i, 0)), pl.BlockSpec((tm, KV_LORA), lambda i: (i, 0)),
                   pl.BlockSpec((tm, LANES), lambda i: (i, 0))],
        out_shape=[jax.ShapeDtypeStruct((m, Q_LORA), BF16), jax.ShapeDtypeStruct((m, KV_LORA), F32),
                   jax.ShapeDtypeStruct((m, LANES), F32)],
        compiler_params=_cparams("parallel"),
        name="mla_prep",
    )(proj, proj, proj, g_q.reshape(1, -1), g_kv.reshape(1, -1), cos, sin)


def _qproj_body(x_ref, w_ref, cos_ref, sin_ref, o_ref):
    acc = _dot(x_ref[...], w_ref[...])
    cos = cos_ref[...] * Q_SCALE
    sin = sin_ref[...] * Q_SCALE
    for h in range(MLA_HEADS):
        base = h * Q_SLOT
        o_ref[:, base:base + QK_NOPE] = (acc[:, base:base + QK_NOPE] * Q_SCALE).astype(o_ref.dtype)
        o_ref[:, base + QK_NOPE:base + Q_SLOT] = _rope_slab(acc[:, base + QK_NOPE:base + Q_SLOT], cos, sin).astype(o_ref.dtype)


def qproj(qn, w_q_slots, cos, sin, *, tm):
    m = qn.shape[0]
    n = w_q_slots.shape[1]
    return pl.pallas_call(
        _qproj_body,
        grid=(m // tm,),
        in_specs=[pl.BlockSpec((tm, Q_LORA), lambda i: (i, 0)), pl.BlockSpec((Q_LORA, n), lambda i: (0, 0)),
                  pl.BlockSpec((tm, LANES), lambda i: (i, 0)), pl.BlockSpec((tm, LANES), lambda i: (i, 0))],
        out_specs=pl.BlockSpec((tm, n), lambda i: (i, 0)),
        out_shape=jax.ShapeDtypeStruct((m, n), BF16),
        compiler_params=_cparams("parallel"),
        name="qproj",
    )(qn, w_q_slots, cos, sin)


def _kv_expand_body(ckv_ref, kpe_ref, w_ref, k_ref, v_ref):
    acc = _dot(ckv_ref[...].astype(BF16), w_ref[...])
    kpe = kpe_ref[...].astype(k_ref.dtype)
    for h in range(MLA_HEADS):
        base = h * Q_SLOT
        k_ref[:, base:base + QK_NOPE] = acc[:, h * QK_NOPE:(h + 1) * QK_NOPE].astype(k_ref.dtype)
        k_ref[:, base + QK_NOPE:base + Q_SLOT] = kpe
    v_ref[...] = acc[:, MLA_HEADS * QK_NOPE:].astype(v_ref.dtype)


def kv_expand(ckv, kpe, w_kv_flat, *, rows, tm):
    n = w_kv_flat.shape[1]
    return pl.pallas_call(
        _kv_expand_body,
        grid=(rows // tm,),
        in_specs=[pl.BlockSpec((tm, KV_LORA), lambda i: (i, 0)), pl.BlockSpec((tm, LANES), lambda i: (i, 0)),
                  pl.BlockSpec((KV_LORA, n), lambda i: (0, 0))],
        out_specs=[pl.BlockSpec((tm, MLA_HEADS * Q_SLOT), lambda i: (i, 0)),
                   pl.BlockSpec((tm, MLA_HEADS * V_HEAD), lambda i: (i, 0))],
        out_shape=[jax.ShapeDtypeStruct((rows, MLA_HEADS * Q_SLOT), BF16),
                   jax.ShapeDtypeStruct((rows, MLA_HEADS * V_HEAD), BF16)],
        compiler_params=_cparams("parallel"),
        name="kv_expand",
    )(ckv, kpe, w_kv_flat)


def _flash_body(q_ref, k_ref, v_ref, o_ref, *, blk, heads):
    qi = pl.program_id(2)
    qs = [q_ref[:, h * Q_SLOT:(h + 1) * Q_SLOT] for h in range(heads)]

    def update(j, diagonal, h, m, l, acc):
        start = pl.multiple_of(j * blk, blk)
        k = k_ref[pl.ds(start, blk), h * Q_SLOT:(h + 1) * Q_SLOT]
        v = v_ref[pl.ds(start, blk), h * V_HEAD:(h + 1) * V_HEAD]
        s = _dot_nt(qs[h], k)
        if diagonal:
            row = lax.broadcasted_iota(jnp.int32, s.shape, 0)
            col = lax.broadcasted_iota(jnp.int32, s.shape, 1)
            s = jnp.where(row >= col, s, -jnp.inf)
        m_new = jnp.maximum(m, jnp.max(s, axis=1, keepdims=True))
        alpha = jnp.exp2(m - m_new)
        p = jnp.exp2(s - m_new)
        l = alpha * l + jnp.sum(p, axis=1, keepdims=True)
        acc = alpha * acc + _dot(p.astype(BF16), v)
        return m_new, l, acc

    def step(j, carry, diagonal):
        return tuple(update(j, diagonal, h, *carry[h]) for h in range(heads))

    init = tuple((jnp.full((blk, 1), -jnp.inf, F32), jnp.zeros((blk, 1), F32), jnp.zeros((blk, V_HEAD), F32))
                 for _ in range(heads))
    carry = lax.fori_loop(0, qi, lambda j, c: step(j, c, False), init)
    carry = step(qi, carry, True)
    for h in range(heads):
        _, l, acc = carry[h]
        o_ref[:, h * V_HEAD:(h + 1) * V_HEAD] = (acc / l).astype(o_ref.dtype)


def flash_attention(q, k, v, *, batch, seq, blk, heads):
    nq = seq // blk
    return pl.pallas_call(
        functools.partial(_flash_body, blk=blk, heads=heads),
        grid=(batch, MLA_HEADS // heads, nq),
        in_specs=[pl.BlockSpec((blk, heads * Q_SLOT), lambda b, h, i: (b * nq + i, h)),
                  pl.BlockSpec((seq, heads * Q_SLOT), lambda b, h, i: (b, h)),
                  pl.BlockSpec((seq, heads * V_HEAD), lambda b, h, i: (b, h))],
        out_specs=pl.BlockSpec((blk, heads * V_HEAD), lambda b, h, i: (b * nq + i, h)),
        out_shape=jax.ShapeDtypeStruct((batch * seq, MLA_HEADS * V_HEAD), BF16),
        compiler_params=_cparams("parallel", "parallel", "parallel"),
        name="flash_attention",
    )(q, k, v)


def _decode_body(pt_ref, q_ref, qpe_ref, newc_ref, newk_ref, ckv_hbm, kpe_hbm, o_ref,
                 kbuf, pbuf, sems, m_sc, l_sc, acc_sc, *, layer, pages_per_step, n_steps, n_new):
    pps = pages_per_step
    c = pl.program_id(1)
    n_total = pl.num_programs(0) * n_steps
    g = pl.program_id(0) * n_steps + c
    slot = lax.rem(g, 2)
    nxt = lax.rem(g + 1, n_total)

    def page_copies(chunk, slot_, i):
        pg = pt_ref[chunk * pps + i]
        return (pltpu.make_async_copy(ckv_hbm.at[layer, pg], kbuf.at[slot_, i], sems.at[slot_, 0]),
                pltpu.make_async_copy(kpe_hbm.at[layer, pg], pbuf.at[slot_, i], sems.at[slot_, 1]))

    @pl.when(g == 0)
    def _():
        for i in range(pps):
            for cp in page_copies(0, 0, i):
                cp.start()

    for i in range(pps):
        for cp in page_copies(g, slot, i):
            cp.wait()

    @pl.when(c == 0)
    def _():
        m_sc[...] = jnp.full(m_sc.shape, -jnp.inf, F32)
        l_sc[...] = jnp.zeros(l_sc.shape, F32)
        acc_sc[...] = jnp.zeros(acc_sc.shape, F32)

    q = q_ref[...]
    qp = qpe_ref[:, :QK_ROPE]
    ks, ss = [], []
    for i in range(pps):
        for cp in page_copies(nxt, 1 - slot, i):
            cp.start()
        k = kbuf[slot, i].astype(BF16)
        kp_t = pbuf[slot, i].astype(BF16)
        ks.append(k)
        ss.append(_dot_nt(q, k) + _dot(qp, kp_t))
    s = jnp.concatenate(ss, axis=1)
    m_prev = m_sc[:, :1]
    l_prev = l_sc[:, :1]
    m_new = jnp.maximum(m_prev, jnp.max(s, axis=1, keepdims=True))
    alpha = jnp.exp2(m_prev - m_new)
    p = jnp.exp2(s - m_new)
    l_new = alpha * l_prev + jnp.sum(p, axis=1, keepdims=True)
    page = ks[0].shape[0]
    pv = _dot(p[:, :page].astype(BF16), ks[0])
    for i in range(1, pps):
        pv = pv + _dot(p[:, i * page:(i + 1) * page].astype(BF16), ks[i])
    acc_new = alpha * acc_sc[...] + pv
    m_sc[...] = jnp.broadcast_to(m_new, m_sc.shape)
    l_sc[...] = jnp.broadcast_to(l_new, l_sc.shape)
    acc_sc[...] = acc_new

    @pl.when(c == n_steps - 1)
    def _():
        qf = q.astype(F32)
        qpf = qp.astype(F32)
        kn = newc_ref[...]
        kpn = newk_ref[:, :QK_ROPE]
        row = lax.broadcasted_iota(jnp.int32, (q.shape[0], 1), 0)
        sj = []
        for j in range(n_new):
            v = (jnp.sum(qf * kn[j:j + 1, :], axis=1, keepdims=True)
                 + jnp.sum(qpf * kpn[j:j + 1, :], axis=1, keepdims=True))
            sj.append(jnp.where(row >= j * MLA_HEADS, v, -jnp.inf))
        m_fin = m_new
        for v in sj:
            m_fin = jnp.maximum(m_fin, v)
        a2 = jnp.exp2(m_new - m_fin)
        l_fin = a2 * l_new
        acc_fin = a2 * acc_new
        for j in range(n_new):
            pj = jnp.exp2(sj[j] - m_fin)
            l_fin = l_fin + pj
            acc_fin = acc_fin + pj * kn[j:j + 1, :]
        o_ref[...] = (acc_fin / l_fin).astype(o_ref.dtype)

    @pl.when(g == n_total - 1)
    def _():
        for i in range(pps):
            for cp in page_copies(nxt, 1 - slot, i):
                cp.wait()


def decode_attention(q_lat, q_pe, cache_ckv, cache_kpe, layer, page_table, new_ckv, new_kpe, *, pages_per_step):
    nb, rows, _ = q_lat.shape
    n_pages = page_table.shape[1]
    page = cache_ckv.shape[2]
    pps = pages_per_step
    n_steps = n_pages // pps
    n_new = rows // MLA_HEADS

    assert n_pages == n_steps * pps
    in_specs = [pl.BlockSpec((None, rows, KV_LORA), lambda b, c, pt: (b, 0, 0)),
                pl.BlockSpec((None, rows, LANES), lambda b, c, pt: (b, 0, 0)),
                pl.BlockSpec((None, 8, KV_LORA), lambda b, c, pt: (b, 0, 0)),
                pl.BlockSpec((None, 8, LANES), lambda b, c, pt: (b, 0, 0)),
                pl.BlockSpec(memory_space=pl.ANY), pl.BlockSpec(memory_space=pl.ANY)]
    grid_spec = pltpu.PrefetchScalarGridSpec(
        num_scalar_prefetch=1,
        grid=(nb, n_steps),
        in_specs=in_specs,
        out_specs=pl.BlockSpec((None, rows, KV_LORA), lambda b, c, pt: (b, 0, 0)),
        scratch_shapes=[pltpu.VMEM((2, pps, page, KV_LORA), F32), pltpu.VMEM((2, pps, QK_ROPE, page), F32),
                        pltpu.SemaphoreType.DMA((2, 2)),
                        pltpu.VMEM((rows, LANES), F32), pltpu.VMEM((rows, LANES), F32),
                        pltpu.VMEM((rows, KV_LORA), F32)],
    )
    return pl.pallas_call(
        functools.partial(_decode_body, layer=layer, pages_per_step=pps, n_steps=n_steps, n_new=n_new),
        grid_spec=grid_spec,
        out_shape=jax.ShapeDtypeStruct((nb, rows, KV_LORA), BF16),
        compiler_params=_cparams("arbitrary", "arbitrary"),
        name="decode_attention",
    )(page_table.reshape(-1), q_lat, q_pe, new_ckv, new_kpe, cache_ckv, cache_kpe)


def _group_rmsnorm(y, g):
    gw = SSM_INNER // SSM_GROUPS
    parts = []
    for i in range(SSM_GROUPS):
        yg = y[:, i * gw:(i + 1) * gw]
        parts.append(yg * lax.rsqrt(jnp.mean(yg * yg, axis=-1, keepdims=True) + EPS))
    return jnp.concatenate(parts, axis=1) * g


def _ssd_state_update(st, bm, xd, acs_x):
    last = acs_x.shape[0] - 1
    xde = (xd * jnp.exp(acs_x[last:last + 1, :] - acs_x)).astype(BF16)
    gw = SSM_INNER // SSM_GROUPS
    upd = [_dot_tn(bm[:, g * SSM_STATE:(g + 1) * SSM_STATE].astype(BF16), xde[:, g * gw:(g + 1) * gw])
           for g in range(SSM_GROUPS)]
    return st * jnp.exp(acs_x[last:last + 1, :]) + jnp.concatenate(upd, axis=1)


def _ssd_y_off(st, cm, acs_x):
    gw = SSM_INNER // SSM_GROUPS
    parts = [_dot(cm[:, g * SSM_STATE:(g + 1) * SSM_STATE].astype(BF16), st[:, g * gw:(g + 1) * gw].astype(BF16))
             for g in range(SSM_GROUPS)]
    return jnp.concatenate(parts, axis=1) * jnp.exp(acs_x)


def _ssd_prompt_body(z_ref, x_ref, bc_ref, dt_ref, cw_ref, cb_ref, dtb_ref, alog_ref, alogx_ref, dvec_ref, gn_ref,
                     e_ref, tril_ref, y_ref, st_ref, xp_sc, st_sc, *, n_chunks):
    L = SSM_CHUNK
    c = pl.program_id(1)

    @pl.when(c == 0)
    def _():
        xp_sc[0:8, :] = jnp.zeros((8, xp_sc.shape[1]), F32)
        st_sc[...] = jnp.zeros(st_sc.shape, F32)

    @pl.when(c > 0)
    def _():
        xp_sc[0:8, :] = xp_sc[L:L + 8, :]

    xp_sc[8:8 + L, 0:SSM_INNER] = x_ref[...]
    xp_sc[8:8 + L, SSM_INNER:] = bc_ref[...]
    conv = xp_sc[pl.ds(8 - (SSM_CONV - 1), L), :] * cw_ref[0:1, :]
    for k in range(1, SSM_CONV):
        conv = conv + xp_sc[pl.ds(8 - (SSM_CONV - 1) + k, L), :] * cw_ref[k:k + 1, :]
    xbc = _silu(conv + cb_ref[...])
    xs = xbc[:, :SSM_INNER]
    bm = xbc[:, SSM_INNER:SSM_INNER + SSM_GROUPS * SSM_STATE]
    cm = xbc[:, SSM_INNER + SSM_GROUPS * SSM_STATE:]

    hi = lax.Precision.HIGHEST
    dt = _softplus(dt_ref[...] + dtb_ref[...])
    tril = tril_ref[...]
    acs = _dot(tril, dt * (-jnp.exp(alog_ref[...])), precision=hi)
    acs_t = acs.T
    dt_x = _dot(dt, e_ref[...], precision=hi)
    acs_x = _dot(tril, dt_x * (-jnp.exp(alogx_ref[...])), precision=hi)
    xd = xs * dt_x
    xd_b = xd.astype(BF16)

    row = lax.broadcasted_iota(jnp.int32, (L, L), 0)
    col = lax.broadcasted_iota(jnp.int32, (L, L), 1)
    causal = row >= col
    lane = lax.broadcasted_iota(jnp.int32, (L, LANES), 1)
    heads_per_group = SSM_HEADS // SSM_GROUPS
    y_parts = []
    cb = [_dot_nt(cm[:, g * SSM_STATE:(g + 1) * SSM_STATE].astype(BF16),
                  bm[:, g * SSM_STATE:(g + 1) * SSM_STATE].astype(BF16)) for g in range(SSM_GROUPS)]
    for pair in range(SSM_HEADS // 2):
        xd_pair = xd_b[:, pair * LANES:(pair + 1) * LANES]
        outs = []
        for h in (2 * pair, 2 * pair + 1):
            decay = jnp.exp(jnp.where(causal, acs[:, h:h + 1] - acs_t[h:h + 1, :], -jnp.inf))
            outs.append(_dot((cb[h // heads_per_group] * decay).astype(BF16), xd_pair))
        y_parts.append(jnp.where(lane < SSM_HEAD_DIM, outs[0], outs[1]))
    y_diag = jnp.concatenate(y_parts, axis=1)

    st = st_sc[...]
    y = (y_diag + _ssd_y_off(st, cm, acs_x)) + dvec_ref[...] * xs
    y = y * _silu(z_ref[...])
    y_ref[...] = _group_rmsnorm(y, gn_ref[...]).astype(y_ref.dtype)
    st_new = _ssd_state_update(st, bm, xd, acs_x)
    st_sc[...] = st_new

    @pl.when(c == n_chunks - 1)
    def _():
        st_ref[...] = st_new.T


def _ssd_consts(conv_w, conv_b, dt_bias, a_log, d_vec, g_norm):
    pad = LANES - SSM_HEADS
    e_np = np.zeros((LANES, SSM_INNER), np.float32)
    for hh in range(SSM_HEADS):
        e_np[hh, hh * SSM_HEAD_DIM:(hh + 1) * SSM_HEAD_DIM] = 1.0
    e_mat = jnp.asarray(e_np)
    return dict(
        cw=conv_w, cb=conv_b.reshape(1, -1),
        dtb=jnp.pad(dt_bias, (0, pad)).reshape(1, LANES),
        alog=jnp.pad(a_log, (0, pad)).reshape(1, LANES),
        alogx=jnp.repeat(a_log, SSM_HEAD_DIM).reshape(1, SSM_INNER),
        dvec=jnp.repeat(d_vec, SSM_HEAD_DIM).reshape(1, SSM_INNER),
        gn=g_norm.reshape(1, SSM_INNER), e=e_mat)


def _full(shape):
    nd = len(shape)
    return pl.BlockSpec(shape, lambda *_: (0,) * nd)


def ssd_prompt(proj, consts, *, batch, seq):
    L = SSM_CHUNK
    nc = seq // L
    cdim = SSM_INNER + 2 * SSM_GROUPS * SSM_STATE
    tril = jnp.asarray(np.tril(np.ones((L, L), np.float32)))

    def rows(width, col):
        return pl.BlockSpec((L, width), lambda b, c: (b * nc + c, col // width))

    k = consts
    return pl.pallas_call(
        functools.partial(_ssd_prompt_body, n_chunks=nc),
        grid=(batch, nc),
        in_specs=[rows(SSM_INNER, COL_Z), rows(SSM_INNER, COL_X), rows(2 * SSM_GROUPS * SSM_STATE, COL_BC),
                  rows(LANES, COL_DT),
                  _full((SSM_CONV, cdim)), _full((1, cdim)), _full((1, LANES)), _full((1, LANES)),
                  _full((1, SSM_INNER)), _full((1, SSM_INNER)), _full((1, SSM_INNER)),
                  _full((LANES, SSM_INNER)), _full((L, L))],
        out_specs=[pl.BlockSpec((L, SSM_INNER), lambda b, c: (b * nc + c, 0)),
                   pl.BlockSpec((None, SSM_INNER, SSM_STATE), lambda b, c: (b, 0, 0))],
        out_shape=[jax.ShapeDtypeStruct((batch * seq, SSM_INNER), BF16),
                   jax.ShapeDtypeStruct((batch, SSM_INNER, SSM_STATE), F32)],
        scratch_shapes=[pltpu.VMEM((L + 8, cdim), F32), pltpu.VMEM((SSM_STATE, SSM_INNER), F32)],
        compiler_params=_cparams("parallel", "arbitrary"),
        name="ssd_prompt",
    )(proj, proj, proj, proj, k["cw"], k["cb"], k["dtb"], k["alog"], k["alogx"], k["dvec"], k["gn"], k["e"], tril)


def _sample_mixer_body(z_ref, dt_ref, scb_ref, xp_ref, up_c_ref, up_v_ref, st_in_ref,
                       cw_ref, cb_ref, dtb_ref, alogx_ref, dvec_ref, gn_ref, e_ref, scw_ref,
                       y_ref, ysc_ref, st_ref, u_out_ref, bpad_sc, xdpad_sc, u_sc, *, n_new):
    R = 8
    b = pl.program_id(0)

    @pl.when(b == 0)
    def _():
        bpad_sc[...] = jnp.zeros(bpad_sc.shape, F32)
        xdpad_sc[...] = jnp.zeros(xdpad_sc.shape, F32)

    conv = xp_ref[pl.ds(8 - (SSM_CONV - 1), R), :] * cw_ref[0:1, :]
    for k in range(1, SSM_CONV):
        conv = conv + xp_ref[pl.ds(8 - (SSM_CONV - 1) + k, R), :] * cw_ref[k:k + 1, :]
    xbc = _silu(conv + cb_ref[...])
    xs = xbc[:, :SSM_INNER]
    bm = xbc[:, SSM_INNER:SSM_INNER + SSM_GROUPS * SSM_STATE]
    cm = xbc[:, SSM_INNER + SSM_GROUPS * SSM_STATE:]

    hi = lax.Precision.HIGHEST
    rowl = lax.broadcasted_iota(jnp.int32, (R, LANES), 0)
    dt = jnp.where(rowl < n_new, _softplus(dt_ref[...] + dtb_ref[...]), 0.0)
    dt_x = _dot(dt, e_ref[...], precision=hi)
    da_x = dt_x * (-jnp.exp(alogx_ref[...]))
    rowx = lax.broadcasted_iota(jnp.int32, (R, SSM_INNER), 0)
    acs_x = jnp.zeros((R, SSM_INNER), F32)
    for s in range(n_new):
        acs_x = acs_x + jnp.where(rowx >= s, da_x[s:s + 1, :], 0.0)
    xd = xs * dt_x

    gw = SSM_INNER // SSM_GROUPS
    lanex = lax.broadcasted_iota(jnp.int32, (R, SSM_INNER), 1)
    y_diag = jnp.zeros((R, SSM_INNER), F32)
    for s in range(n_new):
        cbs = [jnp.sum(cm[:, g * SSM_STATE:(g + 1) * SSM_STATE] * bm[s:s + 1, g * SSM_STATE:(g + 1) * SSM_STATE],
                       axis=1, keepdims=True) for g in range(SSM_GROUPS)]
        cb_x = jnp.where(lanex < gw, cbs[0], cbs[1])
        decay = jnp.exp(jnp.where(rowx >= s, acs_x - acs_x[s:s + 1, :], -jnp.inf))
        y_diag = y_diag + (cb_x * decay) * xd[s:s + 1, :]

    st = st_in_ref[...].T
    y_off_parts = [_dot(cm[:, g * SSM_STATE:(g + 1) * SSM_STATE], st[:, g * gw:(g + 1) * gw])
                   for g in range(SSM_GROUPS)]
    y_off = jnp.concatenate(y_off_parts, axis=1) * jnp.exp(acs_x)
    y = (y_diag + y_off) + dvec_ref[...] * xs
    y = y * _silu(z_ref[...])
    y_ref[...] = _group_rmsnorm(y, gn_ref[...])

    bpad_sc[0:R, :] = bm
    xdpad_sc[0:R, :] = xd * jnp.exp(acs_x[R - 1:R, :] - acs_x)
    upd = [_dot_tn(bpad_sc[:, g * SSM_STATE:(g + 1) * SSM_STATE], xdpad_sc[:, g * gw:(g + 1) * gw])
           for g in range(SSM_GROUPS)]
    st_new = st * jnp.exp(acs_x[R - 1:R, :]) + jnp.concatenate(upd, axis=1)
    st_ref[...] = st_new.T

    u_sc[...] = up_c_ref[...] * up_v_ref[...]
    sconv = u_sc[pl.ds(8 - (SC_CONV - 1), R), :] * scw_ref[0:1, :]
    for k in range(1, SC_CONV):
        sconv = sconv + u_sc[pl.ds(8 - (SC_CONV - 1) + k, R), :] * scw_ref[k:k + 1, :]
    ysc_ref[...] = scb_ref[...] * sconv
    u_out_ref[...] = u_sc[8:16, :]


def sample_mixer(proj_s, xp_s, up_c, up_v, st_all, layer, consts, sc_conv_w, *, n_new):
    nb = proj_s.shape[0]
    cdim = SSM_INNER + 2 * SSM_GROUPS * SSM_STATE
    k = consts

    def rows(width, col):
        return pl.BlockSpec((None, 8, width), lambda b: (b, 0, col // width))

    return pl.pallas_call(
        functools.partial(_sample_mixer_body, n_new=n_new),
        grid=(nb,),
        in_specs=[rows(SSM_INNER, COL_Z), rows(LANES, COL_DT), rows(SC_WIDTH, COL_SCB),
                  pl.BlockSpec((None, 16, cdim), lambda b: (b, 0, 0)),
                  pl.BlockSpec((None, 16, SC_WIDTH), lambda b: (b, 0, 0)),
                  pl.BlockSpec((None, 16, SC_WIDTH), lambda b: (b, 0, 0)),
                  pl.BlockSpec((None, None, SSM_INNER, SSM_STATE), lambda b: (layer, b, 0, 0)),
                  _full((SSM_CONV, cdim)), _full((1, cdim)), _full((1, LANES)),
                  _full((1, SSM_INNER)), _full((1, SSM_INNER)), _full((1, SSM_INNER)),
                  _full((LANES, SSM_INNER)), _full((SC_CONV, SC_WIDTH))],
        out_specs=[pl.BlockSpec((None, 8, SSM_INNER), lambda b: (b, 0, 0)),
                   pl.BlockSpec((None, 8, SC_WIDTH), lambda b: (b, 0, 0)),
                   pl.BlockSpec((None, SSM_INNER, SSM_STATE), lambda b: (b, 0, 0)),
                   pl.BlockSpec((None, 8, SC_WIDTH), lambda b: (b, 0, 0))],
        out_shape=[jax.ShapeDtypeStruct((nb, 8, SSM_INNER), F32), jax.ShapeDtypeStruct((nb, 8, SC_WIDTH), F32),
                   jax.ShapeDtypeStruct((nb, SSM_INNER, SSM_STATE), F32),
                   jax.ShapeDtypeStruct((nb, 8, SC_WIDTH), F32)],
        scratch_shapes=[pltpu.VMEM((LANES, SSM_GROUPS * SSM_STATE), F32), pltpu.VMEM((LANES, SSM_INNER), F32),
                        pltpu.VMEM((16, SC_WIDTH), F32)],
        compiler_params=_cparams("arbitrary"),
        name="sample_mixer",
    )(proj_s, proj_s, proj_s, xp_s, up_c, up_v, st_all,
      k["cw"], k["cb"], k["dtb"], k["alogx"], k["dvec"], k["gn"], k["e"], sc_conv_w)


def _sconv_prompt_body(scb_ref, scc_ref, scv_ref, pc_ref, pv_ref, w_ref, y_ref, tail_ref, u_sc, *, ts):
    i = pl.program_id(1)
    hist = pc_ref[...] * pv_ref[...]
    u_sc[0:8, :] = jnp.where(i == 0, 0.0, hist)
    u_sc[8:8 + ts, :] = scc_ref[...] * scv_ref[...]
    conv = u_sc[pl.ds(8 - (SC_CONV - 1), ts), :] * w_ref[0:1, :]
    for k in range(1, SC_CONV):
        conv = conv + u_sc[pl.ds(8 - (SC_CONV - 1) + k, ts), :] * w_ref[k:k + 1, :]
    y_ref[...] = (scb_ref[...] * conv).astype(y_ref.dtype)
    tail_ref[...] = u_sc[ts:ts + 8, :]


def sconv_prompt(proj, sc_conv_w, *, batch, seq, ts):
    nt = seq // ts
    w = SC_WIDTH

    def rows(col):
        return pl.BlockSpec((ts, w), lambda b, i: (b * nt + i, col // w))

    def prev(col):
        return pl.BlockSpec((8, w), lambda b, i: (jnp.maximum((b * nt + i) * (ts // 8) - 1, 0), col // w))

    return pl.pallas_call(
        functools.partial(_sconv_prompt_body, ts=ts),
        grid=(batch, nt),
        in_specs=[rows(COL_SCB), rows(COL_SCC), rows(COL_SCV), prev(COL_SCC), prev(COL_SCV), _full((SC_CONV, w))],
        out_specs=[pl.BlockSpec((ts, w), lambda b, i: (b * nt + i, 0)),
                   pl.BlockSpec((None, 8, w), lambda b, i: (b, 0, 0))],
        out_shape=[jax.ShapeDtypeStruct((batch * seq, w), BF16), jax.ShapeDtypeStruct((batch, 8, w), F32)],
        scratch_shapes=[pltpu.VMEM((ts + 8, w), F32)],
        compiler_params=_cparams("parallel", "arbitrary"),
        name="sconv_prompt",
    )(proj, proj, proj, proj, proj, sc_conv_w)


def _merge_body(h_ref, yap_ref, ysp_ref, ycp_ref, yas_ref, yss_ref, ycs_ref,
                wga_ref, wgb_ref, wgc_ref, ba_ref, bb_ref, bc_ref, wa_ref, wb_ref, wc_ref, o_ref,
                wg_sc, wb_sc, *, n_prompt_tiles):
    i = pl.program_id(1)

    @pl.when(i == 0)
    def _():
        for n, r in enumerate((wga_ref, wgb_ref, wgc_ref)):
            wg_sc[n] = r[...].astype(BF16)
        for n, r in enumerate((wa_ref, wb_ref, wc_ref)):
            wb_sc[n] = r[...].astype(BF16)

    h = h_ref[...]
    is_sample = i >= n_prompt_tiles
    out = None
    for n, (p_ref, s_ref, b_ref) in enumerate(((yap_ref, yas_ref, ba_ref), (ysp_ref, yss_ref, bb_ref),
                                               (ycp_ref, ycs_ref, bc_ref))):
        y = jnp.where(is_sample, s_ref[...], p_ref[...])
        term = _sigmoid(_dot(h, wg_sc[n]) + b_ref[...]) * _dot(y, wb_sc[n])
        out = term if out is None else out + term
    o_ref[...] = out.astype(o_ref.dtype)


def gated_merge(h, y_prompt, y_sample, w_gate, b_gate, w_branches, layer, *, tm, tn):
    m, d = h.shape
    mp, kb = y_prompt[0].shape
    nb = d // tn
    npt = mp // tm
    b2 = b_gate.reshape(b_gate.shape[0], 1, -1)
    hspec = pl.BlockSpec((tm, d), lambda j, i: (i, 0))
    pspec = pl.BlockSpec((tm, kb), lambda j, i: (jnp.minimum(i, npt - 1), 0))
    once = pl.Buffered(1)
    sspec = pl.BlockSpec((tm, kb), lambda j, i: (jnp.maximum(i - npt, 0), 0), pipeline_mode=once)

    def wcol(rows_, off):
        return pl.BlockSpec((None, rows_, tn), lambda j, i: (layer, 0, off * nb + j), pipeline_mode=once)

    return pl.pallas_call(
        functools.partial(_merge_body, n_prompt_tiles=npt),
        grid=(nb, m // tm),
        in_specs=[hspec, pspec, pspec, pspec, sspec, sspec, sspec,
                  wcol(d, 0), wcol(d, 1), wcol(d, 2), wcol(1, 0), wcol(1, 1), wcol(1, 2),
                  wcol(kb, 0), wcol(kb, 0), wcol(kb, 0)],
        out_specs=pl.BlockSpec((tm, tn), lambda j, i: (i, j)),
        out_shape=jax.ShapeDtypeStruct((m, d), BF16),
        scratch_shapes=[pltpu.VMEM((3, d, tn), BF16), pltpu.VMEM((3, kb, tn), BF16)],
        compiler_params=_cparams("parallel", "arbitrary"),
        name="gated_merge",
    )(h, *y_prompt, *y_sample, w_gate, w_gate, w_gate, b2, b2, b2, *w_branches)


def _rope_tables(pos):
    half = QK_ROPE // 2
    inv = ROPE_THETA ** (-jnp.arange(half, dtype=F32) / half)
    ang = pos.astype(F32)[:, None] * inv[None, :]
    c, s = jnp.cos(ang), jnp.sin(ang)
    z = jnp.zeros((pos.shape[0], LANES - QK_ROPE), F32)
    return jnp.concatenate([c, c, z], axis=1), jnp.concatenate([-s, s, z], axis=1)


def _reorder_w_in(w):
    sizes = (Q_LORA, KV_LORA, QK_ROPE, SSM_INNER, SSM_INNER + 2 * SSM_GROUPS * SSM_STATE, SSM_HEADS,
             SC_WIDTH, SC_WIDTH, SC_WIDTH)
    q_c, kv_c, k_pe, z, xbc, dt, sc_b, sc_c, sc_v = jnp.split(w, np.cumsum(sizes)[:-1].tolist(), axis=1)
    d = w.shape[0]
    out = jnp.concatenate([sc_b, sc_c, sc_v, z, xbc, q_c, kv_c,
                           k_pe, jnp.zeros((d, LANES - QK_ROPE), w.dtype),
                           dt, jnp.zeros((d, LANES - SSM_HEADS), w.dtype)], axis=1)
    return out.astype(BF16)


def kernel(x_prompt, x_sample, cache_ckv, cache_kpe, state_ssm, state_mconv, state_sconv, page_table, g_attn_norm, w_in, g_q_a, w_q_b, g_kv_a, w_kv_b, ssm_conv_w, ssm_conv_b, ssm_dt_bias, ssm_a_log, ssm_d, g_ssm_norm, sc_conv_w, w_gate, b_gate, w_br_attn, w_br_ssm, w_br_sc, w_o, g_ffn_norm, w_ff_gate, w_ff_up, w_ff_down, w_router, w_e_gate, w_e_up, w_e_down, g_final):
    bp, tp, d = x_prompt.shape
    bs, ts, _ = x_sample.shape
    depth = w_in.shape[0]
    mp, ms = bp * tp, bs * ts
    m = mp + ms
    n_past = page_table.shape[1] * cache_ckv.shape[2]
    tm = m // 8
    tmh = m // 16
    cdim = SSM_INNER + 2 * SSM_GROUPS * SSM_STATE

    x = jnp.concatenate([x_prompt.reshape(mp, d), x_sample.reshape(ms, d)], axis=0)
    pos = jnp.concatenate([jnp.tile(jnp.arange(tp), bp), jnp.tile(n_past + jnp.arange(ts), bs)])
    cos, sin = _rope_tables(pos)
    cache_kpe_t = jnp.swapaxes(cache_kpe, 2, 3)
    st_all = state_ssm.reshape(depth, bs, SSM_INNER, SSM_STATE)

    outs = {k: [] for k in ("p_ckv", "p_kpe", "p_ssm", "p_mconv", "p_sconv", "s_ckv", "s_kpe", "s_ssm", "s_mconv", "s_sconv")}
    for l in range(depth):
        w_in_l = _reorder_w_in(w_in[l])
        wq = w_q_b[l]
        w_q_slots = jnp.concatenate([wq, jnp.zeros(wq.shape[:2] + (Q_SLOT - wq.shape[2],), wq.dtype)], axis=2)
        w_q_slots = w_q_slots.reshape(Q_LORA, MLA_HEADS * Q_SLOT).astype(BF16)
        wkv = w_kv_b[l]
        w_kv_flat = jnp.concatenate([wkv[..., :QK_NOPE].reshape(KV_LORA, -1), wkv[..., QK_NOPE:].reshape(KV_LORA, -1)],
                                    axis=1).astype(BF16)
        w_uk_t = jnp.transpose(wkv[..., :QK_NOPE], (1, 2, 0)).astype(BF16)
        w_uv = jnp.transpose(wkv[..., QK_NOPE:], (1, 0, 2)).astype(BF16)
        consts = _ssd_consts(ssm_conv_w[l], ssm_conv_b[l], ssm_dt_bias[l], ssm_a_log[l], ssm_d[l], g_ssm_norm[l])

        h = rmsnorm(x, g_attn_norm[l], BF16, tm)
        proj = matmul(h, w_in_l, tm=tm, tn=768)
        qn, ckv, kpe = mla_prep(proj, g_q_a[l], g_kv_a[l], cos, sin, tm=tm)
        q = qproj(qn, w_q_slots, cos, sin, tm=tm)

        k_full, v_full = kv_expand(ckv, kpe, w_kv_flat, rows=mp, tm=1024)
        ya_p = flash_attention(q, k_full, v_full, batch=bp, seq=tp, blk=512, heads=2)

        q_lat = blockdiag_matmul(q, w_uk_t, tm=ms, row_block=mp // ms, col_block0=0, col_stride=2)
        q_pe_s = q[mp:].reshape(ms, MLA_HEADS, Q_SLOT)[:, :, QK_NOPE:].reshape(bs, ts * MLA_HEADS, LANES)
        ckv_s = ckv[mp:].reshape(bs, ts, KV_LORA)
        kpe_s = kpe[mp:].reshape(bs, ts, LANES)
        new_c = jnp.pad(ckv_s, ((0, 0), (0, 8 - ts), (0, 0)))
        new_k = jnp.pad(kpe_s, ((0, 0), (0, 8 - ts), (0, 0)))
        o_lat = decode_attention(q_lat.reshape(bs, ts * MLA_HEADS, KV_LORA), q_pe_s, cache_ckv, cache_kpe_t, l,
                                 page_table, new_c, new_k, pages_per_step=32)
        ya_s = blockdiag_matmul(o_lat.reshape(ms, MLA_HEADS * KV_LORA), w_uv, tm=ms, row_block=0, col_block0=0,
                                col_stride=1)

        ys_p, ssm_p = ssd_prompt(proj, consts, batch=bp, seq=tp)
        yc_p, u_tail_p = sconv_prompt(proj, sc_conv_w[l], batch=bp, seq=tp, ts=512)

        proj_s = jnp.pad(proj[mp:].reshape(bs, ts, D_IN_PAD), ((0, 0), (0, 8 - ts), (0, 0)))
        xbc_s = proj_s[:, :ts, COL_X:COL_X + cdim]
        xp_s = jnp.concatenate([jnp.zeros((bs, 8 - (SSM_CONV - 1), cdim), F32), state_mconv[l], xbc_s,
                                jnp.zeros((bs, 8 - ts, cdim), F32)], axis=1)
        zpad = jnp.zeros((bs, 8 - (SC_CONV - 1), SC_WIDTH), F32)
        zend = jnp.zeros((bs, 8 - ts, SC_WIDTH), F32)
        up_c = jnp.concatenate([zpad, state_sconv[l], proj_s[:, :ts, COL_SCC:COL_SCC + SC_WIDTH], zend], axis=1)
        up_v = jnp.concatenate([zpad, jnp.ones_like(state_sconv[l]), proj_s[:, :ts, COL_SCV:COL_SCV + SC_WIDTH], zend], axis=1)
        ys_s, yc_s, ssm_s, u_new_s = sample_mixer(proj_s, xp_s, up_c, up_v, st_all, l, consts, sc_conv_w[l], n_new=ts)

        y_sample = (ya_s, ys_s[:, :ts].reshape(ms, SSM_INNER).astype(BF16), yc_s[:, :ts].reshape(ms, SC_WIDTH).astype(BF16))
        merged = gated_merge(h, (ya_p, ys_p, yc_p), y_sample, w_gate, b_gate, (w_br_attn, w_br_ssm, w_br_sc), l,
                             tm=ms, tn=512)
        x = matmul(merged, w_o, tm=tm, tn=512, res=x, w_index=l)

        outs["p_ckv"].append(ckv[:mp].reshape(bp, tp, KV_LORA))
        outs["p_kpe"].append(kpe[:mp, :QK_ROPE].reshape(bp, tp, QK_ROPE))
        outs["p_ssm"].append(ssm_p.reshape(bp, SSM_HEADS, SSM_HEAD_DIM, SSM_STATE))
        outs["p_mconv"].append(jnp.stack([proj[(b + 1) * tp - (SSM_CONV - 1):(b + 1) * tp, COL_X:COL_X + cdim]
                                          for b in range(bp)]))
        outs["p_sconv"].append(u_tail_p[:, 8 - (SC_CONV - 1):])
        outs["s_ckv"].append(ckv_s)
        outs["s_kpe"].append(kpe_s[:, :, :QK_ROPE])
        outs["s_ssm"].append(ssm_s.reshape(bs, SSM_HEADS, SSM_HEAD_DIM, SSM_STATE))
        outs["s_mconv"].append(xp_s[:, 8 + ts - (SSM_CONV - 1):8 + ts])
        outs["s_sconv"].append(u_new_s[:, ts - (SC_CONV - 1):ts])

        i = l // 2
        if l % 2 == 0:
            h2 = rmsnorm(x, g_ffn_norm[l], BF16, tm)
            hdn = swiglu_up(h2, w_ff_gate[i], w_ff_up[i], tm=tm, tf=512)
            x = matmul(hdn, w_ff_down[i].astype(BF16), tm=tm, tn=256, res=x, rows_outer=True)
        else:
            h2, gate = rmsnorm_router(x, g_ffn_norm[l], w_router[i], tm)
            hdn = moe_up(h2, w_e_gate[i].astype(BF16), w_e_up[i].astype(BF16), gate, tm=tmh)
            wd = w_e_down[i].reshape(-1, d).astype(BF16)
            x = matmul(hdn, wd, tm=tmh, tn=256, res=x, rows_outer=True)

    y_p, y_s = rmsnorm_split(x, g_final, rows_prompt=mp, tm=ms)
    st = {k: jnp.stack(v, axis=0) for k, v in outs.items()}
    return (y_p.reshape(bp, tp, d), y_s.reshape(bs, ts, d),
            st["p_ckv"], st["p_kpe"], st["p_ssm"], st["p_mconv"], st["p_sconv"],
            st["s_ckv"], st["s_kpe"], st["s_ssm"], st["s_mconv"], st["s_sconv"])
```

```python
import functools

import jax
import jax.numpy as jnp
import numpy as np
from jax import lax
from jax.experimental import pallas as pl
from jax.experimental.pallas import tpu as pltpu

F32 = jnp.float32
BF16 = jnp.bfloat16
EPS = 1e-6
ROPE_THETA = 10000.0
LANES = 128
SUBLANES_BF16 = 16
MLA_HEADS = 8
QK_NOPE = 128
QK_ROPE = 64
V_HEAD = 128
KV_LORA = 512
Q_LORA = 512
Q_SLOT = 256
SSM_HEADS = 16
SSM_HEAD_DIM = 64
SSM_INNER = 1024
SSM_GROUPS = 2
SSM_STATE = 128
SSM_CONV = 4
SSM_CHUNK = 128
SC_WIDTH = 1024
SC_CONV = 3
N_EXPERTS = 8
ATTN_SCALE = (QK_NOPE + QK_ROPE) ** -0.5
Q_SCALE = ATTN_SCALE * float(np.log2(np.e))
VMEM_LIMIT = 56 * 1024 * 1024

COL_SCB, COL_SCC, COL_SCV, COL_Z, COL_X, COL_BC, COL_QC, COL_KVC, COL_KPE, COL_DT = (
    0, 1024, 2048, 3072, 4096, 5120, 5632, 6144, 6656, 7168)
PROJ_TILE = 512
D_IN_PAD = 7680


def _cparams(*sem):
    return pltpu.CompilerParams(dimension_semantics=sem, vmem_limit_bytes=VMEM_LIMIT)


def _sigmoid(x):
    return 1.0 / (1.0 + jnp.exp(-x))


def _silu(x):
    return x * _sigmoid(x)


def _softplus(x):
    return jnp.maximum(x, 0.0) + jnp.log1p(jnp.exp(-jnp.abs(x)))


def _dot(a, b, **kw):
    return jnp.dot(a, b, preferred_element_type=F32, **kw)


def _dot_nt(a, b):
    return lax.dot_general(a, b, (((1,), (1,)), ((), ())), preferred_element_type=F32)


def _dot_tn(a, b):
    return lax.dot_general(a, b, (((0,), (0,)), ((), ())), preferred_element_type=F32)


def _rms(x, g):
    r = lax.rsqrt(jnp.mean(x * x, axis=-1, keepdims=True) + EPS)
    return (x * r) * g


def _rmsnorm_body(x_ref, g_ref, o_ref):
    o_ref[...] = _rms(x_ref[...], g_ref[...]).astype(o_ref.dtype)


def rmsnorm(x, g, out_dtype, tm):
    m, d = x.shape
    return pl.pallas_call(
        _rmsnorm_body,
        grid=(m // tm,),
        in_specs=[pl.BlockSpec((tm, d), lambda i: (i, 0)), pl.BlockSpec((1, d), lambda i: (0, 0))],
        out_specs=pl.BlockSpec((tm, d), lambda i: (i, 0)),
        out_shape=jax.ShapeDtypeStruct((m, d), out_dtype),
        compiler_params=_cparams("parallel"),
        name="rmsnorm",
    )(x, g.reshape(1, d))


def _rmsnorm_split_body(x_ref, g_ref, op_ref, os_ref, *, n_prompt_tiles):
    i = pl.program_id(0)
    y = _rms(x_ref[...], g_ref[...])

    @pl.when(i < n_prompt_tiles)
    def _():
        op_ref[...] = y

    @pl.when(i >= n_prompt_tiles)
    def _():
        os_ref[...] = y


def rmsnorm_split(x, g, *, rows_prompt, tm):
    m, d = x.shape
    npt = rows_prompt // tm
    return pl.pallas_call(
        functools.partial(_rmsnorm_split_body, n_prompt_tiles=npt),
        grid=(m // tm,),
        in_specs=[pl.BlockSpec((tm, d), lambda i: (i, 0)), pl.BlockSpec((1, d), lambda i: (0, 0))],
        out_specs=[pl.BlockSpec((tm, d), lambda i: (jnp.minimum(i, npt - 1), 0)),
                   pl.BlockSpec((tm, d), lambda i: (jnp.maximum(i - npt, 0), 0))],
        out_shape=[jax.ShapeDtypeStruct((rows_prompt, d), F32), jax.ShapeDtypeStruct((m - rows_prompt, d), F32)],
        compiler_params=_cparams("arbitrary"),
        name="rmsnorm_split",
    )(x, g.reshape(1, d))


def _rmsnorm_router_body(x_ref, g_ref, wr_ref, h_ref, gate_ref):
    h = _rms(x_ref[...], g_ref[...])
    h_ref[...] = h.astype(h_ref.dtype)
    lg = _dot(h, wr_ref[...], precision=lax.Precision.HIGHEST)
    lane = lax.broadcasted_iota(jnp.int32, lg.shape, 1).astype(F32)
    lg = jnp.where(lane < N_EXPERTS, lg, -jnp.inf)
    m1 = jnp.max(lg, axis=1, keepdims=True)
    i1 = jnp.min(jnp.where(lg == m1, lane, float(LANES)), axis=1, keepdims=True)
    oh1 = lane == i1
    lg2 = jnp.where(oh1, -jnp.inf, lg)
    m2 = jnp.max(lg2, axis=1, keepdims=True)
    i2 = jnp.min(jnp.where(lg2 == m2, lane, float(LANES)), axis=1, keepdims=True)
    oh2 = lane == i2
    e = jnp.exp(m2 - m1)
    w1 = 1.0 / (1.0 + e)
    w2 = e / (1.0 + e)
    gate_ref[...] = jnp.where(oh1, w1, 0.0) + jnp.where(oh2, w2, 0.0)


def rmsnorm_router(x, g, w_router, tm):
    m, d = x.shape
    wr = jnp.pad(w_router, ((0, 0), (0, LANES - w_router.shape[1])))
    return pl.pallas_call(
        _rmsnorm_router_body,
        grid=(m // tm,),
        in_specs=[pl.BlockSpec((tm, d), lambda i: (i, 0)), pl.BlockSpec((1, d), lambda i: (0, 0)),
                  pl.BlockSpec((d, LANES), lambda i: (0, 0))],
        out_specs=[pl.BlockSpec((tm, d), lambda i: (i, 0)), pl.BlockSpec((tm, LANES), lambda i: (i, 0))],
        out_shape=[jax.ShapeDtypeStruct((m, d), BF16), jax.ShapeDtypeStruct((m, LANES), F32)],
        compiler_params=_cparams("parallel"),
        name="rmsnorm_router",
    )(x, g.reshape(1, d), wr)


def _cast_once(src_refs, dst_refs):
    @pl.when(pl.program_id(1) == 0)
    def _():
        for s, d in zip(src_refs, dst_refs):
            d[...] = s[...].astype(d.dtype)


def _mm_body(x_ref, w_ref, *rest, has_res, cast_w):
    rest = list(rest)
    r_ref = rest.pop(0) if has_res else None
    o_ref = rest.pop(0)
    if cast_w:
        (w_sc,) = rest
        _cast_once([w_ref], [w_sc])
        w_ref = w_sc
    acc = _dot(x_ref[...], w_ref[...])
    o_ref[...] = ((r_ref[...] + acc) if has_res else acc).astype(o_ref.dtype)


def matmul(x, w, *, tm, tn, res=None, out_dtype=F32, rows_outer=False, w_index=None):
    m, kd = x.shape
    n = w.shape[-1]
    cast_w = w.dtype != BF16
    assert not (cast_w and rows_outer)

    def ij(a, b):
        return (a, b) if rows_outer else (b, a)

    if w_index is None:
        w_spec = pl.BlockSpec((kd, tn), lambda a, b: (0, ij(a, b)[1]))
    else:
        w_spec = pl.BlockSpec((None, kd, tn), lambda a, b: (w_index, 0, ij(a, b)[1]))
    in_specs = [pl.BlockSpec((tm, kd), lambda a, b: (ij(a, b)[0], 0)), w_spec]
    args = [x, w]
    if res is not None:
        in_specs.append(pl.BlockSpec((tm, tn), lambda a, b: ij(a, b)))
        args.append(res)
    return pl.pallas_call(
        functools.partial(_mm_body, has_res=res is not None, cast_w=cast_w),
        grid=(m // tm, n // tn) if rows_outer else (n // tn, m // tm),
        in_specs=in_specs,
        out_specs=pl.BlockSpec((tm, tn), lambda a, b: ij(a, b)),
        out_shape=jax.ShapeDtypeStruct((m, n), out_dtype),
        scratch_shapes=[pltpu.VMEM((kd, tn), BF16)] if cast_w else [],
        compiler_params=_cparams("parallel", "arbitrary"),
        name="matmul",
    )(*args)


def _in_proj_body(offs_ref, x_ref, wt_ref, o_ref, w_sc):
    del offs_ref
    _cast_once([wt_ref], [w_sc])
    o_ref[...] = _dot_nt(x_ref[...], w_sc[...])


def in_proj(x, w_t, row_offsets, *, tm, tn):
    m, kd = x.shape
    nt = len(row_offsets)
    assert all(o % SUBLANES_BF16 == 0 for o in row_offsets)
    grid_spec = pltpu.PrefetchScalarGridSpec(
        num_scalar_prefetch=1,
        grid=(nt, m // tm),
        in_specs=[pl.BlockSpec((tm, kd), lambda j, i, offs: (i, 0)),
                  pl.BlockSpec((pl.Element(tn), pl.Element(kd)),
                               lambda j, i, offs: (pl.multiple_of(offs[j], SUBLANES_BF16), 0))],
        out_specs=pl.BlockSpec((tm, tn), lambda j, i, offs: (i, j)),
        scratch_shapes=[pltpu.VMEM((tn, kd), BF16)],
    )
    return pl.pallas_call(
        _in_proj_body,
        grid_spec=grid_spec,
        out_shape=jax.ShapeDtypeStruct((m, nt * tn), F32),
        compiler_params=_cparams("parallel", "arbitrary"),
        name="in_proj",
    )(jnp.asarray(row_offsets, jnp.int32), x, w_t)


def _swiglu_body(x_ref, wg_ref, wu_ref, *rest, scaled, cast_w):
    rest = list(rest)
    gate_ref = rest.pop(0) if scaled else None
    o_ref = rest.pop(0)
    if cast_w:
        _cast_once([wg_ref, wu_ref], rest)
        wg_ref, wu_ref = rest
    x = x_ref[...]
    g = _dot(x, wg_ref[...])
    u = _dot(x, wu_ref[...])
    hdn = _silu(g) * u
    if scaled:
        gate = gate_ref[...]
        lane = lax.broadcasted_iota(jnp.int32, gate.shape, 1)
        sc = jnp.sum(jnp.where(lane == pl.program_id(0), gate, 0.0), axis=1, keepdims=True)
        hdn = hdn * sc
    o_ref[...] = hdn.astype(o_ref.dtype)


def swiglu_up(x, wg, wu, *, tm, tf):
    m, d = x.shape
    f = wg.shape[1]
    return pl.pallas_call(
        functools.partial(_swiglu_body, scaled=False, cast_w=True),
        grid=(f // tf, m // tm),
        in_specs=[pl.BlockSpec((tm, d), lambda j, i: (i, 0)), pl.BlockSpec((d, tf), lambda j, i: (0, j)),
                  pl.BlockSpec((d, tf), lambda j, i: (0, j))],
        out_specs=pl.BlockSpec((tm, tf), lambda j, i: (i, j)),
        out_shape=jax.ShapeDtypeStruct((m, f), BF16),
        scratch_shapes=[pltpu.VMEM((d, tf), BF16), pltpu.VMEM((d, tf), BF16)],
        compiler_params=_cparams("parallel", "arbitrary"),
        name="swiglu_up",
    )(x, wg, wu)


def moe_up(x, wg, wu, gate, *, tm):
    m, d = x.shape
    ne, _, f = wg.shape
    return pl.pallas_call(
        functools.partial(_swiglu_body, scaled=True, cast_w=False),
        grid=(ne, m // tm),
        in_specs=[pl.BlockSpec((tm, d), lambda j, i: (i, 0)), pl.BlockSpec((None, d, f), lambda j, i: (j, 0, 0)),
                  pl.BlockSpec((None, d, f), lambda j, i: (j, 0, 0)), pl.BlockSpec((tm, LANES), lambda j, i: (i, 0))],
        out_specs=pl.BlockSpec((tm, f), lambda j, i: (i, j)),
        out_shape=jax.ShapeDtypeStruct((m, ne * f), BF16),
        compiler_params=_cparams("parallel", "parallel"),
        name="moe_up",
    )(x, wg, wu, gate)


def _blockdiag_body(x_ref, w_ref, o_ref):
    o_ref[...] = _dot(x_ref[...], w_ref[...]).astype(o_ref.dtype)


def blockdiag_matmul(x, w, *, tm, row_block, col_block0, col_stride, out_dtype=BF16):
    nh, ki, no = w.shape
    return pl.pallas_call(
        _blockdiag_body,
        grid=(nh,),
        in_specs=[pl.BlockSpec((tm, ki), lambda h: (row_block, col_block0 + h * col_stride)),
                  pl.BlockSpec((None, ki, no), lambda h: (h, 0, 0))],
        out_specs=pl.BlockSpec((tm, no), lambda h: (0, h)),
        out_shape=jax.ShapeDtypeStruct((tm, nh * no), out_dtype),
        compiler_params=_cparams("parallel"),
        name="blockdiag_matmul",
    )(x, w)


def _rope_slab(x, cos, sin):
    half = QK_ROPE // 2
    lane = lax.broadcasted_iota(jnp.int32, x.shape, 1)
    swapped = jnp.where(lane < half, pltpu.roll(x, LANES - half, 1), pltpu.roll(x, half, 1))
    return x * cos + swapped * sin


def _mla_prep_body(qc_ref, kvc_ref, kpe_ref, gq_ref, gkv_ref, cos_ref, sin_ref, qn_ref, ckv_ref, kpe_out_ref):
    qn_ref[...] = _rms(qc_ref[...], gq_ref[...]).astype(qn_ref.dtype)
    ckv_ref[...] = _rms(kvc_ref[...], gkv_ref[...])
    kpe_out_ref[...] = _rope_slab(kpe_ref[...], cos_ref[...], sin_ref[...])


def mla_prep(proj, g_q, g_kv, cos, sin, *, tm):
    m = proj.shape[0]
    return pl.pallas_call(
        _mla_prep_body,
        grid=(m // tm,),
        in_specs=[pl.BlockSpec((tm, Q_LORA), lambda i: (i, COL_QC // Q_LORA)),
                  pl.BlockSpec((tm, KV_LORA), lambda i: (i, COL_KVC // KV_LORA)),
                  pl.BlockSpec((tm, LANES), lambda i: (i, COL_KPE // LANES)),
                  pl.BlockSpec((1, Q_LORA), lambda i: (0, 0)), pl.BlockSpec((1, KV_LORA), lambda i: (0, 0)),
                  pl.BlockSpec((tm, LANES), lambda i: (i, 0)), pl.BlockSpec((tm, LANES), lambda i: (i, 0))],
        out_specs=[pl.BlockSpec((tm, Q_LORA), lambda i: (i, 0)), pl.BlockSpec((tm, KV_LORA), lambda i: (i, 0)),
                   pl.BlockSpec((tm, LANES), lambda i: (i, 0))],
        out_shape=[jax.ShapeDtypeStruct((m, Q_LORA), BF16), jax.ShapeDtypeStruct((m, KV_LORA), F32),
                   jax.ShapeDtypeStruct((m, LANES), F32)],
        compiler_params=_cparams("parallel"),
        name="mla_prep",
    )(proj, proj, proj, g_q.reshape(1, -1), g_kv.reshape(1, -1), cos, sin)


def _qproj_body(x_ref, w_ref, cos_ref, sin_ref, o_ref):
    acc = _dot(x_ref[...], w_ref[...])
    cos = cos_ref[...] * Q_SCALE
    sin = sin_ref[...] * Q_SCALE
    for h in range(MLA_HEADS):
        base = h * Q_SLOT
        o_ref[:, base:base + QK_NOPE] = (acc[:, base:base + QK_NOPE] * Q_SCALE).astype(o_ref.dtype)
        o_ref[:, base + QK_NOPE:base + Q_SLOT] = _rope_slab(acc[:, base + QK_NOPE:base + Q_SLOT], cos, sin).astype(o_ref.dtype)


def qproj(qn, w_q_slots, cos, sin, *, tm):
    m = qn.shape[0]
    n = w_q_slots.shape[1]
    return pl.pallas_call(
        _qproj_body,
        grid=(m // tm,),
        in_specs=[pl.BlockSpec((tm, Q_LORA), lambda i: (i, 0)), pl.BlockSpec((Q_LORA, n), lambda i: (0, 0)),
                  pl.BlockSpec((tm, LANES), lambda i: (i, 0)), pl.BlockSpec((tm, LANES), lambda i: (i, 0))],
        out_specs=pl.BlockSpec((tm, n), lambda i: (i, 0)),
        out_shape=jax.ShapeDtypeStruct((m, n), BF16),
        compiler_params=_cparams("parallel"),
        name="qproj",
    )(qn, w_q_slots, cos, sin)


def _kv_expand_body(ckv_ref, kpe_ref, w_ref, k_ref, v_ref):
    acc = _dot(ckv_ref[...].astype(BF16), w_ref[...])
    kpe = kpe_ref[...].astype(k_ref.dtype)
    for h in range(MLA_HEADS):
        base = h * Q_SLOT
        k_ref[:, base:base + QK_NOPE] = acc[:, h * QK_NOPE:(h + 1) * QK_NOPE].astype(k_ref.dtype)
        k_ref[:, base + QK_NOPE:base + Q_SLOT] = kpe
    v_ref[...] = acc[:, MLA_HEADS * QK_NOPE:].astype(v_ref.dtype)


def kv_expand(ckv, kpe, w_kv_flat, *, rows, tm):
    n = w_kv_flat.shape[1]
    return pl.pallas_call(
        _kv_expand_body,
        grid=(rows // tm,),
        in_specs=[pl.BlockSpec((tm, KV_LORA), lambda i: (i, 0)), pl.BlockSpec((tm, LANES), lambda i: (i, 0)),
                  pl.BlockSpec((KV_LORA, n), lambda i: (0, 0))],
        out_specs=[pl.BlockSpec((tm, MLA_HEADS * Q_SLOT), lambda i: (i, 0)),
                   pl.BlockSpec((tm, MLA_HEADS * V_HEAD), lambda i: (i, 0))],
        out_shape=[jax.ShapeDtypeStruct((rows, MLA_HEADS * Q_SLOT), BF16),
                   jax.ShapeDtypeStruct((rows, MLA_HEADS * V_HEAD), BF16)],
        compiler_params=_cparams("parallel"),
        name="kv_expand",
    )(ckv, kpe, w_kv_flat)


def _flash_body(q_ref, k_ref, v_ref, o_ref, *, blk, heads):
    qi = pl.program_id(2)
    qs = [q_ref[:, h * Q_SLOT:(h + 1) * Q_SLOT] for h in range(heads)]

    def update(j, diagonal, h, m, l, acc):
        start = pl.multiple_of(j * blk, blk)
        k = k_ref[pl.ds(start, blk), h * Q_SLOT:(h + 1) * Q_SLOT]
        v = v_ref[pl.ds(start, blk), h * V_HEAD:(h + 1) * V_HEAD]
        s = _dot_nt(qs[h], k)
        if diagonal:
            row = lax.broadcasted_iota(jnp.int32, s.shape, 0)
            col = lax.broadcasted_iota(jnp.int32, s.shape, 1)
            s = jnp.where(row >= col, s, -jnp.inf)
        m_new = jnp.maximum(m, jnp.max(s, axis=1, keepdims=True))
        alpha = jnp.exp2(m - m_new)
        p = jnp.exp2(s - m_new)
        l = alpha * l + jnp.sum(p, axis=1, keepdims=True)
        acc = alpha * acc + _dot(p.astype(BF16), v)
        return m_new, l, acc

    def step(j, carry, diagonal):
        return tuple(update(j, diagonal, h, *carry[h]) for h in range(heads))

    init = tuple((jnp.full((blk, 1), -jnp.inf, F32), jnp.zeros((blk, 1), F32), jnp.zeros((blk, V_HEAD), F32))
                 for _ in range(heads))
    carry = lax.fori_loop(0, qi, lambda j, c: step(j, c, False), init)
    carry = step(qi, carry, True)
    for h in range(heads):
        _, l, acc = carry[h]
        o_ref[:, h * V_HEAD:(h + 1) * V_HEAD] = (acc / l).astype(o_ref.dtype)


def flash_attention(q, k, v, *, batch, seq, blk, heads):
    nq = seq // blk
    return pl.pallas_call(
        functools.partial(_flash_body, blk=blk, heads=heads),
        grid=(batch, MLA_HEADS // heads, nq),
        in_specs=[pl.BlockSpec((blk, heads * Q_SLOT), lambda b, h, i: (b * nq + i, h)),
                  pl.BlockSpec((seq, heads * Q_SLOT), lambda b, h, i: (b, h)),
                  pl.BlockSpec((seq, heads * V_HEAD), lambda b, h, i: (b, h))],
        out_specs=pl.BlockSpec((blk, heads * V_HEAD), lambda b, h, i: (b * nq + i, h)),
        out_shape=jax.ShapeDtypeStruct((batch * seq, MLA_HEADS * V_HEAD), BF16),
        compiler_params=_cparams("parallel", "parallel", "parallel"),
        name="flash_attention",
    )(q, k, v)


def _decode_body(pt_ref, q_ref, qpe_ref, newc_ref, newk_ref, ckv_hbm, kpe_hbm, o_ref,
                 kbuf, pbuf, sems, m_sc, l_sc, acc_sc, *, layer, pages_per_step, n_steps, n_new):
    pps = pages_per_step
    c = pl.program_id(1)
    n_total = pl.num_programs(0) * n_steps
    g = pl.program_id(0) * n_steps + c
    slot = lax.rem(g, 2)
    nxt = lax.rem(g + 1, n_total)

    def page_copies(chunk, slot_, i):
        pg = pt_ref[chunk * pps + i]
        return (pltpu.make_async_copy(ckv_hbm.at[layer, pg], kbuf.at[slot_, i], sems.at[slot_, 0]),
                pltpu.make_async_copy(kpe_hbm.at[layer, pg], pbuf.at[slot_, i], sems.at[slot_, 1]))

    @pl.when(g == 0)
    def _():
        for i in range(pps):
            for cp in page_copies(0, 0, i):
                cp.start()

    for i in range(pps):
        for cp in page_copies(g, slot, i):
            cp.wait()

    @pl.when(c == 0)
    def _():
        m_sc[...] = jnp.full(m_sc.shape, -jnp.inf, F32)
        l_sc[...] = jnp.zeros(l_sc.shape, F32)
        acc_sc[...] = jnp.zeros(acc_sc.shape, F32)

    q = q_ref[...]
    qp = qpe_ref[:, :QK_ROPE]
    ks, ss = [], []
    for i in range(pps):
        for cp in page_copies(nxt, 1 - slot, i):
            cp.start()
        k = kbuf[slot, i].astype(BF16)
        kp_t = pbuf[slot, i].astype(BF16)
        ks.append(k)
        ss.append(_dot_nt(q, k) + _dot(qp, kp_t))
    s = jnp.concatenate(ss, axis=1)
    m_prev = m_sc[:, :1]
    l_prev = l_sc[:, :1]
    m_new = jnp.maximum(m_prev, jnp.max(s, axis=1, keepdims=True))
    alpha = jnp.exp2(m_prev - m_new)
    p = jnp.exp2(s - m_new)
    l_new = alpha * l_prev + jnp.sum(p, axis=1, keepdims=True)
    page = ks[0].shape[0]
    pv = _dot(p[:, :page].astype(BF16), ks[0])
    for i in range(1, pps):
        pv = pv + _dot(p[:, i * page:(i + 1) * page].astype(BF16), ks[i])
    acc_new = alpha * acc_sc[...] + pv
    m_sc[...] = jnp.broadcast_to(m_new, m_sc.shape)
    l_sc[...] = jnp.broadcast_to(l_new, l_sc.shape)
    acc_sc[...] = acc_new

    @pl.when(c == n_steps - 1)
    def _():
        qf = q.astype(F32)
        qpf = qp.astype(F32)
        kn = newc_ref[...]
        kpn = newk_ref[:, :QK_ROPE]
        row = lax.broadcasted_iota(jnp.int32, (q.shape[0], 1), 0)
        sj = []
        for j in range(n_new):
            v = (jnp.sum(qf * kn[j:j + 1, :], axis=1, keepdims=True)
                 + jnp.sum(qpf * kpn[j:j + 1, :], axis=1, keepdims=True))
            sj.append(jnp.where(row >= j * MLA_HEADS, v, -jnp.inf))
        m_fin = m_new
        for v in sj:
            m_fin = jnp.maximum(m_fin, v)
        a2 = jnp.exp2(m_new - m_fin)
        l_fin = a2 * l_new
        acc_fin = a2 * acc_new
        for j in range(n_new):
            pj = jnp.exp2(sj[j] - m_fin)
            l_fin = l_fin + pj
            acc_fin = acc_fin + pj * kn[j:j + 1, :]
        o_ref[...] = (acc_fin / l_fin).astype(o_ref.dtype)

    @pl.when(g == n_total - 1)
    def _():
        for i in range(pps):
            for cp in page_copies(nxt, 1 - slot, i):
                cp.wait()


def decode_attention(q_lat, q_pe, cache_ckv, cache_kpe, layer, page_table, new_ckv, new_kpe, *, pages_per_step):
    nb, rows, _ = q_lat.shape
    n_pages = page_table.shape[1]
    page = cache_ckv.shape[2]
    pps = pages_per_step
    n_steps = n_pages // pps
    n_new = rows // MLA_HEADS

    assert n_pages == n_steps * pps
    in_specs = [pl.BlockSpec((None, rows, KV_LORA), lambda b, c, pt: (b, 0, 0)),
                pl.BlockSpec((None, rows, LANES), lambda b, c, pt: (b, 0, 0)),
                pl.BlockSpec((None, 8, KV_LORA), lambda b, c, pt: (b, 0, 0)),
                pl.BlockSpec((None, 8, LANES), lambda b, c, pt: (b, 0, 0)),
                pl.BlockSpec(memory_space=pl.ANY), pl.BlockSpec(memory_space=pl.ANY)]
    grid_spec = pltpu.PrefetchScalarGridSpec(
        num_scalar_prefetch=1,
        grid=(nb, n_steps),
        in_specs=in_specs,
        out_specs=pl.BlockSpec((None, rows, KV_LORA), lambda b, c, pt: (b, 0, 0)),
        scratch_shapes=[pltpu.VMEM((2, pps, page, KV_LORA), F32), pltpu.VMEM((2, pps, QK_ROPE, page), F32),
                        pltpu.SemaphoreType.DMA((2, 2)),
                        pltpu.VMEM((rows, LANES), F32), pltpu.VMEM((rows, LANES), F32),
                        pltpu.VMEM((rows, KV_LORA), F32)],
    )
    return pl.pallas_call(
        functools.partial(_decode_body, layer=layer, pages_per_step=pps, n_steps=n_steps, n_new=n_new),
        grid_spec=grid_spec,
        out_shape=jax.ShapeDtypeStruct((nb, rows, KV_LORA), BF16),
        compiler_params=_cparams("arbitrary", "arbitrary"),
        name="decode_attention",
    )(page_table.reshape(-1), q_lat, q_pe, new_ckv, new_kpe, cache_ckv, cache_kpe)


def _group_rmsnorm(y, g):
    gw = SSM_INNER // SSM_GROUPS
    parts = []
    for i in range(SSM_GROUPS):
        yg = y[:, i * gw:(i + 1) * gw]
        parts.append(yg * lax.rsqrt(jnp.mean(yg * yg, axis=-1, keepdims=True) + EPS))
    return jnp.concatenate(parts, axis=1) * g


def _ssd_state_update(st, bm, xd, acs_x):
    last = acs_x.shape[0] - 1
    xde = (xd * jnp.exp(acs_x[last:last + 1, :] - acs_x)).astype(BF16)
    gw = SSM_INNER // SSM_GROUPS
    upd = [_dot_tn(bm[:, g * SSM_STATE:(g + 1) * SSM_STATE].astype(BF16), xde[:, g * gw:(g + 1) * gw])
           for g in range(SSM_GROUPS)]
    return st * jnp.exp(acs_x[last:last + 1, :]) + jnp.concatenate(upd, axis=1)


def _ssd_y_off(st, cm, acs_x):
    gw = SSM_INNER // SSM_GROUPS
    parts = [_dot(cm[:, g * SSM_STATE:(g + 1) * SSM_STATE].astype(BF16), st[:, g * gw:(g + 1) * gw].astype(BF16))
             for g in range(SSM_GROUPS)]
    return jnp.concatenate(parts, axis=1) * jnp.exp(acs_x)


def _ssd_prompt_body(z_ref, x_ref, bc_ref, dt_ref, cw_ref, cb_ref, dtb_ref, alog_ref, alogx_ref, dvec_ref, gn_ref,
                     e_ref, tril_ref, y_ref, st_ref, xp_sc, st_sc, *, n_chunks):
    L = SSM_CHUNK
    c = pl.program_id(1)

    @pl.when(c == 0)
    def _():
        xp_sc[0:8, :] = jnp.zeros((8, xp_sc.shape[1]), F32)
        st_sc[...] = jnp.zeros(st_sc.shape, F32)

    @pl.when(c > 0)
    def _():
        xp_sc[0:8, :] = xp_sc[L:L + 8, :]

    xp_sc[8:8 + L, 0:SSM_INNER] = x_ref[...]
    xp_sc[8:8 + L, SSM_INNER:] = bc_ref[...]
    conv = xp_sc[pl.ds(8 - (SSM_CONV - 1), L), :] * cw_ref[0:1, :]
    for k in range(1, SSM_CONV):
        conv = conv + xp_sc[pl.ds(8 - (SSM_CONV - 1) + k, L), :] * cw_ref[k:k + 1, :]
    xbc = _silu(conv + cb_ref[...])
    xs = xbc[:, :SSM_INNER]
    bm = xbc[:, SSM_INNER:SSM_INNER + SSM_GROUPS * SSM_STATE]
    cm = xbc[:, SSM_INNER + SSM_GROUPS * SSM_STATE:]

    hi = lax.Precision.HIGHEST
    dt = _softplus(dt_ref[...] + dtb_ref[...])
    tril = tril_ref[...]
    acs = _dot(tril, dt * (-jnp.exp(alog_ref[...])), precision=hi)
    acs_t = acs.T
    dt_x = _dot(dt, e_ref[...], precision=hi)
    acs_x = _dot(tril, dt_x * (-jnp.exp(alogx_ref[...])), precision=hi)
    xd = xs * dt_x
    xd_b = xd.astype(BF16)

    row = lax.broadcasted_iota(jnp.int32, (L, L), 0)
    col = lax.broadcasted_iota(jnp.int32, (L, L), 1)
    causal = row >= col
    lane = lax.broadcasted_iota(jnp.int32, (L, LANES), 1)
    heads_per_group = SSM_HEADS // SSM_GROUPS
    y_parts = []
    cb = [_dot_nt(cm[:, g * SSM_STATE:(g + 1) * SSM_STATE].astype(BF16),
                  bm[:, g * SSM_STATE:(g + 1) * SSM_STATE].astype(BF16)) for g in range(SSM_GROUPS)]
    for pair in range(SSM_HEADS // 2):
        xd_pair = xd_b[:, pair * LANES:(pair + 1) * LANES]
        outs = []
        for h in (2 * pair, 2 * pair + 1):
            decay = jnp.exp(jnp.where(causal, acs[:, h:h + 1] - acs_t[h:h + 1, :], -jnp.inf))
            outs.append(_dot((cb[h // heads_per_group] * decay).astype(BF16), xd_pair))
        y_parts.append(jnp.where(lane < SSM_HEAD_DIM, outs[0], outs[1]))
    y_diag = jnp.concatenate(y_parts, axis=1)

    st = st_sc[...]
    y = (y_diag + _ssd_y_off(st, cm, acs_x)) + dvec_ref[...] * xs
    y = y * _silu(z_ref[...])
    y_ref[...] = _group_rmsnorm(y, gn_ref[...]).astype(y_ref.dtype)
    st_new = _ssd_state_update(st, bm, xd, acs_x)
    st_sc[...] = st_new

    @pl.when(c == n_chunks - 1)
    def _():
        st_ref[...] = st_new.T


def _ssd_consts(conv_w, conv_b, dt_bias, a_log, d_vec, g_norm):
    pad = LANES - SSM_HEADS
    e_np = np.zeros((LANES, SSM_INNER), np.float32)
    for hh in range(SSM_HEADS):
        e_np[hh, hh * SSM_HEAD_DIM:(hh + 1) * SSM_HEAD_DIM] = 1.0
    e_mat = jnp.asarray(e_np)
    return dict(
        cw=conv_w, cb=conv_b.reshape(1, -1),
        dtb=jnp.pad(dt_bias, (0, pad)).reshape(1, LANES),
        alog=jnp.pad(a_log, (0, pad)).reshape(1, LANES),
        alogx=jnp.repeat(a_log, SSM_HEAD_DIM).reshape(1, SSM_INNER),
        dvec=jnp.repeat(d_vec, SSM_HEAD_DIM).reshape(1, SSM_INNER),
        gn=g_norm.reshape(1, SSM_INNER), e=e_mat)


def _full(shape):
    nd = len(shape)
    return pl.BlockSpec(shape, lambda *_: (0,) * nd)


def ssd_prompt(proj, consts, *, batch, seq):
    L = SSM_CHUNK
    nc = seq // L
    cdim = SSM_INNER + 2 * SSM_GROUPS * SSM_STATE
    tril = jnp.asarray(np.tril(np.ones((L, L), np.float32)))

    def rows(width, col):
        return pl.BlockSpec((L, width), lambda b, c: (b * nc + c, col // width))

    k = consts
    return pl.pallas_call(
        functools.partial(_ssd_prompt_body, n_chunks=nc),
        grid=(batch, nc),
        in_specs=[rows(SSM_INNER, COL_Z), rows(SSM_INNER, COL_X), rows(2 * SSM_GROUPS * SSM_STATE, COL_BC),
                  rows(LANES, COL_DT),
                  _full((SSM_CONV, cdim)), _full((1, cdim)), _full((1, LANES)), _full((1, LANES)),
                  _full((1, SSM_INNER)), _full((1, SSM_INNER)), _full((1, SSM_INNER)),
                  _full((LANES, SSM_INNER)), _full((L, L))],
        out_specs=[pl.BlockSpec((L, SSM_INNER), lambda b, c: (b * nc + c, 0)),
                   pl.BlockSpec((None, SSM_INNER, SSM_STATE), lambda b, c: (b, 0, 0))],
        out_shape=[jax.ShapeDtypeStruct((batch * seq, SSM_INNER), BF16),
                   jax.ShapeDtypeStruct((batch, SSM_INNER, SSM_STATE), F32)],
        scratch_shapes=[pltpu.VMEM((L + 8, cdim), F32), pltpu.VMEM((SSM_STATE, SSM_INNER), F32)],
        compiler_params=_cparams("parallel", "arbitrary"),
        name="ssd_prompt",
    )(proj, proj, proj, proj, k["cw"], k["cb"], k["dtb"], k["alog"], k["alogx"], k["dvec"], k["gn"], k["e"], tril)


def _sample_mixer_body(z_ref, dt_ref, scb_ref, xp_ref, up_c_ref, up_v_ref, st_in_ref,
                       cw_ref, cb_ref, dtb_ref, alogx_ref, dvec_ref, gn_ref, e_ref, scw_ref,
                       y_ref, ysc_ref, st_ref, u_out_ref, bpad_sc, xdpad_sc, u_sc, *, n_new, group):
    @pl.when(pl.program_id(0) == 0)
    def _():
        bpad_sc[...] = jnp.zeros(bpad_sc.shape, F32)
        xdpad_sc[...] = jnp.zeros(xdpad_sc.shape, F32)

    for i in range(group):
        _sample_mixer_one(z_ref.at[i], dt_ref.at[i], scb_ref.at[i], xp_ref.at[i], up_c_ref.at[i], up_v_ref.at[i],
                          st_in_ref.at[i], cw_ref, cb_ref, dtb_ref, alogx_ref, dvec_ref, gn_ref, e_ref, scw_ref,
                          y_ref.at[i], ysc_ref.at[i], st_ref.at[i], u_out_ref.at[i],
                          bpad_sc.at[i], xdpad_sc.at[i], u_sc.at[i], n_new=n_new)


def _sample_mixer_one(z_ref, dt_ref, scb_ref, xp_ref, up_c_ref, up_v_ref, st_in_ref,
                      cw_ref, cb_ref, dtb_ref, alogx_ref, dvec_ref, gn_ref, e_ref, scw_ref,
                      y_ref, ysc_ref, st_ref, u_out_ref, bpad_sc, xdpad_sc, u_sc, *, n_new):
    R = 8
    conv = xp_ref[pl.ds(8 - (SSM_CONV - 1), R), :] * cw_ref[0:1, :]
    for k in range(1, SSM_CONV):
        conv = conv + xp_ref[pl.ds(8 - (SSM_CONV - 1) + k, R), :] * cw_ref[k:k + 1, :]
    xbc = _silu(conv + cb_ref[...])
    xs = xbc[:, :SSM_INNER]
    bm = xbc[:, SSM_INNER:SSM_INNER + SSM_GROUPS * SSM_STATE]
    cm = xbc[:, SSM_INNER + SSM_GROUPS * SSM_STATE:]

    hi = lax.Precision.HIGHEST
    rowl = lax.broadcasted_iota(jnp.int32, (R, LANES), 0)
    dt = jnp.where(rowl < n_new, _softplus(dt_ref[...] + dtb_ref[...]), 0.0)
    dt_x = _dot(dt, e_ref[...], precision=hi)
    da_x = dt_x * (-jnp.exp(alogx_ref[...]))
    rowx = lax.broadcasted_iota(jnp.int32, (R, SSM_INNER), 0)
    acs_x = jnp.zeros((R, SSM_INNER), F32)
    for s in range(n_new):
        acs_x = acs_x + jnp.where(rowx >= s, da_x[s:s + 1, :], 0.0)
    xd = xs * dt_x

    gw = SSM_INNER // SSM_GROUPS
    lanex = lax.broadcasted_iota(jnp.int32, (R, SSM_INNER), 1)
    y_diag = jnp.zeros((R, SSM_INNER), F32)
    for s in range(n_new):
        cbs = [jnp.sum(cm[:, g * SSM_STATE:(g + 1) * SSM_STATE] * bm[s:s + 1, g * SSM_STATE:(g + 1) * SSM_STATE],
                       axis=1, keepdims=True) for g in range(SSM_GROUPS)]
        cb_x = jnp.where(lanex < gw, cbs[0], cbs[1])
        decay = jnp.exp(jnp.where(rowx >= s, acs_x - acs_x[s:s + 1, :], -jnp.inf))
        y_diag = y_diag + (cb_x * decay) * xd[s:s + 1, :]

    st = st_in_ref[...].T
    y_off_parts = [_dot(cm[:, g * SSM_STATE:(g + 1) * SSM_STATE], st[:, g * gw:(g + 1) * gw])
                   for g in range(SSM_GROUPS)]
    y_off = jnp.concatenate(y_off_parts, axis=1) * jnp.exp(acs_x)
    y = (y_diag + y_off) + dvec_ref[...] * xs
    y = y * _silu(z_ref[...])
    y_ref[...] = _group_rmsnorm(y, gn_ref[...])

    bpad_sc[0:R, :] = bm
    xdpad_sc[0:R, :] = xd * jnp.exp(acs_x[R - 1:R, :] - acs_x)
    upd = [_dot_tn(bpad_sc[:, g * SSM_STATE:(g + 1) * SSM_STATE], xdpad_sc[:, g * gw:(g + 1) * gw])
           for g in range(SSM_GROUPS)]
    st_new = st * jnp.exp(acs_x[R - 1:R, :]) + jnp.concatenate(upd, axis=1)
    st_ref[...] = st_new.T

    u_sc[...] = up_c_ref[...] * up_v_ref[...]
    sconv = u_sc[pl.ds(8 - (SC_CONV - 1), R), :] * scw_ref[0:1, :]
    for k in range(1, SC_CONV):
        sconv = sconv + u_sc[pl.ds(8 - (SC_CONV - 1) + k, R), :] * scw_ref[k:k + 1, :]
    ysc_ref[...] = scb_ref[...] * sconv
    u_out_ref[...] = u_sc[8:16, :]


def sample_mixer(proj_s, xp_s, up_c, up_v, st_all, layer, consts, sc_conv_w, *, n_new, group):
    nb = proj_s.shape[0]
    cdim = SSM_INNER + 2 * SSM_GROUPS * SSM_STATE
    k = consts
    gs = group

    def rows(width, col):
        return pl.BlockSpec((gs, 8, width), lambda b: (b, 0, col // width))

    def per_sample(*tail):
        return pl.BlockSpec((gs,) + tail, lambda b: (b,) + (0,) * len(tail))

    st_spec = pl.BlockSpec((None, gs, SSM_INNER, SSM_STATE), lambda b: (layer, b, 0, 0))
    args = [proj_s, proj_s, proj_s, xp_s, up_c, up_v, st_all,
            k["cw"], k["cb"], k["dtb"], k["alogx"], k["dvec"], k["gn"], k["e"], sc_conv_w]
    in_specs = [rows(SSM_INNER, COL_Z), rows(LANES, COL_DT), rows(SC_WIDTH, COL_SCB),
                per_sample(16, cdim), per_sample(16, SC_WIDTH), per_sample(16, SC_WIDTH), st_spec,
                _full((SSM_CONV, cdim)), _full((1, cdim)), _full((1, LANES)),
                _full((1, SSM_INNER)), _full((1, SSM_INNER)), _full((1, SSM_INNER)),
                _full((LANES, SSM_INNER)), _full((SC_CONV, SC_WIDTH))]
    return pl.pallas_call(
        functools.partial(_sample_mixer_body, n_new=n_new, group=gs),
        grid=(nb // gs,),
        in_specs=in_specs,
        out_specs=[per_sample(8, SSM_INNER), per_sample(8, SC_WIDTH), per_sample(SSM_INNER, SSM_STATE),
                   per_sample(8, SC_WIDTH)],
        out_shape=[jax.ShapeDtypeStruct((nb, 8, SSM_INNER), F32), jax.ShapeDtypeStruct((nb, 8, SC_WIDTH), F32),
                   jax.ShapeDtypeStruct((nb, SSM_INNER, SSM_STATE), F32),
                   jax.ShapeDtypeStruct((nb, 8, SC_WIDTH), F32)],
        scratch_shapes=[pltpu.VMEM((gs, LANES, SSM_GROUPS * SSM_STATE), F32), pltpu.VMEM((gs, LANES, SSM_INNER), F32),
                        pltpu.VMEM((gs, 16, SC_WIDTH), F32)],
        compiler_params=_cparams("arbitrary"),
        name="sample_mixer",
    )(*args)


def _sconv_prompt_body(scb_ref, scc_ref, scv_ref, pc_ref, pv_ref, w_ref, y_ref, tail_ref, u_sc, *, ts):
    i = pl.program_id(1)
    hist = pc_ref[...] * pv_ref[...]
    u_sc[0:8, :] = jnp.where(i == 0, 0.0, hist)
    u_sc[8:8 + ts, :] = scc_ref[...] * scv_ref[...]
    conv = u_sc[pl.ds(8 - (SC_CONV - 1), ts), :] * w_ref[0:1, :]
    for k in range(1, SC_CONV):
        conv = conv + u_sc[pl.ds(8 - (SC_CONV - 1) + k, ts), :] * w_ref[k:k + 1, :]
    y_ref[...] = (scb_ref[...] * conv).astype(y_ref.dtype)
    tail_ref[...] = u_sc[ts:ts + 8, :]


def sconv_prompt(proj, sc_conv_w, *, batch, seq, ts):
    nt = seq // ts
    w = SC_WIDTH

    def rows(col):
        return pl.BlockSpec((ts, w), lambda b, i: (b * nt + i, col // w))

    def prev(col):
        return pl.BlockSpec((8, w), lambda b, i: (jnp.maximum((b * nt + i) * (ts // 8) - 1, 0), col // w))

    return pl.pallas_call(
        functools.partial(_sconv_prompt_body, ts=ts),
        grid=(batch, nt),
        in_specs=[rows(COL_SCB), rows(COL_SCC), rows(COL_SCV), prev(COL_SCC), prev(COL_SCV), _full((SC_CONV, w))],
        out_specs=[pl.BlockSpec((ts, w), lambda b, i: (b * nt + i, 0)),
                   pl.BlockSpec((None, 8, w), lambda b, i: (b, 0, 0))],
        out_shape=[jax.ShapeDtypeStruct((batch * seq, w), BF16), jax.ShapeDtypeStruct((batch, 8, w), F32)],
        scratch_shapes=[pltpu.VMEM((ts + 8, w), F32)],
        compiler_params=_cparams("parallel", "arbitrary"),
        name="sconv_prompt",
    )(proj, proj, proj, proj, proj, sc_conv_w)


def _merge_body(h_ref, yap_ref, ysp_ref, ycp_ref, yas_ref, yss_ref, ycs_ref,
                wga_ref, wgb_ref, wgc_ref, ba_ref, bb_ref, bc_ref, wa_ref, wb_ref, wc_ref, o_ref,
                wg_sc, wb_sc, *, n_prompt_tiles):
    i = pl.program_id(1)

    @pl.when(i == 0)
    def _():
        for n, r in enumerate((wga_ref, wgb_ref, wgc_ref)):
            wg_sc[n] = r[...].astype(BF16)
        for n, r in enumerate((wa_ref, wb_ref, wc_ref)):
            wb_sc[n] = r[...].astype(BF16)

    h = h_ref[...]
    is_sample = i >= n_prompt_tiles
    out = None
    for n, (p_ref, s_ref, b_ref) in enumerate(((yap_ref, yas_ref, ba_ref), (ysp_ref, yss_ref, bb_ref),
                                               (ycp_ref, ycs_ref, bc_ref))):
        y = jnp.where(is_sample, s_ref[...], p_ref[...])
        term = _sigmoid(_dot(h, wg_sc[n]) + b_ref[...]) * _dot(y, wb_sc[n])
        out = term if out is None else out + term
    o_ref[...] = out.astype(o_ref.dtype)


def gated_merge(h, y_prompt, y_sample, w_gate, b_gate, w_branches, layer, *, tm, tn):
    m, d = h.shape
    mp, kb = y_prompt[0].shape
    nb = d // tn
    npt = mp // tm
    b2 = b_gate.reshape(b_gate.shape[0], 1, -1)
    hspec = pl.BlockSpec((tm, d), lambda j, i: (i, 0))
    pspec = pl.BlockSpec((tm, kb), lambda j, i: (jnp.minimum(i, npt - 1), 0))
    once = pl.Buffered(1)
    sspec = pl.BlockSpec((tm, kb), lambda j, i: (jnp.maximum(i - npt, 0), 0), pipeline_mode=once)

    def wcol(rows_, off):
        return pl.BlockSpec((None, rows_, tn), lambda j, i: (layer, 0, off * nb + j), pipeline_mode=once)

    return pl.pallas_call(
        functools.partial(_merge_body, n_prompt_tiles=npt),
        grid=(nb, m // tm),
        in_specs=[hspec, pspec, pspec, pspec, sspec, sspec, sspec,
                  wcol(d, 0), wcol(d, 1), wcol(d, 2), wcol(1, 0), wcol(1, 1), wcol(1, 2),
                  wcol(kb, 0), wcol(kb, 0), wcol(kb, 0)],
        out_specs=pl.BlockSpec((tm, tn), lambda j, i: (i, j)),
        out_shape=jax.ShapeDtypeStruct((m, d), BF16),
        scratch_shapes=[pltpu.VMEM((3, d, tn), BF16), pltpu.VMEM((3, kb, tn), BF16)],
        compiler_params=_cparams("parallel", "arbitrary"),
        name="gated_merge",
    )(h, *y_prompt, *y_sample, w_gate, w_gate, w_gate, b2, b2, b2, *w_branches)


def _rope_tables(pos):
    half = QK_ROPE // 2
    inv = ROPE_THETA ** (-jnp.arange(half, dtype=F32) / half)
    ang = pos.astype(F32)[:, None] * inv[None, :]
    c, s = jnp.cos(ang), jnp.sin(ang)
    z = jnp.zeros((pos.shape[0], LANES - QK_ROPE), F32)
    return jnp.concatenate([c, c, z], axis=1), jnp.concatenate([-s, s, z], axis=1)


def _in_proj_row_offsets(layer):
    sizes = (Q_LORA, KV_LORA, QK_ROPE, SSM_INNER, SSM_INNER + 2 * SSM_GROUPS * SSM_STATE, SSM_HEADS,
             SC_WIDTH, SC_WIDTH, SC_WIDTH)
    q_c, kv_c, k_pe, z, xbc, dt, sc_b, sc_c, sc_v = (int(v) for v in np.cumsum((0,) + sizes)[:-1])
    d_in = int(sum(sizes))
    starts = []
    for first, width in ((sc_b, SC_WIDTH), (sc_c, SC_WIDTH), (sc_v, SC_WIDTH), (z, SSM_INNER),
                         (xbc, SSM_INNER + 2 * SSM_GROUPS * SSM_STATE), (q_c, Q_LORA), (kv_c, KV_LORA),
                         (k_pe, PROJ_TILE), (dt, PROJ_TILE)):
        starts += [first + t for t in range(0, width, PROJ_TILE)]
    assert len(starts) * PROJ_TILE == D_IN_PAD and max(starts) + PROJ_TILE <= d_in
    return tuple(layer * d_in + r for r in starts)


def kernel(x_prompt, x_sample, cache_ckv, cache_kpe, state_ssm, state_mconv, state_sconv, page_table, g_attn_norm, w_in, g_q_a, w_q_b, g_kv_a, w_kv_b, ssm_conv_w, ssm_conv_b, ssm_dt_bias, ssm_a_log, ssm_d, g_ssm_norm, sc_conv_w, w_gate, b_gate, w_br_attn, w_br_ssm, w_br_sc, w_o, g_ffn_norm, w_ff_gate, w_ff_up, w_ff_down, w_router, w_e_gate, w_e_up, w_e_down, g_final):
    bp, tp, d = x_prompt.shape
    bs, ts, _ = x_sample.shape
    depth = w_in.shape[0]
    mp, ms = bp * tp, bs * ts
    m = mp + ms
    n_past = page_table.shape[1] * cache_ckv.shape[2]
    tm = m // 8
    tmh = m // 16
    cdim = SSM_INNER + 2 * SSM_GROUPS * SSM_STATE

    x = jnp.concatenate([x_prompt.reshape(mp, d), x_sample.reshape(ms, d)], axis=0)
    pos = jnp.concatenate([jnp.tile(jnp.arange(tp), bp), jnp.tile(n_past + jnp.arange(ts), bs)])
    cos, sin = _rope_tables(pos)
    cache_kpe_t = jnp.swapaxes(cache_kpe, 2, 3)
    st_all = state_ssm.reshape(depth, bs, SSM_INNER, SSM_STATE)
    w_in_t = jnp.swapaxes(w_in, 1, 2).reshape(-1, d)

    outs = {k: [] for k in ("p_ckv", "p_kpe", "p_ssm", "p_mconv", "p_sconv", "s_ckv", "s_kpe", "s_ssm", "s_mconv", "s_sconv")}
    for l in range(depth):
        wq = w_q_b[l]
        w_q_slots = jnp.concatenate([wq, jnp.zeros(wq.shape[:2] + (Q_SLOT - wq.shape[2],), wq.dtype)], axis=2)
        w_q_slots = w_q_slots.reshape(Q_LORA, MLA_HEADS * Q_SLOT).astype(BF16)
        wkv = w_kv_b[l]
        w_kv_flat = jnp.concatenate([wkv[..., :QK_NOPE].reshape(KV_LORA, -1), wkv[..., QK_NOPE:].reshape(KV_LORA, -1)],
                                    axis=1).astype(BF16)
        w_uk_t = jnp.transpose(wkv[..., :QK_NOPE], (1, 2, 0)).astype(BF16)
        w_uv = jnp.transpose(wkv[..., QK_NOPE:], (1, 0, 2)).astype(BF16)
        consts = _ssd_consts(ssm_conv_w[l], ssm_conv_b[l], ssm_dt_bias[l], ssm_a_log[l], ssm_d[l], g_ssm_norm[l])

        h = rmsnorm(x, g_attn_norm[l], BF16, tm)
        proj = in_proj(h, w_in_t, _in_proj_row_offsets(l), tm=tm, tn=PROJ_TILE)
        qn, ckv, kpe = mla_prep(proj, g_q_a[l], g_kv_a[l], cos, sin, tm=tm)
        q = qproj(qn, w_q_slots, cos, sin, tm=tm)

        k_full, v_full = kv_expand(ckv, kpe, w_kv_flat, rows=mp, tm=1024)
        ya_p = flash_attention(q, k_full, v_full, batch=bp, seq=tp, blk=1024, heads=2)

        q_lat = blockdiag_matmul(q, w_uk_t, tm=ms, row_block=mp // ms, col_block0=0, col_stride=2)
        q_pe_s = q[mp:].reshape(ms, MLA_HEADS, Q_SLOT)[:, :, QK_NOPE:].reshape(bs, ts * MLA_HEADS, LANES)
        ckv_s = ckv[mp:].reshape(bs, ts, KV_LORA)
        kpe_s = kpe[mp:].reshape(bs, ts, LANES)
        new_c = jnp.pad(ckv_s, ((0, 0), (0, 8 - ts), (0, 0)))
        new_k = jnp.pad(kpe_s, ((0, 0), (0, 8 - ts), (0, 0)))
        o_lat = decode_attention(q_lat.reshape(bs, ts * MLA_HEADS, KV_LORA), q_pe_s, cache_ckv, cache_kpe_t, l,
                                 page_table, new_c, new_k, pages_per_step=32)
        ya_s = blockdiag_matmul(o_lat.reshape(ms, MLA_HEADS * KV_LORA), w_uv, tm=ms, row_block=0, col_block0=0,
                                col_stride=1)

        ys_p, ssm_p = ssd_prompt(proj, consts, batch=bp, seq=tp)
        yc_p, u_tail_p = sconv_prompt(proj, sc_conv_w[l], batch=bp, seq=tp, ts=512)

        proj_s = jnp.pad(proj[mp:].reshape(bs, ts, D_IN_PAD), ((0, 0), (0, 8 - ts), (0, 0)))
        xbc_s = proj_s[:, :ts, COL_X:COL_X + cdim]
        xp_s = jnp.concatenate([jnp.zeros((bs, 8 - (SSM_CONV - 1), cdim), F32), state_mconv[l], xbc_s,
                                jnp.zeros((bs, 8 - ts, cdim), F32)], axis=1)
        zpad = jnp.zeros((bs, 8 - (SC_CONV - 1), SC_WIDTH), F32)
        zend = jnp.zeros((bs, 8 - ts, SC_WIDTH), F32)
        up_c = jnp.concatenate([zpad, state_sconv[l], proj_s[:, :ts, COL_SCC:COL_SCC + SC_WIDTH], zend], axis=1)
        up_v = jnp.concatenate([zpad, jnp.ones_like(state_sconv[l]), proj_s[:, :ts, COL_SCV:COL_SCV + SC_WIDTH], zend], axis=1)
        ys_s, yc_s, ssm_s, u_new_s = sample_mixer(proj_s, xp_s, up_c, up_v, st_all, l, consts, sc_conv_w[l],
                                                  n_new=ts, group=4)

        y_sample = (ya_s, ys_s[:, :ts].reshape(ms, SSM_INNER).astype(BF16), yc_s[:, :ts].reshape(ms, SC_WIDTH).astype(BF16))
        merged = gated_merge(h, (ya_p, ys_p, yc_p), y_sample, w_gate, b_gate, (w_br_attn, w_br_ssm, w_br_sc), l,
                             tm=ms, tn=512)
        x = matmul(merged, w_o, tm=tm, tn=512, res=x, w_index=l)

        outs["p_ckv"].append(ckv[:mp].reshape(bp, tp, KV_LORA))
        outs["p_kpe"].append(kpe[:mp, :QK_ROPE].reshape(bp, tp, QK_ROPE))
        outs["p_ssm"].append(ssm_p.reshape(bp, SSM_HEADS, SSM_HEAD_DIM, SSM_STATE))
        outs["p_mconv"].append(jnp.stack([proj[(b + 1) * tp - (SSM_CONV - 1):(b + 1) * tp, COL_X:COL_X + cdim]
                                          for b in range(bp)]))
        outs["p_sconv"].append(u_tail_p[:, 8 - (SC_CONV - 1):])
        outs["s_ckv"].append(ckv_s)
        outs["s_kpe"].append(kpe_s[:, :, :QK_ROPE])
        outs["s_ssm"].append(ssm_s.reshape(bs, SSM_HEADS, SSM_HEAD_DIM, SSM_STATE))
        outs["s_mconv"].append(xp_s[:, 8 + ts - (SSM_CONV - 1):8 + ts])
        outs["s_sconv"].append(u_new_s[:, ts - (SC_CONV - 1):ts])

        i = l // 2
        if l % 2 == 0:
            h2 = rmsnorm(x, g_ffn_norm[l], BF16, tm)
            hdn = swiglu_up(h2, w_ff_gate[i], w_ff_up[i], tm=tm, tf=512)
            x = matmul(hdn, w_ff_down[i].astype(BF16), tm=tm, tn=256, res=x, rows_outer=True)
        else:
            h2, gate = rmsnorm_router(x, g_ffn_norm[l], w_router[i], tm)
            hdn = moe_up(h2, w_e_gate[i].astype(BF16), w_e_up[i].astype(BF16), gate, tm=tmh)
            wd = w_e_down[i].reshape(-1, d).astype(BF16)
            x = matmul(hdn, wd, tm=tmh, tn=256, res=x, rows_outer=True)

    y_p, y_s = rmsnorm_split(x, g_final, rows_prompt=mp, tm=ms)
    st = {k: jnp.stack(v, axis=0) for k, v in outs.items()}
    return (y_p.reshape(bp, tp, d), y_s.reshape(bs, ts, d),
            st["p_ckv"], st["p_kpe"], st["p_ssm"], st["p_mconv"], st["p_sconv"],
            st["s_ckv"], st["s_kpe"], st["s_ssm"], st["s_mconv"], st["s_sconv"])
```

```python
import functools

import jax
import jax.numpy as jnp
import numpy as np
from jax import lax
from jax.experimental import pallas as pl
from jax.experimental.pallas import tpu as pltpu

F32 = jnp.float32
BF16 = jnp.bfloat16
EPS = 1e-6
ROPE_THETA = 10000.0
LANES = 128
SUBLANES_BF16 = 16
MLA_HEADS = 8
QK_NOPE = 128
QK_ROPE = 64
V_HEAD = 128
KV_LORA = 512
Q_LORA = 512
Q_SLOT = 256
SSM_HEADS = 16
SSM_HEAD_DIM = 64
SSM_INNER = 1024
SSM_GROUPS = 2
SSM_STATE = 128
SSM_CONV = 4
SSM_CHUNK = 128
SC_WIDTH = 1024
SC_CONV = 3
N_EXPERTS = 8
ATTN_SCALE = (QK_NOPE + QK_ROPE) ** -0.5
Q_SCALE = ATTN_SCALE * float(np.log2(np.e))
VMEM_LIMIT = 56 * 1024 * 1024

COL_SCB, COL_SCC, COL_SCV, COL_Z, COL_X, COL_BC, COL_QC, COL_KVC, COL_KPE, COL_DT = (
    0, 1024, 2048, 3072, 4096, 5120, 5632, 6144, 6656, 7168)
PROJ_TILE = 512
D_IN_PAD = 7680


def _cparams(*sem):
    return pltpu.CompilerParams(dimension_semantics=sem, vmem_limit_bytes=VMEM_LIMIT)


def _sigmoid(x):
    return 1.0 / (1.0 + jnp.exp(-x))


def _silu(x):
    return x * _sigmoid(x)


def _softplus(x):
    return jnp.maximum(x, 0.0) + jnp.log1p(jnp.exp(-jnp.abs(x)))


def _dot(a, b, **kw):
    return jnp.dot(a, b, preferred_element_type=F32, **kw)


def _dot_nt(a, b):
    return lax.dot_general(a, b, (((1,), (1,)), ((), ())), preferred_element_type=F32)


def _dot_tn(a, b):
    return lax.dot_general(a, b, (((0,), (0,)), ((), ())), preferred_element_type=F32)


def _rms(x, g):
    r = lax.rsqrt(jnp.mean(x * x, axis=-1, keepdims=True) + EPS)
    return (x * r) * g


def _rmsnorm_body(x_ref, g_ref, o_ref):
    o_ref[...] = _rms(x_ref[...], g_ref[...]).astype(o_ref.dtype)


def rmsnorm(x, g, out_dtype, tm):
    m, d = x.shape
    return pl.pallas_call(
        _rmsnorm_body,
        grid=(m // tm,),
        in_specs=[pl.BlockSpec((tm, d), lambda i: (i, 0)), pl.BlockSpec((1, d), lambda i: (0, 0))],
        out_specs=pl.BlockSpec((tm, d), lambda i: (i, 0)),
        out_shape=jax.ShapeDtypeStruct((m, d), out_dtype),
        compiler_params=_cparams("parallel"),
        name="rmsnorm",
    )(x, g.reshape(1, d))


def _rmsnorm_split_body(x_ref, g_ref, op_ref, os_ref, *, n_prompt_tiles):
    i = pl.program_id(0)
    y = _rms(x_ref[...], g_ref[...])

    @pl.when(i < n_prompt_tiles)
    def _():
        op_ref[...] = y

    @pl.when(i >= n_prompt_tiles)
    def _():
        os_ref[...] = y


def rmsnorm_split(x, g, *, rows_prompt, tm):
    m, d = x.shape
    npt = rows_prompt // tm
    return pl.pallas_call(
        functools.partial(_rmsnorm_split_body, n_prompt_tiles=npt),
        grid=(m // tm,),
        in_specs=[pl.BlockSpec((tm, d), lambda i: (i, 0)), pl.BlockSpec((1, d), lambda i: (0, 0))],
        out_specs=[pl.BlockSpec((tm, d), lambda i: (jnp.minimum(i, npt - 1), 0)),
                   pl.BlockSpec((tm, d), lambda i: (jnp.maximum(i - npt, 0), 0))],
        out_shape=[jax.ShapeDtypeStruct((rows_prompt, d), F32), jax.ShapeDtypeStruct((m - rows_prompt, d), F32)],
        compiler_params=_cparams("arbitrary"),
        name="rmsnorm_split",
    )(x, g.reshape(1, d))


def _rmsnorm_router_body(x_ref, g_ref, wr_ref, h_ref, gate_ref):
    h = _rms(x_ref[...], g_ref[...])
    h_ref[...] = h.astype(h_ref.dtype)
    lg = _dot(h, wr_ref[...], precision=lax.Precision.HIGHEST)
    lane = lax.broadcasted_iota(jnp.int32, lg.shape, 1).astype(F32)
    lg = jnp.where(lane < N_EXPERTS, lg, -jnp.inf)
    m1 = jnp.max(lg, axis=1, keepdims=True)
    i1 = jnp.min(jnp.where(lg == m1, lane, float(LANES)), axis=1, keepdims=True)
    oh1 = lane == i1
    lg2 = jnp.where(oh1, -jnp.inf, lg)
    m2 = jnp.max(lg2, axis=1, keepdims=True)
    i2 = jnp.min(jnp.where(lg2 == m2, lane, float(LANES)), axis=1, keepdims=True)
    oh2 = lane == i2
    e = jnp.exp(m2 - m1)
    w1 = 1.0 / (1.0 + e)
    w2 = e / (1.0 + e)
    gate_ref[...] = jnp.where(oh1, w1, 0.0) + jnp.where(oh2, w2, 0.0)


def rmsnorm_router(x, g, w_router, tm):
    m, d = x.shape
    wr = jnp.pad(w_router, ((0, 0), (0, LANES - w_router.shape[1])))
    return pl.pallas_call(
        _rmsnorm_router_body,
        grid=(m // tm,),
        in_specs=[pl.BlockSpec((tm, d), lambda i: (i, 0)), pl.BlockSpec((1, d), lambda i: (0, 0)),
                  pl.BlockSpec((d, LANES), lambda i: (0, 0))],
        out_specs=[pl.BlockSpec((tm, d), lambda i: (i, 0)), pl.BlockSpec((tm, LANES), lambda i: (i, 0))],
        out_shape=[jax.ShapeDtypeStruct((m, d), BF16), jax.ShapeDtypeStruct((m, LANES), F32)],
        compiler_params=_cparams("parallel"),
        name="rmsnorm_router",
    )(x, g.reshape(1, d), wr)


def _cast_once(src_refs, dst_refs):
    @pl.when(pl.program_id(1) == 0)
    def _():
        for s, d in zip(src_refs, dst_refs):
            d[...] = s[...].astype(d.dtype)


def _mm_body(x_ref, w_ref, *rest, has_res, cast_w):
    rest = list(rest)
    r_ref = rest.pop(0) if has_res else None
    o_ref = rest.pop(0)
    if cast_w:
        (w_sc,) = rest
        _cast_once([w_ref], [w_sc])
        w_ref = w_sc
    acc = _dot(x_ref[...], w_ref[...])
    o_ref[...] = ((r_ref[...] + acc) if has_res else acc).astype(o_ref.dtype)


def matmul(x, w, *, tm, tn, res=None, out_dtype=F32, rows_outer=False, w_index=None):
    m, kd = x.shape
    n = w.shape[-1]
    cast_w = w.dtype != BF16
    assert not (cast_w and rows_outer)

    def ij(a, b):
        return (a, b) if rows_outer else (b, a)

    if w_index is None:
        w_spec = pl.BlockSpec((kd, tn), lambda a, b: (0, ij(a, b)[1]))
    else:
        w_spec = pl.BlockSpec((None, kd, tn), lambda a, b: (w_index, 0, ij(a, b)[1]))
    in_specs = [pl.BlockSpec((tm, kd), lambda a, b: (ij(a, b)[0], 0)), w_spec]
    args = [x, w]
    if res is not None:
        in_specs.append(pl.BlockSpec((tm, tn), lambda a, b: ij(a, b)))
        args.append(res)
    return pl.pallas_call(
        functools.partial(_mm_body, has_res=res is not None, cast_w=cast_w),
        grid=(m // tm, n // tn) if rows_outer else (n // tn, m // tm),
        in_specs=in_specs,
        out_specs=pl.BlockSpec((tm, tn), lambda a, b: ij(a, b)),
        out_shape=jax.ShapeDtypeStruct((m, n), out_dtype),
        scratch_shapes=[pltpu.VMEM((kd, tn), BF16)] if cast_w else [],
        compiler_params=_cparams("parallel", "arbitrary"),
        name="matmul",
    )(*args)


def _in_proj_body(offs_ref, x_ref, wt_ref, o_ref, w_sc):
    del offs_ref

    @pl.when(pl.program_id(1) == 0)
    def _():
        w_sc[...] = wt_ref[...].T.astype(w_sc.dtype)

    o_ref[...] = _dot(x_ref[...], w_sc[...])


def in_proj(x, w_t, row_offsets, *, tm, tn):
    m, kd = x.shape
    nt = len(row_offsets)
    assert all(o % SUBLANES_BF16 == 0 for o in row_offsets)
    grid_spec = pltpu.PrefetchScalarGridSpec(
        num_scalar_prefetch=1,
        grid=(nt, m // tm),
        in_specs=[pl.BlockSpec((tm, kd), lambda j, i, offs: (i, 0)),
                  pl.BlockSpec((pl.Element(tn), pl.Element(kd)),
                               lambda j, i, offs: (pl.multiple_of(offs[j], SUBLANES_BF16), 0))],
        out_specs=pl.BlockSpec((tm, tn), lambda j, i, offs: (i, j)),
        scratch_shapes=[pltpu.VMEM((kd, tn), BF16)],
    )
    return pl.pallas_call(
        _in_proj_body,
        grid_spec=grid_spec,
        out_shape=jax.ShapeDtypeStruct((m, nt * tn), F32),
        compiler_params=_cparams("parallel", "arbitrary"),
        name="in_proj",
    )(jnp.asarray(row_offsets, jnp.int32), x, w_t)


def _swiglu_body(x_ref, wg_ref, wu_ref, *rest, scaled, cast_w):
    rest = list(rest)
    gate_ref = rest.pop(0) if scaled else None
    o_ref = rest.pop(0)
    if cast_w:
        _cast_once([wg_ref, wu_ref], rest)
        wg_ref, wu_ref = rest
    x = x_ref[...]
    g = _dot(x, wg_ref[...])
    u = _dot(x, wu_ref[...])
    hdn = _silu(g) * u
    if scaled:
        gate = gate_ref[...]
        lane = lax.broadcasted_iota(jnp.int32, gate.shape, 1)
        sc = jnp.sum(jnp.where(lane == pl.program_id(0), gate, 0.0), axis=1, keepdims=True)
        hdn = hdn * sc
    o_ref[...] = hdn.astype(o_ref.dtype)


def swiglu_up(x, wg, wu, *, tm, tf):
    m, d = x.shape
    f = wg.shape[1]
    return pl.pallas_call(
        functools.partial(_swiglu_body, scaled=False, cast_w=True),
        grid=(f // tf, m // tm),
        in_specs=[pl.BlockSpec((tm, d), lambda j, i: (i, 0)), pl.BlockSpec((d, tf), lambda j, i: (0, j)),
                  pl.BlockSpec((d, tf), lambda j, i: (0, j))],
        out_specs=pl.BlockSpec((tm, tf), lambda j, i: (i, j)),
        out_shape=jax.ShapeDtypeStruct((m, f), BF16),
        scratch_shapes=[pltpu.VMEM((d, tf), BF16), pltpu.VMEM((d, tf), BF16)],
        compiler_params=_cparams("parallel", "arbitrary"),
        name="swiglu_up",
    )(x, wg, wu)


def moe_up(x, wg, wu, gate, *, tm):
    m, d = x.shape
    ne, _, f = wg.shape
    return pl.pallas_call(
        functools.partial(_swiglu_body, scaled=True, cast_w=False),
        grid=(ne, m // tm),
        in_specs=[pl.BlockSpec((tm, d), lambda j, i: (i, 0)), pl.BlockSpec((None, d, f), lambda j, i: (j, 0, 0)),
                  pl.BlockSpec((None, d, f), lambda j, i: (j, 0, 0)), pl.BlockSpec((tm, LANES), lambda j, i: (i, 0))],
        out_specs=pl.BlockSpec((tm, f), lambda j, i: (i, j)),
        out_shape=jax.ShapeDtypeStruct((m, ne * f), BF16),
        compiler_params=_cparams("parallel", "parallel"),
        name="moe_up",
    )(x, wg, wu, gate)


def _blockdiag_body(x_ref, w_ref, o_ref):
    o_ref[...] = _dot(x_ref[...], w_ref[...]).astype(o_ref.dtype)


def blockdiag_matmul(x, w, *, tm, row_block, col_block0, col_stride, out_dtype=BF16):
    nh, ki, no = w.shape
    return pl.pallas_call(
        _blockdiag_body,
        grid=(nh,),
        in_specs=[pl.BlockSpec((tm, ki), lambda h: (row_block, col_block0 + h * col_stride)),
                  pl.BlockSpec((None, ki, no), lambda h: (h, 0, 0))],
        out_specs=pl.BlockSpec((tm, no), lambda h: (0, h)),
        out_shape=jax.ShapeDtypeStruct((tm, nh * no), out_dtype),
        compiler_params=_cparams("parallel"),
        name="blockdiag_matmul",
    )(x, w)


def _rope_slab(x, cos, sin):
    half = QK_ROPE // 2
    lane = lax.broadcasted_iota(jnp.int32, x.shape, 1)
    swapped = jnp.where(lane < half, pltpu.roll(x, LANES - half, 1), pltpu.roll(x, half, 1))
    return x * cos + swapped * sin


def _mla_prep_body(qc_ref, kvc_ref, kpe_ref, gq_ref, gkv_ref, cos_ref, sin_ref, qn_ref, ckv_ref, kpe_out_ref):
    qn_ref[...] = _rms(qc_ref[...], gq_ref[...]).astype(qn_ref.dtype)
    ckv_ref[...] = _rms(kvc_ref[...], gkv_ref[...])
    kpe_out_ref[...] = _rope_slab(kpe_ref[...], cos_ref[...], sin_ref[...])


def mla_prep(proj, g_q, g_kv, cos, sin, *, tm):
    m = proj.shape[0]
    return pl.pallas_call(
        _mla_prep_body,
        grid=(m // tm,),
        in_specs=[pl.BlockSpec((tm, Q_LORA), lambda i: (i, COL_QC // Q_LORA)),
                  pl.BlockSpec((tm, KV_LORA), lambda i: (i, COL_KVC // KV_LORA)),
                  pl.BlockSpec((tm, LANES), lambda i: (i, COL_KPE // LANES)),
                  pl.BlockSpec((1, Q_LORA), lambda i: (0, 0)), pl.BlockSpec((1, KV_LORA), lambda i: (0, 0)),
                  pl.BlockSpec((tm, LANES), lambda i: (i, 0)), pl.BlockSpec((tm, LANES), lambda i: (i, 0))],
        out_specs=[pl.BlockSpec((tm, Q_LORA), lambda i: (i, 0)), pl.BlockSpec((tm, KV_LORA), lambda i: (i, 0)),
                   pl.BlockSpec((tm, LANES), lambda i: (i, 0))],
        out_shape=[jax.ShapeDtypeStruct((m, Q_LORA), BF16), jax.ShapeDtypeStruct((m, KV_LORA), F32),
                   jax.ShapeDtypeStruct((m, LANES), F32)],
        compiler_params=_cparams("parallel"),
        name="mla_prep",
    )(proj, proj, proj, g_q.reshape(1, -1), g_kv.reshape(1, -1), cos, sin)


def _qproj_body(x_ref, w_ref, cos_ref, sin_ref, o_ref):
    acc = _dot(x_ref[...], w_ref[...])
    cos = cos_ref[...] * Q_SCALE
    sin = sin_ref[...] * Q_SCALE
    for h in range(MLA_HEADS):
        base = h * Q_SLOT
        o_ref[:, base:base + QK_NOPE] = (acc[:, base:base + QK_NOPE] * Q_SCALE).astype(o_ref.dtype)
        o_ref[:, base + QK_NOPE:base + Q_SLOT] = _rope_slab(acc[:, base + QK_NOPE:base + Q_SLOT], cos, sin).astype(o_ref.dtype)


def qproj(qn, w_q_slots, cos, sin, *, tm):
    m = qn.shape[0]
    n = w_q_slots.shape[1]
    return pl.pallas_call(
        _qproj_body,
        grid=(m // tm,),
        in_specs=[pl.BlockSpec((tm, Q_LORA), lambda i: (i, 0)), pl.BlockSpec((Q_LORA, n), lambda i: (0, 0)),
                  pl.BlockSpec((tm, LANES), lambda i: (i, 0)), pl.BlockSpec((tm, LANES), lambda i: (i, 0))],
        out_specs=pl.BlockSpec((tm, n), lambda i: (i, 0)),
        out_shape=jax.ShapeDtypeStruct((m, n), BF16),
        compiler_params=_cparams("parallel"),
        name="qproj",
    )(qn, w_q_slots, cos, sin)


def _kv_expand_body(ckv_ref, kpe_ref, w_ref, k_ref, v_ref):
    acc = _dot(ckv_ref[...].astype(BF16), w_ref[...])
    kpe = kpe_ref[...].astype(k_ref.dtype)
    for h in range(MLA_HEADS):
        base = h * Q_SLOT
        k_ref[:, base:base + QK_NOPE] = acc[:, h * QK_NOPE:(h + 1) * QK_NOPE].astype(k_ref.dtype)
        k_ref[:, base + QK_NOPE:base + Q_SLOT] = kpe
    v_ref[...] = acc[:, MLA_HEADS * QK_NOPE:].astype(v_ref.dtype)


def kv_expand(ckv, kpe, w_kv_flat, *, rows, tm):
    n = w_kv_flat.shape[1]
    return pl.pallas_call(
        _kv_expand_body,
        grid=(rows // tm,),
        in_specs=[pl.BlockSpec((tm, KV_LORA), lambda i: (i, 0)), pl.BlockSpec((tm, LANES), lambda i: (i, 0)),
                  pl.BlockSpec((KV_LORA, n), lambda i: (0, 0))],
        out_specs=[pl.BlockSpec((tm, MLA_HEADS * Q_SLOT), lambda i: (i, 0)),
                   pl.BlockSpec((tm, MLA_HEADS * V_HEAD), lambda i: (i, 0))],
        out_shape=[jax.ShapeDtypeStruct((rows, MLA_HEADS * Q_SLOT), BF16),
                   jax.ShapeDtypeStruct((rows, MLA_HEADS * V_HEAD), BF16)],
        compiler_params=_cparams("parallel"),
        name="kv_expand",
    )(ckv, kpe, w_kv_flat)


def _flash_body(q_ref, k_ref, v_ref, o_ref, *, blk, heads):
    qi = pl.program_id(2)
    qs = [q_ref[:, h * Q_SLOT:(h + 1) * Q_SLOT] for h in range(heads)]

    def update(j, diagonal, h, m, l, acc):
        start = pl.multiple_of(j * blk, blk)
        k = k_ref[pl.ds(start, blk), h * Q_SLOT:(h + 1) * Q_SLOT]
        v = v_ref[pl.ds(start, blk), h * V_HEAD:(h + 1) * V_HEAD]
        s = _dot_nt(qs[h], k)
        if diagonal:
            row = lax.broadcasted_iota(jnp.int32, s.shape, 0)
            col = lax.broadcasted_iota(jnp.int32, s.shape, 1)
            s = jnp.where(row >= col, s, -jnp.inf)
        m_new = jnp.maximum(m, jnp.max(s, axis=1, keepdims=True))
        alpha = jnp.exp2(m - m_new)
        p = jnp.exp2(s - m_new)
        l = alpha * l + jnp.sum(p, axis=1, keepdims=True)
        acc = alpha * acc + _dot(p.astype(BF16), v)
        return m_new, l, acc

    def step(j, carry, diagonal):
        return tuple(update(j, diagonal, h, *carry[h]) for h in range(heads))

    init = tuple((jnp.full((blk, 1), -jnp.inf, F32), jnp.zeros((blk, 1), F32), jnp.zeros((blk, V_HEAD), F32))
                 for _ in range(heads))
    carry = lax.fori_loop(0, qi, lambda j, c: step(j, c, False), init)
    carry = step(qi, carry, True)
    for h in range(heads):
        _, l, acc = carry[h]
        o_ref[:, h * V_HEAD:(h + 1) * V_HEAD] = (acc / l).astype(o_ref.dtype)


def flash_attention(q, k, v, *, batch, seq, blk, heads):
    nq = seq // blk
    return pl.pallas_call(
        functools.partial(_flash_body, blk=blk, heads=heads),
        grid=(batch, MLA_HEADS // heads, nq),
        in_specs=[pl.BlockSpec((blk, heads * Q_SLOT), lambda b, h, i: (b * nq + i, h)),
                  pl.BlockSpec((seq, heads * Q_SLOT), lambda b, h, i: (b, h)),
                  pl.BlockSpec((seq, heads * V_HEAD), lambda b, h, i: (b, h))],
        out_specs=pl.BlockSpec((blk, heads * V_HEAD), lambda b, h, i: (b * nq + i, h)),
        out_shape=jax.ShapeDtypeStruct((batch * seq, MLA_HEADS * V_HEAD), BF16),
        compiler_params=_cparams("parallel", "parallel", "parallel"),
        name="flash_attention",
    )(q, k, v)


def _decode_body(pt_ref, q_ref, qpe_ref, newc_ref, newk_ref, ckv_hbm, kpe_hbm, o_ref,
                 kbuf, pbuf, sems, m_sc, l_sc, acc_sc, *, layer, pages_per_step, n_steps, n_new):
    pps = pages_per_step
    c = pl.program_id(1)
    n_total = pl.num_programs(0) * n_steps
    g = pl.program_id(0) * n_steps + c
    slot = lax.rem(g, 2)
    nxt = lax.rem(g + 1, n_total)

    def page_copies(chunk, slot_, i):
        pg = pt_ref[chunk * pps + i]
        return (pltpu.make_async_copy(ckv_hbm.at[layer, pg], kbuf.at[slot_, i], sems.at[slot_, 0]),
                pltpu.make_async_copy(kpe_hbm.at[layer, pg], pbuf.at[slot_, i], sems.at[slot_, 1]))

    @pl.when(g == 0)
    def _():
        for i in range(pps):
            for cp in page_copies(0, 0, i):
                cp.start()

    for i in range(pps):
        for cp in page_copies(g, slot, i):
            cp.wait()

    @pl.when(c == 0)
    def _():
        m_sc[...] = jnp.full(m_sc.shape, -jnp.inf, F32)
        l_sc[...] = jnp.zeros(l_sc.shape, F32)
        acc_sc[...] = jnp.zeros(acc_sc.shape, F32)

    q = q_ref[...]
    qp = qpe_ref[:, :QK_ROPE]
    ks, ss = [], []
    for i in range(pps):
        for cp in page_copies(nxt, 1 - slot, i):
            cp.start()
        k = kbuf[slot, i].astype(BF16)
        kp_t = pbuf[slot, i].astype(BF16)
        ks.append(k)
        ss.append(_dot_nt(q, k) + _dot(qp, kp_t))
    s = jnp.concatenate(ss, axis=1)
    m_prev = m_sc[:, :1]
    l_prev = l_sc[:, :1]
    m_new = jnp.maximum(m_prev, jnp.max(s, axis=1, keepdims=True))
    alpha = jnp.exp2(m_prev - m_new)
    p = jnp.exp2(s - m_new)
    l_new = alpha * l_prev + jnp.sum(p, axis=1, keepdims=True)
    page = ks[0].shape[0]
    pv = _dot(p[:, :page].astype(BF16), ks[0])
    for i in range(1, pps):
        pv = pv + _dot(p[:, i * page:(i + 1) * page].astype(BF16), ks[i])
    acc_new = alpha * acc_sc[...] + pv
    m_sc[...] = jnp.broadcast_to(m_new, m_sc.shape)
    l_sc[...] = jnp.broadcast_to(l_new, l_sc.shape)
    acc_sc[...] = acc_new

    @pl.when(c == n_steps - 1)
    def _():
        qf = q.astype(F32)
        qpf = qp.astype(F32)
        kn = newc_ref[...]
        kpn = newk_ref[:, :QK_ROPE]
        row = lax.broadcasted_iota(jnp.int32, (q.shape[0], 1), 0)
        sj = []
        for j in range(n_new):
            v = (jnp.sum(qf * kn[j:j + 1, :], axis=1, keepdims=True)
                 + jnp.sum(qpf * kpn[j:j + 1, :], axis=1, keepdims=True))
            sj.append(jnp.where(row >= j * MLA_HEADS, v, -jnp.inf))
        m_fin = m_new
        for v in sj:
            m_fin = jnp.maximum(m_fin, v)
        a2 = jnp.exp2(m_new - m_fin)
        l_fin = a2 * l_new
        acc_fin = a2 * acc_new
        for j in range(n_new):
            pj = jnp.exp2(sj[j] - m_fin)
            l_fin = l_fin + pj
            acc_fin = acc_fin + pj * kn[j:j + 1, :]
        o_ref[...] = (acc_fin / l_fin).astype(o_ref.dtype)

    @pl.when(g == n_total - 1)
    def _():
        for i in range(pps):
            for cp in page_copies(nxt, 1 - slot, i):
                cp.wait()


def decode_attention(q_lat, q_pe, cache_ckv, cache_kpe, layer, page_table, new_ckv, new_kpe, *, pages_per_step):
    nb, rows, _ = q_lat.shape
    n_pages = page_table.shape[1]
    page = cache_ckv.shape[2]
    pps = pages_per_step
    n_steps = n_pages // pps
    n_new = rows // MLA_HEADS

    assert n_pages == n_steps * pps
    in_specs = [pl.BlockSpec((None, rows, KV_LORA), lambda b, c, pt: (b, 0, 0)),
                pl.BlockSpec((None, rows, LANES), lambda b, c, pt: (b, 0, 0)),
                pl.BlockSpec((None, 8, KV_LORA), lambda b, c, pt: (b, 0, 0)),
                pl.BlockSpec((None, 8, LANES), lambda b, c, pt: (b, 0, 0)),
                pl.BlockSpec(memory_space=pl.ANY), pl.BlockSpec(memory_space=pl.ANY)]
    grid_spec = pltpu.PrefetchScalarGridSpec(
        num_scalar_prefetch=1,
        grid=(nb, n_steps),
        in_specs=in_specs,
        out_specs=pl.BlockSpec((None, rows, KV_LORA), lambda b, c, pt: (b, 0, 0)),
        scratch_shapes=[pltpu.VMEM((2, pps, page, KV_LORA), F32), pltpu.VMEM((2, pps, QK_ROPE, page), F32),
                        pltpu.SemaphoreType.DMA((2, 2)),
                        pltpu.VMEM((rows, LANES), F32), pltpu.VMEM((rows, LANES), F32),
                        pltpu.VMEM((rows, KV_LORA), F32)],
    )
    return pl.pallas_call(
        functools.partial(_decode_body, layer=layer, pages_per_step=pps, n_steps=n_steps, n_new=n_new),
        grid_spec=grid_spec,
        out_shape=jax.ShapeDtypeStruct((nb, rows, KV_LORA), BF16),
        compiler_params=_cparams("arbitrary", "arbitrary"),
        name="decode_attention",
    )(page_table.reshape(-1), q_lat, q_pe, new_ckv, new_kpe, cache_ckv, cache_kpe)


def _group_rmsnorm(y, g):
    gw = SSM_INNER // SSM_GROUPS
    parts = []
    for i in range(SSM_GROUPS):
        yg = y[:, i * gw:(i + 1) * gw]
        parts.append(yg * lax.rsqrt(jnp.mean(yg * yg, axis=-1, keepdims=True) + EPS))
    return jnp.concatenate(parts, axis=1) * g


def _ssd_state_update(st, bm, xd, acs_x):
    last = acs_x.shape[0] - 1
    xde = (xd * jnp.exp(acs_x[last:last + 1, :] - acs_x)).astype(BF16)
    gw = SSM_INNER // SSM_GROUPS
    upd = [_dot_tn(bm[:, g * SSM_STATE:(g + 1) * SSM_STATE].astype(BF16), xde[:, g * gw:(g + 1) * gw])
           for g in range(SSM_GROUPS)]
    return st * jnp.exp(acs_x[last:last + 1, :]) + jnp.concatenate(upd, axis=1)


def _ssd_y_off(st, cm, acs_x):
    gw = SSM_INNER // SSM_GROUPS
    parts = [_dot(cm[:, g * SSM_STATE:(g + 1) * SSM_STATE].astype(BF16), st[:, g * gw:(g + 1) * gw].astype(BF16))
             for g in range(SSM_GROUPS)]
    return jnp.concatenate(parts, axis=1) * jnp.exp(acs_x)


def _ssd_prompt_body(z_ref, x_ref, bc_ref, dt_ref, cw_ref, cb_ref, dtb_ref, alog_ref, alogx_ref, dvec_ref, gn_ref,
                     e_ref, tril_ref, y_ref, st_ref, xp_sc, st_sc, *, n_chunks):
    L = SSM_CHUNK
    c = pl.program_id(1)

    @pl.when(c == 0)
    def _():
        xp_sc[0:8, :] = jnp.zeros((8, xp_sc.shape[1]), F32)
        st_sc[...] = jnp.zeros(st_sc.shape, F32)

    @pl.when(c > 0)
    def _():
        xp_sc[0:8, :] = xp_sc[L:L + 8, :]

    xp_sc[8:8 + L, 0:SSM_INNER] = x_ref[...]
    xp_sc[8:8 + L, SSM_INNER:] = bc_ref[...]
    conv = xp_sc[pl.ds(8 - (SSM_CONV - 1), L), :] * cw_ref[0:1, :]
    for k in range(1, SSM_CONV):
        conv = conv + xp_sc[pl.ds(8 - (SSM_CONV - 1) + k, L), :] * cw_ref[k:k + 1, :]
    xbc = _silu(conv + cb_ref[...])
    xs = xbc[:, :SSM_INNER]
    bm = xbc[:, SSM_INNER:SSM_INNER + SSM_GROUPS * SSM_STATE]
    cm = xbc[:, SSM_INNER + SSM_GROUPS * SSM_STATE:]

    hi = lax.Precision.HIGHEST
    dt = _softplus(dt_ref[...] + dtb_ref[...])
    tril = tril_ref[...]
    acs = _dot(tril, dt * (-jnp.exp(alog_ref[...])), precision=hi)
    acs_t = acs.T
    dt_x = _dot(dt, e_ref[...], precision=hi)
    acs_x = _dot(tril, dt_x * (-jnp.exp(alogx_ref[...])), precision=hi)
    xd = xs * dt_x
    xd_b = xd.astype(BF16)

    row = lax.broadcasted_iota(jnp.int32, (L, L), 0)
    col = lax.broadcasted_iota(jnp.int32, (L, L), 1)
    causal = row >= col
    lane = lax.broadcasted_iota(jnp.int32, (L, LANES), 1)
    heads_per_group = SSM_HEADS // SSM_GROUPS
    y_parts = []
    cb = [_dot_nt(cm[:, g * SSM_STATE:(g + 1) * SSM_STATE].astype(BF16),
                  bm[:, g * SSM_STATE:(g + 1) * SSM_STATE].astype(BF16)) for g in range(SSM_GROUPS)]
    for pair in range(SSM_HEADS // 2):
        xd_pair = xd_b[:, pair * LANES:(pair + 1) * LANES]
        outs = []
        for h in (2 * pair, 2 * pair + 1):
            decay = jnp.exp(jnp.where(causal, acs[:, h:h + 1] - acs_t[h:h + 1, :], -jnp.inf))
            outs.append(_dot((cb[h // heads_per_group] * decay).astype(BF16), xd_pair))
        y_parts.append(jnp.where(lane < SSM_HEAD_DIM, outs[0], outs[1]))
    y_diag = jnp.concatenate(y_parts, axis=1)

    st = st_sc[...]
    y = (y_diag + _ssd_y_off(st, cm, acs_x)) + dvec_ref[...] * xs
    y = y * _silu(z_ref[...])
    y_ref[...] = _group_rmsnorm(y, gn_ref[...]).astype(y_ref.dtype)
    st_new = _ssd_state_update(st, bm, xd, acs_x)
    st_sc[...] = st_new

    @pl.when(c == n_chunks - 1)
    def _():
        st_ref[...] = st_new.T


def _ssd_consts(conv_w, conv_b, dt_bias, a_log, d_vec, g_norm):
    pad = LANES - SSM_HEADS
    e_np = np.zeros((LANES, SSM_INNER), np.float32)
    for hh in range(SSM_HEADS):
        e_np[hh, hh * SSM_HEAD_DIM:(hh + 1) * SSM_HEAD_DIM] = 1.0
    e_mat = jnp.asarray(e_np)
    return dict(
        cw=conv_w, cb=conv_b.reshape(1, -1),
        dtb=jnp.pad(dt_bias, (0, pad)).reshape(1, LANES),
        alog=jnp.pad(a_log, (0, pad)).reshape(1, LANES),
        alogx=jnp.repeat(a_log, SSM_HEAD_DIM).reshape(1, SSM_INNER),
        dvec=jnp.repeat(d_vec, SSM_HEAD_DIM).reshape(1, SSM_INNER),
        gn=g_norm.reshape(1, SSM_INNER), e=e_mat)


def _full(shape):
    nd = len(shape)
    return pl.BlockSpec(shape, lambda *_: (0,) * nd)


def ssd_prompt(proj, consts, *, batch, seq):
    L = SSM_CHUNK
    nc = seq // L
    cdim = SSM_INNER + 2 * SSM_GROUPS * SSM_STATE
    tril = jnp.asarray(np.tril(np.ones((L, L), np.float32)))

    def rows(width, col):
        return pl.BlockSpec((L, width), lambda b, c: (b * nc + c, col // width))

    k = consts
    return pl.pallas_call(
        functools.partial(_ssd_prompt_body, n_chunks=nc),
        grid=(batch, nc),
        in_specs=[rows(SSM_INNER, COL_Z), rows(SSM_INNER, COL_X), rows(2 * SSM_GROUPS * SSM_STATE, COL_BC),
                  rows(LANES, COL_DT),
                  _full((SSM_CONV, cdim)), _full((1, cdim)), _full((1, LANES)), _full((1, LANES)),
                  _full((1, SSM_INNER)), _full((1, SSM_INNER)), _full((1, SSM_INNER)),
                  _full((LANES, SSM_INNER)), _full((L, L))],
        out_specs=[pl.BlockSpec((L, SSM_INNER), lambda b, c: (b * nc + c, 0)),
                   pl.BlockSpec((None, SSM_INNER, SSM_STATE), lambda b, c: (b, 0, 0))],
        out_shape=[jax.ShapeDtypeStruct((batch * seq, SSM_INNER), BF16),
                   jax.ShapeDtypeStruct((batch, SSM_INNER, SSM_STATE), F32)],
        scratch_shapes=[pltpu.VMEM((L + 8, cdim), F32), pltpu.VMEM((SSM_STATE, SSM_INNER), F32)],
        compiler_params=_cparams("parallel", "arbitrary"),
        name="ssd_prompt",
    )(proj, proj, proj, proj, k["cw"], k["cb"], k["dtb"], k["alog"], k["alogx"], k["dvec"], k["gn"], k["e"], tril)


def _sample_mixer_body(z_ref, dt_ref, scb_ref, xp_ref, up_c_ref, up_v_ref, st_in_ref,
                       cw_ref, cb_ref, dtb_ref, alogx_ref, dvec_ref, gn_ref, e_ref, scw_ref,
                       y_ref, ysc_ref, st_ref, u_out_ref, bpad_sc, xdpad_sc, u_sc, *, n_new, group):
    @pl.when(pl.program_id(0) == 0)
    def _():
        bpad_sc[...] = jnp.zeros(bpad_sc.shape, F32)
        xdpad_sc[...] = jnp.zeros(xdpad_sc.shape, F32)

    for i in range(group):
        _sample_mixer_one(z_ref.at[i], dt_ref.at[i], scb_ref.at[i], xp_ref.at[i], up_c_ref.at[i], up_v_ref.at[i],
                          st_in_ref.at[i], cw_ref, cb_ref, dtb_ref, alogx_ref, dvec_ref, gn_ref, e_ref, scw_ref,
                          y_ref.at[i], ysc_ref.at[i], st_ref.at[i], u_out_ref.at[i],
                          bpad_sc.at[i], xdpad_sc.at[i], u_sc.at[i], n_new=n_new)


def _sample_mixer_one(z_ref, dt_ref, scb_ref, xp_ref, up_c_ref, up_v_ref, st_in_ref,
                      cw_ref, cb_ref, dtb_ref, alogx_ref, dvec_ref, gn_ref, e_ref, scw_ref,
                      y_ref, ysc_ref, st_ref, u_out_ref, bpad_sc, xdpad_sc, u_sc, *, n_new):
    R = 8
    conv = xp_ref[pl.ds(8 - (SSM_CONV - 1), R), :] * cw_ref[0:1, :]
    for k in range(1, SSM_CONV):
        conv = conv + xp_ref[pl.ds(8 - (SSM_CONV - 1) + k, R), :] * cw_ref[k:k + 1, :]
    xbc = _silu(conv + cb_ref[...])
    xs = xbc[:, :SSM_INNER]
    bm = xbc[:, SSM_INNER:SSM_INNER + SSM_GROUPS * SSM_STATE]
    cm = xbc[:, SSM_INNER + SSM_GROUPS * SSM_STATE:]

    hi = lax.Precision.HIGHEST
    rowl = lax.broadcasted_iota(jnp.int32, (R, LANES), 0)
    dt = jnp.where(rowl < n_new, _softplus(dt_ref[...] + dtb_ref[...]), 0.0)
    dt_x = _dot(dt, e_ref[...], precision=hi)
    da_x = dt_x * (-jnp.exp(alogx_ref[...]))
    rowx = lax.broadcasted_iota(jnp.int32, (R, SSM_INNER), 0)
    acs_x = jnp.zeros((R, SSM_INNER), F32)
    for s in range(n_new):
        acs_x = acs_x + jnp.where(rowx >= s, da_x[s:s + 1, :], 0.0)
    xd = xs * dt_x

    gw = SSM_INNER // SSM_GROUPS
    lanex = lax.broadcasted_iota(jnp.int32, (R, SSM_INNER), 1)
    y_diag = jnp.zeros((R, SSM_INNER), F32)
    for s in range(n_new):
        cbs = [jnp.sum(cm[:, g * SSM_STATE:(g + 1) * SSM_STATE] * bm[s:s + 1, g * SSM_STATE:(g + 1) * SSM_STATE],
                       axis=1, keepdims=True) for g in range(SSM_GROUPS)]
        cb_x = jnp.where(lanex < gw, cbs[0], cbs[1])
        decay = jnp.exp(jnp.where(rowx >= s, acs_x - acs_x[s:s + 1, :], -jnp.inf))
        y_diag = y_diag + (cb_x * decay) * xd[s:s + 1, :]

    st = st_in_ref[...].T
    y_off_parts = [_dot(cm[:, g * SSM_STATE:(g + 1) * SSM_STATE], st[:, g * gw:(g + 1) * gw])
                   for g in range(SSM_GROUPS)]
    y_off = jnp.concatenate(y_off_parts, axis=1) * jnp.exp(acs_x)
    y = (y_diag + y_off) + dvec_ref[...] * xs
    y = y * _silu(z_ref[...])
    y_ref[...] = _group_rmsnorm(y, gn_ref[...])

    bpad_sc[0:R, :] = bm
    xdpad_sc[0:R, :] = xd * jnp.exp(acs_x[R - 1:R, :] - acs_x)
    upd = [_dot_tn(bpad_sc[:, g * SSM_STATE:(g + 1) * SSM_STATE], xdpad_sc[:, g * gw:(g + 1) * gw])
           for g in range(SSM_GROUPS)]
    st_new = st * jnp.exp(acs_x[R - 1:R, :]) + jnp.concatenate(upd, axis=1)
    st_ref[...] = st_new.T

    u_sc[...] = up_c_ref[...] * up_v_ref[...]
    sconv = u_sc[pl.ds(8 - (SC_CONV - 1), R), :] * scw_ref[0:1, :]
    for k in range(1, SC_CONV):
        sconv = sconv + u_sc[pl.ds(8 - (SC_CONV - 1) + k, R), :] * scw_ref[k:k + 1, :]
    ysc_ref[...] = scb_ref[...] * sconv
    u_out_ref[...] = u_sc[8:16, :]


def sample_mixer(proj_s, xp_s, up_c, up_v, st_all, layer, consts, sc_conv_w, *, n_new, group):
    nb = proj_s.shape[0]
    cdim = SSM_INNER + 2 * SSM_GROUPS * SSM_STATE
    k = consts
    gs = group

    def rows(width, col):
        return pl.BlockSpec((gs, 8, width), lambda b: (b, 0, col // width))

    def per_sample(*tail):
        return pl.BlockSpec((gs,) + tail, lambda b: (b,) + (0,) * len(tail))

    st_spec = pl.BlockSpec((None, gs, SSM_INNER, SSM_STATE), lambda b: (layer, b, 0, 0))
    args = [proj_s, proj_s, proj_s, xp_s, up_c, up_v, st_all,
            k["cw"], k["cb"], k["dtb"], k["alogx"], k["dvec"], k["gn"], k["e"], sc_conv_w]
    in_specs = [rows(SSM_INNER, COL_Z), rows(LANES, COL_DT), rows(SC_WIDTH, COL_SCB),
                per_sample(16, cdim), per_sample(16, SC_WIDTH), per_sample(16, SC_WIDTH), st_spec,
                _full((SSM_CONV, cdim)), _full((1, cdim)), _full((1, LANES)),
                _full((1, SSM_INNER)), _full((1, SSM_INNER)), _full((1, SSM_INNER)),
                _full((LANES, SSM_INNER)), _full((SC_CONV, SC_WIDTH))]
    return pl.pallas_call(
        functools.partial(_sample_mixer_body, n_new=n_new, group=gs),
        grid=(nb // gs,),
        in_specs=in_specs,
        out_specs=[per_sample(8, SSM_INNER), per_sample(8, SC_WIDTH), per_sample(SSM_INNER, SSM_STATE),
                   per_sample(8, SC_WIDTH)],
        out_shape=[jax.ShapeDtypeStruct((nb, 8, SSM_INNER), F32), jax.ShapeDtypeStruct((nb, 8, SC_WIDTH), F32),
                   jax.ShapeDtypeStruct((nb, SSM_INNER, SSM_STATE), F32),
                   jax.ShapeDtypeStruct((nb, 8, SC_WIDTH), F32)],
        scratch_shapes=[pltpu.VMEM((gs, LANES, SSM_GROUPS * SSM_STATE), F32), pltpu.VMEM((gs, LANES, SSM_INNER), F32),
                        pltpu.VMEM((gs, 16, SC_WIDTH), F32)],
        compiler_params=_cparams("arbitrary"),
        name="sample_mixer",
    )(*args)


def _sconv_prompt_body(scb_ref, scc_ref, scv_ref, pc_ref, pv_ref, w_ref, y_ref, tail_ref, u_sc, *, ts):
    i = pl.program_id(1)
    hist = pc_ref[...] * pv_ref[...]
    u_sc[0:8, :] = jnp.where(i == 0, 0.0, hist)
    u_sc[8:8 + ts, :] = scc_ref[...] * scv_ref[...]
    conv = u_sc[pl.ds(8 - (SC_CONV - 1), ts), :] * w_ref[0:1, :]
    for k in range(1, SC_CONV):
        conv = conv + u_sc[pl.ds(8 - (SC_CONV - 1) + k, ts), :] * w_ref[k:k + 1, :]
    y_ref[...] = (scb_ref[...] * conv).astype(y_ref.dtype)
    tail_ref[...] = u_sc[ts:ts + 8, :]


def sconv_prompt(proj, sc_conv_w, *, batch, seq, ts):
    nt = seq // ts
    w = SC_WIDTH

    def rows(col):
        return pl.BlockSpec((ts, w), lambda b, i: (b * nt + i, col // w))

    def prev(col):
        return pl.BlockSpec((8, w), lambda b, i: (jnp.maximum((b * nt + i) * (ts // 8) - 1, 0), col // w))

    return pl.pallas_call(
        functools.partial(_sconv_prompt_body, ts=ts),
        grid=(batch, nt),
        in_specs=[rows(COL_SCB), rows(COL_SCC), rows(COL_SCV), prev(COL_SCC), prev(COL_SCV), _full((SC_CONV, w))],
        out_specs=[pl.BlockSpec((ts, w), lambda b, i: (b * nt + i, 0)),
                   pl.BlockSpec((None, 8, w), lambda b, i: (b, 0, 0))],
        out_shape=[jax.ShapeDtypeStruct((batch * seq, w), BF16), jax.ShapeDtypeStruct((batch, 8, w), F32)],
        scratch_shapes=[pltpu.VMEM((ts + 8, w), F32)],
        compiler_params=_cparams("parallel", "arbitrary"),
        name="sconv_prompt",
    )(proj, proj, proj, proj, proj, sc_conv_w)


def _merge_body(h_ref, yap_ref, ysp_ref, ycp_ref, yas_ref, yss_ref, ycs_ref,
                wga_ref, wgb_ref, wgc_ref, ba_ref, bb_ref, bc_ref, wa_ref, wb_ref, wc_ref, o_ref,
                wg_sc, wb_sc, *, n_prompt_tiles):
    i = pl.program_id(1)

    @pl.when(i == 0)
    def _():
        for n, r in enumerate((wga_ref, wgb_ref, wgc_ref)):
            wg_sc[n] = r[...].astype(BF16)
        for n, r in enumerate((wa_ref, wb_ref, wc_ref)):
            wb_sc[n] = r[...].astype(BF16)

    h = h_ref[...]
    is_sample = i >= n_prompt_tiles
    out = None
    for n, (p_ref, s_ref, b_ref) in enumerate(((yap_ref, yas_ref, ba_ref), (ysp_ref, yss_ref, bb_ref),
                                               (ycp_ref, ycs_ref, bc_ref))):
        y = jnp.where(is_sample, s_ref[...], p_ref[...])
        term = _sigmoid(_dot(h, wg_sc[n]) + b_ref[...]) * _dot(y, wb_sc[n])
        out = term if out is None else out + term
    o_ref[...] = out.astype(o_ref.dtype)


def gated_merge(h, y_prompt, y_sample, w_gate, b_gate, w_branches, layer, *, tm, tn):
    m, d = h.shape
    mp, kb = y_prompt[0].shape
    nb = d // tn
    npt = mp // tm
    b2 = b_gate.reshape(b_gate.shape[0], 1, -1)
    hspec = pl.BlockSpec((tm, d), lambda j, i: (i, 0))
    pspec = pl.BlockSpec((tm, kb), lambda j, i: (jnp.minimum(i, npt - 1), 0))
    once = pl.Buffered(1)
    sspec = pl.BlockSpec((tm, kb), lambda j, i: (jnp.maximum(i - npt, 0), 0), pipeline_mode=once)

    def wcol(rows_, off):
        return pl.BlockSpec((None, rows_, tn), lambda j, i: (layer, 0, off * nb + j), pipeline_mode=once)

    return pl.pallas_call(
        functools.partial(_merge_body, n_prompt_tiles=npt),
        grid=(nb, m // tm),
        in_specs=[hspec, pspec, pspec, pspec, sspec, sspec, sspec,
                  wcol(d, 0), wcol(d, 1), wcol(d, 2), wcol(1, 0), wcol(1, 1), wcol(1, 2),
                  wcol(kb, 0), wcol(kb, 0), wcol(kb, 0)],
        out_specs=pl.BlockSpec((tm, tn), lambda j, i: (i, j)),
        out_shape=jax.ShapeDtypeStruct((m, d), BF16),
        scratch_shapes=[pltpu.VMEM((3, d, tn), BF16), pltpu.VMEM((3, kb, tn), BF16)],
        compiler_params=_cparams("parallel", "arbitrary"),
        name="gated_merge",
    )(h, *y_prompt, *y_sample, w_gate, w_gate, w_gate, b2, b2, b2, *w_branches)


def _rope_tables(pos):
    half = QK_ROPE // 2
    inv = ROPE_THETA ** (-jnp.arange(half, dtype=F32) / half)
    ang = pos.astype(F32)[:, None] * inv[None, :]
    c, s = jnp.cos(ang), jnp.sin(ang)
    z = jnp.zeros((pos.shape[0], LANES - QK_ROPE), F32)
    return jnp.concatenate([c, c, z], axis=1), jnp.concatenate([-s, s, z], axis=1)


def _in_proj_row_offsets(layer):
    sizes = (Q_LORA, KV_LORA, QK_ROPE, SSM_INNER, SSM_INNER + 2 * SSM_GROUPS * SSM_STATE, SSM_HEADS,
             SC_WIDTH, SC_WIDTH, SC_WIDTH)
    q_c, kv_c, k_pe, z, xbc, dt, sc_b, sc_c, sc_v = (int(v) for v in np.cumsum((0,) + sizes)[:-1])
    d_in = int(sum(sizes))
    starts = []
    for first, width in ((sc_b, SC_WIDTH), (sc_c, SC_WIDTH), (sc_v, SC_WIDTH), (z, SSM_INNER),
                         (xbc, SSM_INNER + 2 * SSM_GROUPS * SSM_STATE), (q_c, Q_LORA), (kv_c, KV_LORA),
                         (k_pe, PROJ_TILE), (dt, PROJ_TILE)):
        starts += [first + t for t in range(0, width, PROJ_TILE)]
    assert len(starts) * PROJ_TILE == D_IN_PAD and max(starts) + PROJ_TILE <= d_in
    return tuple(layer * d_in + r for r in starts)


def kernel(x_prompt, x_sample, cache_ckv, cache_kpe, state_ssm, state_mconv, state_sconv, page_table, g_attn_norm, w_in, g_q_a, w_q_b, g_kv_a, w_kv_b, ssm_conv_w, ssm_conv_b, ssm_dt_bias, ssm_a_log, ssm_d, g_ssm_norm, sc_conv_w, w_gate, b_gate, w_br_attn, w_br_ssm, w_br_sc, w_o, g_ffn_norm, w_ff_gate, w_ff_up, w_ff_down, w_router, w_e_gate, w_e_up, w_e_down, g_final):
    bp, tp, d = x_prompt.shape
    bs, ts, _ = x_sample.shape
    depth = w_in.shape[0]
    mp, ms = bp * tp, bs * ts
    m = mp + ms
    n_past = page_table.shape[1] * cache_ckv.shape[2]
    tm = m // 8
    tmh = m // 16
    tmw = m // 4
    cdim = SSM_INNER + 2 * SSM_GROUPS * SSM_STATE

    x = jnp.concatenate([x_prompt.reshape(mp, d), x_sample.reshape(ms, d)], axis=0)
    pos = jnp.concatenate([jnp.tile(jnp.arange(tp), bp), jnp.tile(n_past + jnp.arange(ts), bs)])
    cos, sin = _rope_tables(pos)
    cache_kpe_t = jnp.swapaxes(cache_kpe, 2, 3)
    st_all = state_ssm.reshape(depth, bs, SSM_INNER, SSM_STATE)
    w_in_t = jnp.swapaxes(w_in, 1, 2).reshape(-1, d)

    outs = {k: [] for k in ("p_ckv", "p_kpe", "p_ssm", "p_mconv", "p_sconv", "s_ckv", "s_kpe", "s_ssm", "s_mconv", "s_sconv")}
    for l in range(depth):
        wq = w_q_b[l]
        w_q_slots = jnp.concatenate([wq, jnp.zeros(wq.shape[:2] + (Q_SLOT - wq.shape[2],), wq.dtype)], axis=2)
        w_q_slots = w_q_slots.reshape(Q_LORA, MLA_HEADS * Q_SLOT).astype(BF16)
        wkv = w_kv_b[l]
        w_kv_flat = jnp.concatenate([wkv[..., :QK_NOPE].reshape(KV_LORA, -1), wkv[..., QK_NOPE:].reshape(KV_LORA, -1)],
                                    axis=1).astype(BF16)
        w_uk_t = jnp.transpose(wkv[..., :QK_NOPE], (1, 2, 0)).astype(BF16)
        w_uv = jnp.transpose(wkv[..., QK_NOPE:], (1, 0, 2)).astype(BF16)
        consts = _ssd_consts(ssm_conv_w[l], ssm_conv_b[l], ssm_dt_bias[l], ssm_a_log[l], ssm_d[l], g_ssm_norm[l])

        h = rmsnorm(x, g_attn_norm[l], BF16, tm)
        proj = in_proj(h, w_in_t, _in_proj_row_offsets(l), tm=tmw, tn=PROJ_TILE)
        qn, ckv, kpe = mla_prep(proj, g_q_a[l], g_kv_a[l], cos, sin, tm=tm)
        q = qproj(qn, w_q_slots, cos, sin, tm=tm)

        k_full, v_full = kv_expand(ckv, kpe, w_kv_flat, rows=mp, tm=1024)
        ya_p = flash_attention(q, k_full, v_full, batch=bp, seq=tp, blk=1024, heads=2)

        q_lat = blockdiag_matmul(q, w_uk_t, tm=ms, row_block=mp // ms, col_block0=0, col_stride=2)
        q_pe_s = q[mp:].reshape(ms, MLA_HEADS, Q_SLOT)[:, :, QK_NOPE:].reshape(bs, ts * MLA_HEADS, LANES)
        ckv_s = ckv[mp:].reshape(bs, ts, KV_LORA)
        kpe_s = kpe[mp:].reshape(bs, ts, LANES)
        new_c = jnp.pad(ckv_s, ((0, 0), (0, 8 - ts), (0, 0)))
        new_k = jnp.pad(kpe_s, ((0, 0), (0, 8 - ts), (0, 0)))
        o_lat = decode_attention(q_lat.reshape(bs, ts * MLA_HEADS, KV_LORA), q_pe_s, cache_ckv, cache_kpe_t, l,
                                 page_table, new_c, new_k, pages_per_step=32)
        ya_s = blockdiag_matmul(o_lat.reshape(ms, MLA_HEADS * KV_LORA), w_uv, tm=ms, row_block=0, col_block0=0,
                                col_stride=1)

        ys_p, ssm_p = ssd_prompt(proj, consts, batch=bp, seq=tp)
        yc_p, u_tail_p = sconv_prompt(proj, sc_conv_w[l], batch=bp, seq=tp, ts=512)

        proj_s = jnp.pad(proj[mp:].reshape(bs, ts, D_IN_PAD), ((0, 0), (0, 8 - ts), (0, 0)))
        xbc_s = proj_s[:, :ts, COL_X:COL_X + cdim]
        xp_s = jnp.concatenate([jnp.zeros((bs, 8 - (SSM_CONV - 1), cdim), F32), state_mconv[l], xbc_s,
                                jnp.zeros((bs, 8 - ts, cdim), F32)], axis=1)
        zpad = jnp.zeros((bs, 8 - (SC_CONV - 1), SC_WIDTH), F32)
        zend = jnp.zeros((bs, 8 - ts, SC_WIDTH), F32)
        up_c = jnp.concatenate([zpad, state_sconv[l], proj_s[:, :ts, COL_SCC:COL_SCC + SC_WIDTH], zend], axis=1)
        up_v = jnp.concatenate([zpad, jnp.ones_like(state_sconv[l]), proj_s[:, :ts, COL_SCV:COL_SCV + SC_WIDTH], zend], axis=1)
        ys_s, yc_s, ssm_s, u_new_s = sample_mixer(proj_s, xp_s, up_c, up_v, st_all, l, consts, sc_conv_w[l],
                                                  n_new=ts, group=4)

        y_sample = (ya_s, ys_s[:, :ts].reshape(ms, SSM_INNER).astype(BF16), yc_s[:, :ts].reshape(ms, SC_WIDTH).astype(BF16))
        merged = gated_merge(h, (ya_p, ys_p, yc_p), y_sample, w_gate, b_gate, (w_br_attn, w_br_ssm, w_br_sc), l,
                             tm=ms, tn=512)
        x = matmul(merged, w_o, tm=tm, tn=512, res=x, w_index=l)

        outs["p_ckv"].append(ckv[:mp].reshape(bp, tp, KV_LORA))
        outs["p_kpe"].append(kpe[:mp, :QK_ROPE].reshape(bp, tp, QK_ROPE))
        outs["p_ssm"].append(ssm_p.reshape(bp, SSM_HEADS, SSM_HEAD_DIM, SSM_STATE))
        outs["p_mconv"].append(jnp.stack([proj[(b + 1) * tp - (SSM_CONV - 1):(b + 1) * tp, COL_X:COL_X + cdim]
                                          for b in range(bp)]))
        outs["p_sconv"].append(u_tail_p[:, 8 - (SC_CONV - 1):])
        outs["s_ckv"].append(ckv_s)
        outs["s_kpe"].append(kpe_s[:, :, :QK_ROPE])
        outs["s_ssm"].append(ssm_s.reshape(bs, SSM_HEADS, SSM_HEAD_DIM, SSM_STATE))
        outs["s_mconv"].append(xp_s[:, 8 + ts - (SSM_CONV - 1):8 + ts])
        outs["s_sconv"].append(u_new_s[:, ts - (SC_CONV - 1):ts])

        i = l // 2
        if l % 2 == 0:
            h2 = rmsnorm(x, g_ffn_norm[l], BF16, tm)
            hdn = swiglu_up(h2, w_ff_gate[i], w_ff_up[i], tm=tmw, tf=512)
            x = matmul(hdn, w_ff_down[i].astype(BF16), tm=tm, tn=256, res=x, rows_outer=True)
        else:
            h2, gate = rmsnorm_router(x, g_ffn_norm[l], w_router[i], tm)
            hdn = moe_up(h2, w_e_gate[i].astype(BF16), w_e_up[i].astype(BF16), gate, tm=tm)
            wd = w_e_down[i].reshape(-1, d).astype(BF16)
            x = matmul(hdn, wd, tm=tmh, tn=256, res=x, rows_outer=True)

    y_p, y_s = rmsnorm_split(x, g_final, rows_prompt=mp, tm=ms)
    st = {k: jnp.stack(v, axis=0) for k, v in outs.items()}
    return (y_p.reshape(bp, tp, d), y_s.reshape(bs, ts, d),
            st["p_ckv"], st["p_kpe"], st["p_ssm"], st["p_mconv"], st["p_sconv"],
            st["s_ckv"], st["s_kpe"], st["s_ssm"], st["s_mconv"], st["s_sconv"])
```

```python
import functools

import jax
import jax.numpy as jnp
import numpy as np
from jax import lax
from jax.experimental import pallas as pl
from jax.experimental.pallas import tpu as pltpu

F32 = jnp.float32
BF16 = jnp.bfloat16
EPS = 1e-6
ROPE_THETA = 10000.0
LANES = 128
SUBLANES_BF16 = 16
MLA_HEADS = 8
QK_NOPE = 128
QK_ROPE = 64
V_HEAD = 128
KV_LORA = 512
Q_LORA = 512
Q_SLOT = 256
SSM_HEADS = 16
SSM_HEAD_DIM = 64
SSM_INNER = 1024
SSM_GROUPS = 2
SSM_STATE = 128
SSM_CONV = 4
SSM_CHUNK = 128
SC_WIDTH = 1024
SC_CONV = 3
N_EXPERTS = 8
ATTN_SCALE = (QK_NOPE + QK_ROPE) ** -0.5
Q_SCALE = ATTN_SCALE * float(np.log2(np.e))
VMEM_LIMIT = 56 * 1024 * 1024

COL_SCB, COL_SCC, COL_SCV, COL_Z, COL_X, COL_BC, COL_QC, COL_KVC = (
    0, 1024, 2048, 3072, 4096, 5120, 5632, 6144)
PROJ_TILE = 512
D_IN_PAD = 6656
SLAB_KPE, SLAB_DT = 0, 1
D_IN_SLABS = 2 * LANES


def _cparams(*sem):
    return pltpu.CompilerParams(dimension_semantics=sem, vmem_limit_bytes=VMEM_LIMIT)


def _sigmoid(x):
    return 1.0 / (1.0 + jnp.exp(-x))


def _silu(x):
    return x * _sigmoid(x)


def _softplus(x):
    return jnp.maximum(x, 0.0) + jnp.log1p(jnp.exp(-jnp.abs(x)))


def _dot(a, b, **kw):
    return jnp.dot(a, b, preferred_element_type=F32, **kw)


def _dot_nt(a, b):
    return lax.dot_general(a, b, (((1,), (1,)), ((), ())), preferred_element_type=F32)


def _dot_tn(a, b):
    return lax.dot_general(a, b, (((0,), (0,)), ((), ())), preferred_element_type=F32)


def _rms(x, g):
    r = lax.rsqrt(jnp.mean(x * x, axis=-1, keepdims=True) + EPS)
    return (x * r) * g


def _rmsnorm_body(x_ref, g_ref, o_ref):
    o_ref[...] = _rms(x_ref[...], g_ref[...]).astype(o_ref.dtype)


def rmsnorm(x, g, out_dtype, tm):
    m, d = x.shape
    return pl.pallas_call(
        _rmsnorm_body,
        grid=(m // tm,),
        in_specs=[pl.BlockSpec((tm, d), lambda i: (i, 0)), pl.BlockSpec((1, d), lambda i: (0, 0))],
        out_specs=pl.BlockSpec((tm, d), lambda i: (i, 0)),
        out_shape=jax.ShapeDtypeStruct((m, d), out_dtype),
        compiler_params=_cparams("parallel"),
        name="rmsnorm",
    )(x, g.reshape(1, d))


def _rmsnorm_split_body(x_ref, g_ref, op_ref, os_ref, *, n_prompt_tiles):
    i = pl.program_id(0)
    y = _rms(x_ref[...], g_ref[...])

    @pl.when(i < n_prompt_tiles)
    def _():
        op_ref[...] = y

    @pl.when(i >= n_prompt_tiles)
    def _():
        os_ref[...] = y


def rmsnorm_split(x, g, *, rows_prompt, tm):
    m, d = x.shape
    npt = rows_prompt // tm
    return pl.pallas_call(
        functools.partial(_rmsnorm_split_body, n_prompt_tiles=npt),
        grid=(m // tm,),
        in_specs=[pl.BlockSpec((tm, d), lambda i: (i, 0)), pl.BlockSpec((1, d), lambda i: (0, 0))],
        out_specs=[pl.BlockSpec((tm, d), lambda i: (jnp.minimum(i, npt - 1), 0)),
                   pl.BlockSpec((tm, d), lambda i: (jnp.maximum(i - npt, 0), 0))],
        out_shape=[jax.ShapeDtypeStruct((rows_prompt, d), F32), jax.ShapeDtypeStruct((m - rows_prompt, d), F32)],
        compiler_params=_cparams("arbitrary"),
        name="rmsnorm_split",
    )(x, g.reshape(1, d))


def _rmsnorm_router_body(x_ref, g_ref, wr_ref, h_ref, gate_ref):
    h = _rms(x_ref[...], g_ref[...])
    h_ref[...] = h.astype(h_ref.dtype)
    lg = _dot(h, wr_ref[...], precision=lax.Precision.HIGHEST)
    lane = lax.broadcasted_iota(jnp.int32, lg.shape, 1).astype(F32)
    lg = jnp.where(lane < N_EXPERTS, lg, -jnp.inf)
    m1 = jnp.max(lg, axis=1, keepdims=True)
    i1 = jnp.min(jnp.where(lg == m1, lane, float(LANES)), axis=1, keepdims=True)
    oh1 = lane == i1
    lg2 = jnp.where(oh1, -jnp.inf, lg)
    m2 = jnp.max(lg2, axis=1, keepdims=True)
    i2 = jnp.min(jnp.where(lg2 == m2, lane, float(LANES)), axis=1, keepdims=True)
    oh2 = lane == i2
    e = jnp.exp(m2 - m1)
    w1 = 1.0 / (1.0 + e)
    w2 = e / (1.0 + e)
    gate_ref[...] = jnp.where(oh1, w1, 0.0) + jnp.where(oh2, w2, 0.0)


def rmsnorm_router(x, g, w_router, tm):
    m, d = x.shape
    wr = jnp.pad(w_router, ((0, 0), (0, LANES - w_router.shape[1])))
    return pl.pallas_call(
        _rmsnorm_router_body,
        grid=(m // tm,),
        in_specs=[pl.BlockSpec((tm, d), lambda i: (i, 0)), pl.BlockSpec((1, d), lambda i: (0, 0)),
                  pl.BlockSpec((d, LANES), lambda i: (0, 0))],
        out_specs=[pl.BlockSpec((tm, d), lambda i: (i, 0)), pl.BlockSpec((tm, LANES), lambda i: (i, 0))],
        out_shape=[jax.ShapeDtypeStruct((m, d), BF16), jax.ShapeDtypeStruct((m, LANES), F32)],
        compiler_params=_cparams("parallel"),
        name="rmsnorm_router",
    )(x, g.reshape(1, d), wr)


def _cast_once(src_refs, dst_refs):
    @pl.when(pl.program_id(1) == 0)
    def _():
        for s, d in zip(src_refs, dst_refs):
            d[...] = s[...].astype(d.dtype)


def _mm_body(x_ref, w_ref, *rest, has_res, cast_w):
    rest = list(rest)
    r_ref = rest.pop(0) if has_res else None
    o_ref = rest.pop(0)
    if cast_w:
        (w_sc,) = rest
        _cast_once([w_ref], [w_sc])
        w_ref = w_sc
    acc = _dot(x_ref[...], w_ref[...])
    o_ref[...] = ((r_ref[...] + acc) if has_res else acc).astype(o_ref.dtype)


def matmul(x, w, *, tm, tn, res=None, out_dtype=F32, rows_outer=False, w_index=None):
    m, kd = x.shape
    n = w.shape[-1]
    cast_w = w.dtype != BF16
    assert not (cast_w and rows_outer)

    def ij(a, b):
        return (a, b) if rows_outer else (b, a)

    if w_index is None:
        w_spec = pl.BlockSpec((kd, tn), lambda a, b: (0, ij(a, b)[1]))
    else:
        w_spec = pl.BlockSpec((None, kd, tn), lambda a, b: (w_index, 0, ij(a, b)[1]))
    in_specs = [pl.BlockSpec((tm, kd), lambda a, b: (ij(a, b)[0], 0)), w_spec]
    args = [x, w]
    if res is not None:
        in_specs.append(pl.BlockSpec((tm, tn), lambda a, b: ij(a, b)))
        args.append(res)
    return pl.pallas_call(
        functools.partial(_mm_body, has_res=res is not None, cast_w=cast_w),
        grid=(m // tm, n // tn) if rows_outer else (n // tn, m // tm),
        in_specs=in_specs,
        out_specs=pl.BlockSpec((tm, tn), lambda a, b: ij(a, b)),
        out_shape=jax.ShapeDtypeStruct((m, n), out_dtype),
        scratch_shapes=[pltpu.VMEM((kd, tn), BF16)] if cast_w else [],
        compiler_params=_cparams("parallel", "arbitrary"),
        name="matmul",
    )(*args)


def _in_proj_body(offs_ref, x_ref, wt_ref, o_ref, w_sc):
    del offs_ref

    @pl.when(pl.program_id(1) == 0)
    def _():
        w_sc[...] = wt_ref[...].T.astype(w_sc.dtype)

    o_ref[...] = _dot(x_ref[...], w_sc[...])


def in_proj(x, w_t, row_offsets, *, tm, tn):
    m, kd = x.shape
    nt = len(row_offsets)
    assert all(o % SUBLANES_BF16 == 0 for o in row_offsets)
    grid_spec = pltpu.PrefetchScalarGridSpec(
        num_scalar_prefetch=1,
        grid=(nt, m // tm),
        in_specs=[pl.BlockSpec((tm, kd), lambda j, i, offs: (i, 0)),
                  pl.BlockSpec((pl.Element(tn), pl.Element(kd)),
                               lambda j, i, offs: (pl.multiple_of(offs[j], SUBLANES_BF16), 0))],
        out_specs=pl.BlockSpec((tm, tn), lambda j, i, offs: (i, j)),
        scratch_shapes=[pltpu.VMEM((kd, tn), BF16)],
    )
    return pl.pallas_call(
        _in_proj_body,
        grid_spec=grid_spec,
        out_shape=jax.ShapeDtypeStruct((m, nt * tn), F32),
        compiler_params=_cparams("parallel", "arbitrary"),
        name="in_proj",
    )(jnp.asarray(row_offsets, jnp.int32), x, w_t)


def _swiglu_body(x_ref, wg_ref, wu_ref, *rest, scaled, cast_w):
    rest = list(rest)
    gate_ref = rest.pop(0) if scaled else None
    o_ref = rest.pop(0)
    if cast_w:
        _cast_once([wg_ref, wu_ref], rest)
        wg_ref, wu_ref = rest
    x = x_ref[...]
    g = _dot(x, wg_ref[...])
    u = _dot(x, wu_ref[...])
    hdn = _silu(g) * u
    if scaled:
        gate = gate_ref[...]
        lane = lax.broadcasted_iota(jnp.int32, gate.shape, 1)
        sc = jnp.sum(jnp.where(lane == pl.program_id(0), gate, 0.0), axis=1, keepdims=True)
        hdn = hdn * sc
    o_ref[...] = hdn.astype(o_ref.dtype)


def swiglu_up(x, wg, wu, *, tm, tf):
    m, d = x.shape
    f = wg.shape[1]
    return pl.pallas_call(
        functools.partial(_swiglu_body, scaled=False, cast_w=True),
        grid=(f // tf, m // tm),
        in_specs=[pl.BlockSpec((tm, d), lambda j, i: (i, 0)), pl.BlockSpec((d, tf), lambda j, i: (0, j)),
                  pl.BlockSpec((d, tf), lambda j, i: (0, j))],
        out_specs=pl.BlockSpec((tm, tf), lambda j, i: (i, j)),
        out_shape=jax.ShapeDtypeStruct((m, f), BF16),
        scratch_shapes=[pltpu.VMEM((d, tf), BF16), pltpu.VMEM((d, tf), BF16)],
        compiler_params=_cparams("parallel", "arbitrary"),
        name="swiglu_up",
    )(x, wg, wu)


def moe_up(x, wg, wu, gate, *, tm):
    m, d = x.shape
    ne, _, f = wg.shape
    return pl.pallas_call(
        functools.partial(_swiglu_body, scaled=True, cast_w=False),
        grid=(ne, m // tm),
        in_specs=[pl.BlockSpec((tm, d), lambda j, i: (i, 0)), pl.BlockSpec((None, d, f), lambda j, i: (j, 0, 0)),
                  pl.BlockSpec((None, d, f), lambda j, i: (j, 0, 0)), pl.BlockSpec((tm, LANES), lambda j, i: (i, 0))],
        out_specs=pl.BlockSpec((tm, f), lambda j, i: (i, j)),
        out_shape=jax.ShapeDtypeStruct((m, ne * f), BF16),
        compiler_params=_cparams("parallel", "parallel"),
        name="moe_up",
    )(x, wg, wu, gate)


def _blockdiag_body(x_ref, w_ref, o_ref):
    o_ref[...] = _dot(x_ref[...], w_ref[...]).astype(o_ref.dtype)


def blockdiag_matmul(x, w, *, tm, row_block, col_block0, col_stride, out_dtype=BF16):
    nh, ki, no = w.shape
    return pl.pallas_call(
        _blockdiag_body,
        grid=(nh,),
        in_specs=[pl.BlockSpec((tm, ki), lambda h: (row_block, col_block0 + h * col_stride)),
                  pl.BlockSpec((None, ki, no), lambda h: (h, 0, 0))],
        out_specs=pl.BlockSpec((tm, no), lambda h: (0, h)),
        out_shape=jax.ShapeDtypeStruct((tm, nh * no), out_dtype),
        compiler_params=_cparams("parallel"),
        name="blockdiag_matmul",
    )(x, w)


def _rope_slab(x, cos, sin):
    half = QK_ROPE // 2
    lane = lax.broadcasted_iota(jnp.int32, x.shape, 1)
    swapped = jnp.where(lane < half, pltpu.roll(x, LANES - half, 1), pltpu.roll(x, half, 1))
    return x * cos + swapped * sin


def _mla_prep_body(qc_ref, kvc_ref, kpe_ref, gq_ref, gkv_ref, cos_ref, sin_ref, qn_ref, ckv_ref, kpe_out_ref):
    qn_ref[...] = _rms(qc_ref[...], gq_ref[...]).astype(qn_ref.dtype)
    ckv_ref[...] = _rms(kvc_ref[...], gkv_ref[...])
    kpe_out_ref[...] = _rope_slab(kpe_ref[...], cos_ref[...], sin_ref[...])


def mla_prep(proj, slabs, g_q, g_kv, cos, sin, *, tm):
    m = proj.shape[0]
    return pl.pallas_call(
        _mla_prep_body,
        grid=(m // tm,),
        in_specs=[pl.BlockSpec((tm, Q_LORA), lambda i: (i, COL_QC // Q_LORA)),
                  pl.BlockSpec((tm, KV_LORA), lambda i: (i, COL_KVC // KV_LORA)),
                  pl.BlockSpec((tm, LANES), lambda i: (i, SLAB_KPE)),
                  pl.BlockSpec((1, Q_LORA), lambda i: (0, 0)), pl.BlockSpec((1, KV_LORA), lambda i: (0, 0)),
                  pl.BlockSpec((tm, LANES), lambda i: (i, 0)), pl.BlockSpec((tm, LANES), lambda i: (i, 0))],
        out_specs=[pl.BlockSpec((tm, Q_LORA), lambda i: (i, 0)), pl.BlockSpec((tm, KV_LORA), lambda i: (i, 0)),
                   pl.BlockSpec((tm, LANES), lambda i: (i, 0))],
        out_shape=[jax.ShapeDtypeStruct((m, Q_LORA), BF16), jax.ShapeDtypeStruct((m, KV_LORA), F32),
                   jax.ShapeDtypeStruct((m, LANES), F32)],
        compiler_params=_cparams("parallel"),
        name="mla_prep",
    )(proj, proj, slabs, g_q.reshape(1, -1), g_kv.reshape(1, -1), cos, sin)


def _qproj_body(x_ref, w_ref, cos_ref, sin_ref, o_ref):
    acc = _dot(x_ref[...], w_ref[...])
    cos = cos_ref[...] * Q_SCALE
    sin = sin_ref[...] * Q_SCALE
    for h in range(MLA_HEADS):
        base = h * Q_SLOT
        o_ref[:, base:base + QK_NOPE] = (acc[:, base:base + QK_NOPE] * Q_SCALE).astype(o_ref.dtype)
        o_ref[:, base + QK_NOPE:base + Q_SLOT] = _rope_slab(acc[:, base + QK_NOPE:base + Q_SLOT], cos, sin).astype(o_ref.dtype)


def qproj(qn, w_q_slots, cos, sin, *, tm):
    m = qn.shape[0]
    n = w_q_slots.shape[1]
    return pl.pallas_call(
        _qproj_body,
        grid=(m // tm,),
        in_specs=[pl.BlockSpec((tm, Q_LORA), lambda i: (i, 0)), pl.BlockSpec((Q_LORA, n), lambda i: (0, 0)),
                  pl.BlockSpec((tm, LANES), lambda i: (i, 0)), pl.BlockSpec((tm, LANES), lambda i: (i, 0))],
        out_specs=pl.BlockSpec((tm, n), lambda i: (i, 0)),
        out_shape=jax.ShapeDtypeStruct((m, n), BF16),
        compiler_params=_cparams("parallel"),
        name="qproj",
    )(qn, w_q_slots, cos, sin)


def _kv_expand_body(ckv_ref, kpe_ref, w_ref, k_ref, v_ref):
    acc = _dot(ckv_ref[...].astype(BF16), w_ref[...])
    kpe = kpe_ref[...].astype(k_ref.dtype)
    for h in range(MLA_HEADS):
        base = h * Q_SLOT
        k_ref[:, base:base + QK_NOPE] = acc[:, h * QK_NOPE:(h + 1) * QK_NOPE].astype(k_ref.dtype)
        k_ref[:, base + QK_NOPE:base + Q_SLOT] = kpe
    v_ref[...] = acc[:, MLA_HEADS * QK_NOPE:].astype(v_ref.dtype)


def kv_expand(ckv, kpe, w_kv_flat, *, rows, tm):
    n = w_kv_flat.shape[1]
    return pl.pallas_call(
        _kv_expand_body,
        grid=(rows // tm,),
        in_specs=[pl.BlockSpec((tm, KV_LORA), lambda i: (i, 0)), pl.BlockSpec((tm, LANES), lambda i: (i, 0)),
                  pl.BlockSpec((KV_LORA, n), lambda i: (0, 0))],
        out_specs=[pl.BlockSpec((tm, MLA_HEADS * Q_SLOT), lambda i: (i, 0)),
                   pl.BlockSpec((tm, MLA_HEADS * V_HEAD), lambda i: (i, 0))],
        out_shape=[jax.ShapeDtypeStruct((rows, MLA_HEADS * Q_SLOT), BF16),
                   jax.ShapeDtypeStruct((rows, MLA_HEADS * V_HEAD), BF16)],
        compiler_params=_cparams("parallel"),
        name="kv_expand",
    )(ckv, kpe, w_kv_flat)


def _flash_body(q_ref, k_ref, v_ref, o_ref, *, blk, heads):
    qi = pl.program_id(2)
    qs = [q_ref[:, h * Q_SLOT:(h + 1) * Q_SLOT] for h in range(heads)]

    def update(j, diagonal, h, m, l, acc):
        start = pl.multiple_of(j * blk, blk)
        k = k_ref[pl.ds(start, blk), h * Q_SLOT:(h + 1) * Q_SLOT]
        v = v_ref[pl.ds(start, blk), h * V_HEAD:(h + 1) * V_HEAD]
        s = _dot_nt(qs[h], k)
        if diagonal:
            row = lax.broadcasted_iota(jnp.int32, s.shape, 0)
            col = lax.broadcasted_iota(jnp.int32, s.shape, 1)
            s = jnp.where(row >= col, s, -jnp.inf)
        m_new = jnp.maximum(m, jnp.max(s, axis=1, keepdims=True))
        alpha = jnp.exp2(m - m_new)
        p = jnp.exp2(s - m_new)
        l = alpha * l + jnp.sum(p, axis=1, keepdims=True)
        acc = alpha * acc + _dot(p.astype(BF16), v)
        return m_new, l, acc

    def step(j, carry, diagonal):
        return tuple(update(j, diagonal, h, *carry[h]) for h in range(heads))

    init = tuple((jnp.full((blk, 1), -jnp.inf, F32), jnp.zeros((blk, 1), F32), jnp.zeros((blk, V_HEAD), F32))
                 for _ in range(heads))
    carry = lax.fori_loop(0, qi, lambda j, c: step(j, c, False), init)
    carry = step(qi, carry, True)
    for h in range(heads):
        _, l, acc = carry[h]
        o_ref[:, h * V_HEAD:(h + 1) * V_HEAD] = (acc / l).astype(o_ref.dtype)


def flash_attention(q, k, v, *, batch, seq, blk, heads):
    nq = seq // blk
    return pl.pallas_call(
        functools.partial(_flash_body, blk=blk, heads=heads),
        grid=(batch, MLA_HEADS // heads, nq),
        in_specs=[pl.BlockSpec((blk, heads * Q_SLOT), lambda b, h, i: (b * nq + i, h)),
                  pl.BlockSpec((seq, heads * Q_SLOT), lambda b, h, i: (b, h)),
                  pl.BlockSpec((seq, heads * V_HEAD), lambda b, h, i: (b, h))],
        out_specs=pl.BlockSpec((blk, heads * V_HEAD), lambda b, h, i: (b * nq + i, h)),
        out_shape=jax.ShapeDtypeStruct((batch * seq, MLA_HEADS * V_HEAD), BF16),
        compiler_params=_cparams("parallel", "parallel", "parallel"),
        name="flash_attention",
    )(q, k, v)


def _decode_body(pt_ref, q_ref, qpe_ref, newc_ref, newk_ref, ckv_hbm, kpe_hbm, o_ref,
                 kbuf, pbuf, sems, m_sc, l_sc, acc_sc, *, layer, pages_per_step, n_steps, n_new):
    pps = pages_per_step
    c = pl.program_id(1)
    n_total = pl.num_programs(0) * n_steps
    g = pl.program_id(0) * n_steps + c
    slot = lax.rem(g, 2)
    nxt = lax.rem(g + 1, n_total)

    def page_copies(chunk, slot_, i):
        pg = pt_ref[chunk * pps + i]
        return (pltpu.make_async_copy(ckv_hbm.at[layer, pg], kbuf.at[slot_, i], sems.at[slot_, 0]),
                pltpu.make_async_copy(kpe_hbm.at[layer, pg], pbuf.at[slot_, i], sems.at[slot_, 1]))

    @pl.when(g == 0)
    def _():
        for i in range(pps):
            for cp in page_copies(0, 0, i):
                cp.start()

    for i in range(pps):
        for cp in page_copies(g, slot, i):
            cp.wait()

    @pl.when(c == 0)
    def _():
        m_sc[...] = jnp.full(m_sc.shape, -jnp.inf, F32)
        l_sc[...] = jnp.zeros(l_sc.shape, F32)
        acc_sc[...] = jnp.zeros(acc_sc.shape, F32)

    q = q_ref[...]
    qp = qpe_ref[:, :QK_ROPE]
    ks, ss = [], []
    for i in range(pps):
        for cp in page_copies(nxt, 1 - slot, i):
            cp.start()
        k = kbuf[slot, i].astype(BF16)
        kp_t = pbuf[slot, i].astype(BF16)
        ks.append(k)
        ss.append(_dot_nt(q, k) + _dot(qp, kp_t))
    s = jnp.concatenate(ss, axis=1)
    m_prev = m_sc[:, :1]
    l_prev = l_sc[:, :1]
    m_new = jnp.maximum(m_prev, jnp.max(s, axis=1, keepdims=True))
    alpha = jnp.exp2(m_prev - m_new)
    p = jnp.exp2(s - m_new)
    l_new = alpha * l_prev + jnp.sum(p, axis=1, keepdims=True)
    page = ks[0].shape[0]
    pv = _dot(p[:, :page].astype(BF16), ks[0])
    for i in range(1, pps):
        pv = pv + _dot(p[:, i * page:(i + 1) * page].astype(BF16), ks[i])
    acc_new = alpha * acc_sc[...] + pv
    m_sc[...] = jnp.broadcast_to(m_new, m_sc.shape)
    l_sc[...] = jnp.broadcast_to(l_new, l_sc.shape)
    acc_sc[...] = acc_new

    @pl.when(c == n_steps - 1)
    def _():
        qf = q.astype(F32)
        qpf = qp.astype(F32)
        kn = newc_ref[...]
        kpn = newk_ref[:, :QK_ROPE]
        row = lax.broadcasted_iota(jnp.int32, (q.shape[0], 1), 0)
        sj = []
        for j in range(n_new):
            v = (jnp.sum(qf * kn[j:j + 1, :], axis=1, keepdims=True)
                 + jnp.sum(qpf * kpn[j:j + 1, :], axis=1, keepdims=True))
            sj.append(jnp.where(row >= j * MLA_HEADS, v, -jnp.inf))
        m_fin = m_new
        for v in sj:
            m_fin = jnp.maximum(m_fin, v)
        a2 = jnp.exp2(m_new - m_fin)
        l_fin = a2 * l_new
        acc_fin = a2 * acc_new
        for j in range(n_new):
            pj = jnp.exp2(sj[j] - m_fin)
            l_fin = l_fin + pj
            acc_fin = acc_fin + pj * kn[j:j + 1, :]
        o_ref[...] = (acc_fin / l_fin).astype(o_ref.dtype)

    @pl.when(g == n_total - 1)
    def _():
        for i in range(pps):
            for cp in page_copies(nxt, 1 - slot, i):
                cp.wait()


def decode_attention(q_lat, q_pe, cache_ckv, cache_kpe, layer, page_table, new_ckv, new_kpe, *, pages_per_step):
    nb, rows, _ = q_lat.shape
    n_pages = page_table.shape[1]
    page = cache_ckv.shape[2]
    pps = pages_per_step
    n_steps = n_pages // pps
    n_new = rows // MLA_HEADS

    assert n_pages == n_steps * pps
    in_specs = [pl.BlockSpec((None, rows, KV_LORA), lambda b, c, pt: (b, 0, 0)),
                pl.BlockSpec((None, rows, LANES), lambda b, c, pt: (b, 0, 0)),
                pl.BlockSpec((None, 8, KV_LORA), lambda b, c, pt: (b, 0, 0)),
                pl.BlockSpec((None, 8, LANES), lambda b, c, pt: (b, 0, 0)),
                pl.BlockSpec(memory_space=pl.ANY), pl.BlockSpec(memory_space=pl.ANY)]
    grid_spec = pltpu.PrefetchScalarGridSpec(
        num_scalar_prefetch=1,
        grid=(nb, n_steps),
        in_specs=in_specs,
        out_specs=pl.BlockSpec((None, rows, KV_LORA), lambda b, c, pt: (b, 0, 0)),
        scratch_shapes=[pltpu.VMEM((2, pps, page, KV_LORA), F32), pltpu.VMEM((2, pps, QK_ROPE, page), F32),
                        pltpu.SemaphoreType.DMA((2, 2)),
                        pltpu.VMEM((rows, LANES), F32), pltpu.VMEM((rows, LANES), F32),
                        pltpu.VMEM((rows, KV_LORA), F32)],
    )
    return pl.pallas_call(
        functools.partial(_decode_body, layer=layer, pages_per_step=pps, n_steps=n_steps, n_new=n_new),
        grid_spec=grid_spec,
        out_shape=jax.ShapeDtypeStruct((nb, rows, KV_LORA), BF16),
        compiler_params=_cparams("arbitrary", "arbitrary"),
        name="decode_attention",
    )(page_table.reshape(-1), q_lat, q_pe, new_ckv, new_kpe, cache_ckv, cache_kpe)


def _group_rmsnorm(y, g):
    gw = SSM_INNER // SSM_GROUPS
    parts = []
    for i in range(SSM_GROUPS):
        yg = y[:, i * gw:(i + 1) * gw]
        parts.append(yg * lax.rsqrt(jnp.mean(yg * yg, axis=-1, keepdims=True) + EPS))
    return jnp.concatenate(parts, axis=1) * g


def _ssd_state_update(st, bm, xd, acs_x):
    last = acs_x.shape[0] - 1
    xde = (xd * jnp.exp(acs_x[last:last + 1, :] - acs_x)).astype(BF16)
    gw = SSM_INNER // SSM_GROUPS
    upd = [_dot_tn(bm[:, g * SSM_STATE:(g + 1) * SSM_STATE].astype(BF16), xde[:, g * gw:(g + 1) * gw])
           for g in range(SSM_GROUPS)]
    return st * jnp.exp(acs_x[last:last + 1, :]) + jnp.concatenate(upd, axis=1)


def _ssd_y_off(st, cm, acs_x):
    gw = SSM_INNER // SSM_GROUPS
    parts = [_dot(cm[:, g * SSM_STATE:(g + 1) * SSM_STATE].astype(BF16), st[:, g * gw:(g + 1) * gw].astype(BF16))
             for g in range(SSM_GROUPS)]
    return jnp.concatenate(parts, axis=1) * jnp.exp(acs_x)


def _ssd_prompt_body(z_ref, x_ref, bc_ref, dt_ref, cw_ref, cb_ref, dtb_ref, alog_ref, alogx_ref, dvec_ref, gn_ref,
                     e_ref, tril_ref, y_ref, st_ref, xp_sc, st_sc, *, n_chunks):
    L = SSM_CHUNK
    c = pl.program_id(1)

    @pl.when(c == 0)
    def _():
        xp_sc[0:8, :] = jnp.zeros((8, xp_sc.shape[1]), F32)
        st_sc[...] = jnp.zeros(st_sc.shape, F32)

    @pl.when(c > 0)
    def _():
        xp_sc[0:8, :] = xp_sc[L:L + 8, :]

    xp_sc[8:8 + L, 0:SSM_INNER] = x_ref[...]
    xp_sc[8:8 + L, SSM_INNER:] = bc_ref[...]
    conv = xp_sc[pl.ds(8 - (SSM_CONV - 1), L), :] * cw_ref[0:1, :]
    for k in range(1, SSM_CONV):
        conv = conv + xp_sc[pl.ds(8 - (SSM_CONV - 1) + k, L), :] * cw_ref[k:k + 1, :]
    xbc = _silu(conv + cb_ref[...])
    xs = xbc[:, :SSM_INNER]
    bm = xbc[:, SSM_INNER:SSM_INNER + SSM_GROUPS * SSM_STATE]
    cm = xbc[:, SSM_INNER + SSM_GROUPS * SSM_STATE:]

    hi = lax.Precision.HIGHEST
    dt = _softplus(dt_ref[...] + dtb_ref[...])
    tril = tril_ref[...]
    acs = _dot(tril, dt * (-jnp.exp(alog_ref[...])), precision=hi)
    acs_t = acs.T
    dt_x = _dot(dt, e_ref[...], precision=hi)
    acs_x = _dot(tril, dt_x * (-jnp.exp(alogx_ref[...])), precision=hi)
    xd = xs * dt_x
    xd_b = xd.astype(BF16)

    row = lax.broadcasted_iota(jnp.int32, (L, L), 0)
    col = lax.broadcasted_iota(jnp.int32, (L, L), 1)
    causal = row >= col
    lane = lax.broadcasted_iota(jnp.int32, (L, LANES), 1)
    heads_per_group = SSM_HEADS // SSM_GROUPS
    y_parts = []
    cb = [_dot_nt(cm[:, g * SSM_STATE:(g + 1) * SSM_STATE].astype(BF16),
                  bm[:, g * SSM_STATE:(g + 1) * SSM_STATE].astype(BF16)) for g in range(SSM_GROUPS)]
    for pair in range(SSM_HEADS // 2):
        xd_pair = xd_b[:, pair * LANES:(pair + 1) * LANES]
        outs = []
        for h in (2 * pair, 2 * pair + 1):
            decay = jnp.exp(jnp.where(causal, acs[:, h:h + 1] - acs_t[h:h + 1, :], -jnp.inf))
            outs.append(_dot((cb[h // heads_per_group] * decay).astype(BF16), xd_pair))
        y_parts.append(jnp.where(lane < SSM_HEAD_DIM, outs[0], outs[1]))
    y_diag = jnp.concatenate(y_parts, axis=1)

    st = st_sc[...]
    y = (y_diag + _ssd_y_off(st, cm, acs_x)) + dvec_ref[...] * xs
    y = y * _silu(z_ref[...])
    y_ref[...] = _group_rmsnorm(y, gn_ref[...]).astype(y_ref.dtype)
    st_new = _ssd_state_update(st, bm, xd, acs_x)
    st_sc[...] = st_new

    @pl.when(c == n_chunks - 1)
    def _():
        st_ref[...] = st_new.T


def _ssd_consts(conv_w, conv_b, dt_bias, a_log, d_vec, g_norm):
    pad = LANES - SSM_HEADS
    e_np = np.zeros((LANES, SSM_INNER), np.float32)
    for hh in range(SSM_HEADS):
        e_np[hh, hh * SSM_HEAD_DIM:(hh + 1) * SSM_HEAD_DIM] = 1.0
    e_mat = jnp.asarray(e_np)
    return dict(
        cw=conv_w, cb=conv_b.reshape(1, -1),
        dtb=jnp.pad(dt_bias, (0, pad)).reshape(1, LANES),
        alog=jnp.pad(a_log, (0, pad)).reshape(1, LANES),
        alogx=jnp.repeat(a_log, SSM_HEAD_DIM).reshape(1, SSM_INNER),
        dvec=jnp.repeat(d_vec, SSM_HEAD_DIM).reshape(1, SSM_INNER),
        gn=g_norm.reshape(1, SSM_INNER), e=e_mat)


def _full(shape):
    nd = len(shape)
    return pl.BlockSpec(shape, lambda *_: (0,) * nd)


def ssd_prompt(proj, slabs, consts, *, batch, seq):
    L = SSM_CHUNK
    nc = seq // L
    cdim = SSM_INNER + 2 * SSM_GROUPS * SSM_STATE
    tril = jnp.asarray(np.tril(np.ones((L, L), np.float32)))

    def rows(width, col):
        return pl.BlockSpec((L, width), lambda b, c: (b * nc + c, col // width))

    k = consts
    return pl.pallas_call(
        functools.partial(_ssd_prompt_body, n_chunks=nc),
        grid=(batch, nc),
        in_specs=[rows(SSM_INNER, COL_Z), rows(SSM_INNER, COL_X), rows(2 * SSM_GROUPS * SSM_STATE, COL_BC),
                  pl.BlockSpec((L, LANES), lambda b, c: (b * nc + c, SLAB_DT)),
                  _full((SSM_CONV, cdim)), _full((1, cdim)), _full((1, LANES)), _full((1, LANES)),
                  _full((1, SSM_INNER)), _full((1, SSM_INNER)), _full((1, SSM_INNER)),
                  _full((LANES, SSM_INNER)), _full((L, L))],
        out_specs=[pl.BlockSpec((L, SSM_INNER), lambda b, c: (b * nc + c, 0)),
                   pl.BlockSpec((None, SSM_INNER, SSM_STATE), lambda b, c: (b, 0, 0))],
        out_shape=[jax.ShapeDtypeStruct((batch * seq, SSM_INNER), BF16),
                   jax.ShapeDtypeStruct((batch, SSM_INNER, SSM_STATE), F32)],
        scratch_shapes=[pltpu.VMEM((L + 8, cdim), F32), pltpu.VMEM((SSM_STATE, SSM_INNER), F32)],
        compiler_params=_cparams("parallel", "arbitrary"),
        name="ssd_prompt",
    )(proj, proj, proj, slabs, k["cw"], k["cb"], k["dtb"], k["alog"], k["alogx"], k["dvec"], k["gn"], k["e"], tril)


def _sample_mixer_body(z_ref, dt_ref, scb_ref, xp_ref, up_c_ref, up_v_ref, st_in_ref,
                       cw_ref, cb_ref, dtb_ref, alogx_ref, dvec_ref, gn_ref, e_ref, scw_ref,
                       y_ref, ysc_ref, st_ref, u_out_ref, bpad_sc, xdpad_sc, u_sc, *, n_new, group):
    @pl.when(pl.program_id(0) == 0)
    def _():
        bpad_sc[...] = jnp.zeros(bpad_sc.shape, F32)
        xdpad_sc[...] = jnp.zeros(xdpad_sc.shape, F32)

    for i in range(group):
        _sample_mixer_one(z_ref.at[i], dt_ref.at[i], scb_ref.at[i], xp_ref.at[i], up_c_ref.at[i], up_v_ref.at[i],
                          st_in_ref.at[i], cw_ref, cb_ref, dtb_ref, alogx_ref, dvec_ref, gn_ref, e_ref, scw_ref,
                          y_ref.at[i], ysc_ref.at[i], st_ref.at[i], u_out_ref.at[i],
                          bpad_sc.at[i], xdpad_sc.at[i], u_sc.at[i], n_new=n_new)


def _sample_mixer_one(z_ref, dt_ref, scb_ref, xp_ref, up_c_ref, up_v_ref, st_in_ref,
                      cw_ref, cb_ref, dtb_ref, alogx_ref, dvec_ref, gn_ref, e_ref, scw_ref,
                      y_ref, ysc_ref, st_ref, u_out_ref, bpad_sc, xdpad_sc, u_sc, *, n_new):
    R = 8
    conv = xp_ref[pl.ds(8 - (SSM_CONV - 1), R), :] * cw_ref[0:1, :]
    for k in range(1, SSM_CONV):
        conv = conv + xp_ref[pl.ds(8 - (SSM_CONV - 1) + k, R), :] * cw_ref[k:k + 1, :]
    xbc = _silu(conv + cb_ref[...])
    xs = xbc[:, :SSM_INNER]
    bm = xbc[:, SSM_INNER:SSM_INNER + SSM_GROUPS * SSM_STATE]
    cm = xbc[:, SSM_INNER + SSM_GROUPS * SSM_STATE:]

    hi = lax.Precision.HIGHEST
    rowl = lax.broadcasted_iota(jnp.int32, (R, LANES), 0)
    dt = jnp.where(rowl < n_new, _softplus(dt_ref[...] + dtb_ref[...]), 0.0)
    dt_x = _dot(dt, e_ref[...], precision=hi)
    da_x = dt_x * (-jnp.exp(alogx_ref[...]))
    rowx = lax.broadcasted_iota(jnp.int32, (R, SSM_INNER), 0)
    acs_x = jnp.zeros((R, SSM_INNER), F32)
    for s in range(n_new):
        acs_x = acs_x + jnp.where(rowx >= s, da_x[s:s + 1, :], 0.0)
    xd = xs * dt_x

    gw = SSM_INNER // SSM_GROUPS
    lanex = lax.broadcasted_iota(jnp.int32, (R, SSM_INNER), 1)
    y_diag = jnp.zeros((R, SSM_INNER), F32)
    for s in range(n_new):
        cbs = [jnp.sum(cm[:, g * SSM_STATE:(g + 1) * SSM_STATE] * bm[s:s + 1, g * SSM_STATE:(g + 1) * SSM_STATE],
                       axis=1, keepdims=True) for g in range(SSM_GROUPS)]
        cb_x = jnp.where(lanex < gw, cbs[0], cbs[1])
        decay = jnp.exp(jnp.where(rowx >= s, acs_x - acs_x[s:s + 1, :], -jnp.inf))
        y_diag = y_diag + (cb_x * decay) * xd[s:s + 1, :]

    st = st_in_ref[...].T
    y_off_parts = [_dot(cm[:, g * SSM_STATE:(g + 1) * SSM_STATE], st[:, g * gw:(g + 1) * gw])
                   for g in range(SSM_GROUPS)]
    y_off = jnp.concatenate(y_off_parts, axis=1) * jnp.exp(acs_x)
    y = (y_diag + y_off) + dvec_ref[...] * xs
    y = y * _silu(z_ref[...])
    y_ref[...] = _group_rmsnorm(y, gn_ref[...])

    bpad_sc[0:R, :] = bm
    xdpad_sc[0:R, :] = xd * jnp.exp(acs_x[R - 1:R, :] - acs_x)
    upd = [_dot_tn(bpad_sc[:, g * SSM_STATE:(g + 1) * SSM_STATE], xdpad_sc[:, g * gw:(g + 1) * gw])
           for g in range(SSM_GROUPS)]
    st_new = st * jnp.exp(acs_x[R - 1:R, :]) + jnp.concatenate(upd, axis=1)
    st_ref[...] = st_new.T

    u_sc[...] = up_c_ref[...] * up_v_ref[...]
    sconv = u_sc[pl.ds(8 - (SC_CONV - 1), R), :] * scw_ref[0:1, :]
    for k in range(1, SC_CONV):
        sconv = sconv + u_sc[pl.ds(8 - (SC_CONV - 1) + k, R), :] * scw_ref[k:k + 1, :]
    ysc_ref[...] = scb_ref[...] * sconv
    u_out_ref[...] = u_sc[8:16, :]


def sample_mixer(proj_s, slabs_s, xp_s, up_c, up_v, st_all, layer, consts, sc_conv_w, *, n_new, group):
    nb = proj_s.shape[0]
    cdim = SSM_INNER + 2 * SSM_GROUPS * SSM_STATE
    k = consts
    gs = group

    def rows(width, col):
        return pl.BlockSpec((gs, 8, width), lambda b: (b, 0, col // width))

    def per_sample(*tail):
        return pl.BlockSpec((gs,) + tail, lambda b: (b,) + (0,) * len(tail))

    st_spec = pl.BlockSpec((None, gs, SSM_INNER, SSM_STATE), lambda b: (layer, b, 0, 0))
    args = [proj_s, slabs_s, proj_s, xp_s, up_c, up_v, st_all,
            k["cw"], k["cb"], k["dtb"], k["alogx"], k["dvec"], k["gn"], k["e"], sc_conv_w]
    in_specs = [rows(SSM_INNER, COL_Z), pl.BlockSpec((gs, 8, LANES), lambda b: (b, 0, SLAB_DT)),
                rows(SC_WIDTH, COL_SCB),
                per_sample(16, cdim), per_sample(16, SC_WIDTH), per_sample(16, SC_WIDTH), st_spec,
                _full((SSM_CONV, cdim)), _full((1, cdim)), _full((1, LANES)),
                _full((1, SSM_INNER)), _full((1, SSM_INNER)), _full((1, SSM_INNER)),
                _full((LANES, SSM_INNER)), _full((SC_CONV, SC_WIDTH))]
    return pl.pallas_call(
        functools.partial(_sample_mixer_body, n_new=n_new, group=gs),
        grid=(nb // gs,),
        in_specs=in_specs,
        out_specs=[per_sample(8, SSM_INNER), per_sample(8, SC_WIDTH), per_sample(SSM_INNER, SSM_STATE),
                   per_sample(8, SC_WIDTH)],
        out_shape=[jax.ShapeDtypeStruct((nb, 8, SSM_INNER), F32), jax.ShapeDtypeStruct((nb, 8, SC_WIDTH), F32),
                   jax.ShapeDtypeStruct((nb, SSM_INNER, SSM_STATE), F32),
                   jax.ShapeDtypeStruct((nb, 8, SC_WIDTH), F32)],
        scratch_shapes=[pltpu.VMEM((gs, LANES, SSM_GROUPS * SSM_STATE), F32), pltpu.VMEM((gs, LANES, SSM_INNER), F32),
                        pltpu.VMEM((gs, 16, SC_WIDTH), F32)],
        compiler_params=_cparams("arbitrary"),
        name="sample_mixer",
    )(*args)


def _sconv_prompt_body(scb_ref, scc_ref, scv_ref, pc_ref, pv_ref, w_ref, y_ref, tail_ref, u_sc, *, ts):
    i = pl.program_id(1)
    hist = pc_ref[...] * pv_ref[...]
    u_sc[0:8, :] = jnp.where(i == 0, 0.0, hist)
    u_sc[8:8 + ts, :] = scc_ref[...] * scv_ref[...]
    conv = u_sc[pl.ds(8 - (SC_CONV - 1), ts), :] * w_ref[0:1, :]
    for k in range(1, SC_CONV):
        conv = conv + u_sc[pl.ds(8 - (SC_CONV - 1) + k, ts), :] * w_ref[k:k + 1, :]
    y_ref[...] = (scb_ref[...] * conv).astype(y_ref.dtype)
    tail_ref[...] = u_sc[ts:ts + 8, :]


def sconv_prompt(proj, sc_conv_w, *, batch, seq, ts):
    nt = seq // ts
    w = SC_WIDTH

    def rows(col):
        return pl.BlockSpec((ts, w), lambda b, i: (b * nt + i, col // w))

    def prev(col):
        return pl.BlockSpec((8, w), lambda b, i: (jnp.maximum((b * nt + i) * (ts // 8) - 1, 0), col // w))

    return pl.pallas_call(
        functools.partial(_sconv_prompt_body, ts=ts),
        grid=(batch, nt),
        in_specs=[rows(COL_SCB), rows(COL_SCC), rows(COL_SCV), prev(COL_SCC), prev(COL_SCV), _full((SC_CONV, w))],
        out_specs=[pl.BlockSpec((ts, w), lambda b, i: (b * nt + i, 0)),
                   pl.BlockSpec((None, 8, w), lambda b, i: (b, 0, 0))],
        out_shape=[jax.ShapeDtypeStruct((batch * seq, w), BF16), jax.ShapeDtypeStruct((batch, 8, w), F32)],
        scratch_shapes=[pltpu.VMEM((ts + 8, w), F32)],
        compiler_params=_cparams("parallel", "arbitrary"),
        name="sconv_prompt",
    )(proj, proj, proj, proj, proj, sc_conv_w)


def _merge_body(h_ref, yap_ref, ysp_ref, ycp_ref, yas_ref, yss_ref, ycs_ref,
                wga_ref, wgb_ref, wgc_ref, ba_ref, bb_ref, bc_ref, wa_ref, wb_ref, wc_ref, o_ref,
                wg_sc, wb_sc, *, n_prompt_tiles):
    i = pl.program_id(1)

    @pl.when(i == 0)
    def _():
        for n, r in enumerate((wga_ref, wgb_ref, wgc_ref)):
            wg_sc[n] = r[...].astype(BF16)
        for n, r in enumerate((wa_ref, wb_ref, wc_ref)):
            wb_sc[n] = r[...].astype(BF16)

    h = h_ref[...]
    is_sample = i >= n_prompt_tiles
    out = None
    for n, (p_ref, s_ref, b_ref) in enumerate(((yap_ref, yas_ref, ba_ref), (ysp_ref, yss_ref, bb_ref),
                                               (ycp_ref, ycs_ref, bc_ref))):
        y = jnp.where(is_sample, s_ref[...], p_ref[...])
        term = _sigmoid(_dot(h, wg_sc[n]) + b_ref[...]) * _dot(y, wb_sc[n])
        out = term if out is None else out + term
    o_ref[...] = out.astype(o_ref.dtype)


def gated_merge(h, y_prompt, y_sample, w_gate, b_gate, w_branches, layer, *, tm, tn):
    m, d = h.shape
    mp, kb = y_prompt[0].shape
    nb = d // tn
    npt = mp // tm
    b2 = b_gate.reshape(b_gate.shape[0], 1, -1)
    hspec = pl.BlockSpec((tm, d), lambda j, i: (i, 0))
    pspec = pl.BlockSpec((tm, kb), lambda j, i: (jnp.minimum(i, npt - 1), 0))
    once = pl.Buffered(1)
    sspec = pl.BlockSpec((tm, kb), lambda j, i: (jnp.maximum(i - npt, 0), 0), pipeline_mode=once)

    def wcol(rows_, off):
        return pl.BlockSpec((None, rows_, tn), lambda j, i: (layer, 0, off * nb + j), pipeline_mode=once)

    return pl.pallas_call(
        functools.partial(_merge_body, n_prompt_tiles=npt),
        grid=(nb, m // tm),
        in_specs=[hspec, pspec, pspec, pspec, sspec, sspec, sspec,
                  wcol(d, 0), wcol(d, 1), wcol(d, 2), wcol(1, 0), wcol(1, 1), wcol(1, 2),
                  wcol(kb, 0), wcol(kb, 0), wcol(kb, 0)],
        out_specs=pl.BlockSpec((tm, tn), lambda j, i: (i, j)),
        out_shape=jax.ShapeDtypeStruct((m, d), BF16),
        scratch_shapes=[pltpu.VMEM((3, d, tn), BF16), pltpu.VMEM((3, kb, tn), BF16)],
        compiler_params=_cparams("parallel", "arbitrary"),
        name="gated_merge",
    )(h, *y_prompt, *y_sample, w_gate, w_gate, w_gate, b2, b2, b2, *w_branches)


def _rope_tables(pos):
    half = QK_ROPE // 2
    inv = ROPE_THETA ** (-jnp.arange(half, dtype=F32) / half)
    ang = pos.astype(F32)[:, None] * inv[None, :]
    c, s = jnp.cos(ang), jnp.sin(ang)
    z = jnp.zeros((pos.shape[0], LANES - QK_ROPE), F32)
    return jnp.concatenate([c, c, z], axis=1), jnp.concatenate([-s, s, z], axis=1)


def _in_proj_row_offsets(layer):
    sizes = (Q_LORA, KV_LORA, QK_ROPE, SSM_INNER, SSM_INNER + 2 * SSM_GROUPS * SSM_STATE, SSM_HEADS,
             SC_WIDTH, SC_WIDTH, SC_WIDTH)
    q_c, kv_c, k_pe, z, xbc, dt, sc_b, sc_c, sc_v = (int(v) for v in np.cumsum((0,) + sizes)[:-1])
    d_in = int(sum(sizes))
    main = []
    for first, width in ((sc_b, SC_WIDTH), (sc_c, SC_WIDTH), (sc_v, SC_WIDTH), (z, SSM_INNER),
                         (xbc, SSM_INNER + 2 * SSM_GROUPS * SSM_STATE), (q_c, Q_LORA), (kv_c, KV_LORA)):
        main += [first + t for t in range(0, width, PROJ_TILE)]
    assert len(main) * PROJ_TILE == D_IN_PAD
    return tuple(layer * d_in + r for r in main), tuple(layer * d_in + r for r in (k_pe, dt))


def kernel(x_prompt, x_sample, cache_ckv, cache_kpe, state_ssm, state_mconv, state_sconv, page_table, g_attn_norm, w_in, g_q_a, w_q_b, g_kv_a, w_kv_b, ssm_conv_w, ssm_conv_b, ssm_dt_bias, ssm_a_log, ssm_d, g_ssm_norm, sc_conv_w, w_gate, b_gate, w_br_attn, w_br_ssm, w_br_sc, w_o, g_ffn_norm, w_ff_gate, w_ff_up, w_ff_down, w_router, w_e_gate, w_e_up, w_e_down, g_final):
    bp, tp, d = x_prompt.shape
    bs, ts, _ = x_sample.shape
    depth = w_in.shape[0]
    mp, ms = bp * tp, bs * ts
    m = mp + ms
    n_past = page_table.shape[1] * cache_ckv.shape[2]
    tm = m // 8
    tmh = m // 16
    tmw = m // 4
    cdim = SSM_INNER + 2 * SSM_GROUPS * SSM_STATE

    x = jnp.concatenate([x_prompt.reshape(mp, d), x_sample.reshape(ms, d)], axis=0)
    pos = jnp.concatenate([jnp.tile(jnp.arange(tp), bp), jnp.tile(n_past + jnp.arange(ts), bs)])
    cos, sin = _rope_tables(pos)
    cache_kpe_t = jnp.swapaxes(cache_kpe, 2, 3)
    st_all = state_ssm.reshape(depth, bs, SSM_INNER, SSM_STATE)
    w_in_t = jnp.swapaxes(w_in, 1, 2).reshape(-1, d)

    outs = {k: [] for k in ("p_ckv", "p_kpe", "p_ssm", "p_mconv", "p_sconv", "s_ckv", "s_kpe", "s_ssm", "s_mconv", "s_sconv")}
    for l in range(depth):
        wq = w_q_b[l]
        w_q_slots = jnp.concatenate([wq, jnp.zeros(wq.shape[:2] + (Q_SLOT - wq.shape[2],), wq.dtype)], axis=2)
        w_q_slots = w_q_slots.reshape(Q_LORA, MLA_HEADS * Q_SLOT).astype(BF16)
        wkv = w_kv_b[l]
        w_kv_flat = jnp.concatenate([wkv[..., :QK_NOPE].reshape(KV_LORA, -1), wkv[..., QK_NOPE:].reshape(KV_LORA, -1)],
                                    axis=1).astype(BF16)
        w_uk_t = jnp.transpose(wkv[..., :QK_NOPE], (1, 2, 0)).astype(BF16)
        w_uv = jnp.transpose(wkv[..., QK_NOPE:], (1, 0, 2)).astype(BF16)
        consts = _ssd_consts(ssm_conv_w[l], ssm_conv_b[l], ssm_dt_bias[l], ssm_a_log[l], ssm_d[l], g_ssm_norm[l])

        h = rmsnorm(x, g_attn_norm[l], BF16, tm)
        rows_main, rows_slabs = _in_proj_row_offsets(l)
        proj = in_proj(h, w_in_t, rows_main, tm=tmw, tn=PROJ_TILE)
        slabs = in_proj(h, w_in_t, rows_slabs, tm=tmw, tn=LANES)
        qn, ckv, kpe = mla_prep(proj, slabs, g_q_a[l], g_kv_a[l], cos, sin, tm=tm)
        q = qproj(qn, w_q_slots, cos, sin, tm=tm)

        k_full, v_full = kv_expand(ckv, kpe, w_kv_flat, rows=mp, tm=1024)
        ya_p = flash_attention(q, k_full, v_full, batch=bp, seq=tp, blk=1024, heads=2)

        q_lat = blockdiag_matmul(q, w_uk_t, tm=ms, row_block=mp // ms, col_block0=0, col_stride=2)
        q_pe_s = q[mp:].reshape(ms, MLA_HEADS, Q_SLOT)[:, :, QK_NOPE:].reshape(bs, ts * MLA_HEADS, LANES)
        ckv_s = ckv[mp:].reshape(bs, ts, KV_LORA)
        kpe_s = kpe[mp:].reshape(bs, ts, LANES)
        new_c = jnp.pad(ckv_s, ((0, 0), (0, 8 - ts), (0, 0)))
        new_k = jnp.pad(kpe_s, ((0, 0), (0, 8 - ts), (0, 0)))
        o_lat = decode_attention(q_lat.reshape(bs, ts * MLA_HEADS, KV_LORA), q_pe_s, cache_ckv, cache_kpe_t, l,
                                 page_table, new_c, new_k, pages_per_step=32)
        ya_s = blockdiag_matmul(o_lat.reshape(ms, MLA_HEADS * KV_LORA), w_uv, tm=ms, row_block=0, col_block0=0,
                                col_stride=1)

        ys_p, ssm_p = ssd_prompt(proj, slabs, consts, batch=bp, seq=tp)
        yc_p, u_tail_p = sconv_prompt(proj, sc_conv_w[l], batch=bp, seq=tp, ts=512)

        proj_s = jnp.pad(proj[mp:].reshape(bs, ts, D_IN_PAD), ((0, 0), (0, 8 - ts), (0, 0)))
        slabs_s = jnp.pad(slabs[mp:].reshape(bs, ts, D_IN_SLABS), ((0, 0), (0, 8 - ts), (0, 0)))
        xbc_s = proj_s[:, :ts, COL_X:COL_X + cdim]
        xp_s = jnp.concatenate([jnp.zeros((bs, 8 - (SSM_CONV - 1), cdim), F32), state_mconv[l], xbc_s,
                                jnp.zeros((bs, 8 - ts, cdim), F32)], axis=1)
        zpad = jnp.zeros((bs, 8 - (SC_CONV - 1), SC_WIDTH), F32)
        zend = jnp.zeros((bs, 8 - ts, SC_WIDTH), F32)
        up_c = jnp.concatenate([zpad, state_sconv[l], proj_s[:, :ts, COL_SCC:COL_SCC + SC_WIDTH], zend], axis=1)
        up_v = jnp.concatenate([zpad, jnp.ones_like(state_sconv[l]), proj_s[:, :ts, COL_SCV:COL_SCV + SC_WIDTH], zend], axis=1)
        ys_s, yc_s, ssm_s, u_new_s = sample_mixer(proj_s, slabs_s, xp_s, up_c, up_v, st_all, l, consts, sc_conv_w[l],
                                                  n_new=ts, group=4)

        y_sample = (ya_s, ys_s[:, :ts].reshape(ms, SSM_INNER).astype(BF16), yc_s[:, :ts].reshape(ms, SC_WIDTH).astype(BF16))
        merged = gated_merge(h, (ya_p, ys_p, yc_p), y_sample, w_gate, b_gate, (w_br_attn, w_br_ssm, w_br_sc), l,
                             tm=ms, tn=512)
        x = matmul(merged, w_o, tm=tm, tn=512, res=x, w_index=l)

        outs["p_ckv"].append(ckv[:mp].reshape(bp, tp, KV_LORA))
        outs["p_kpe"].append(kpe[:mp, :QK_ROPE].reshape(bp, tp, QK_ROPE))
        outs["p_ssm"].append(ssm_p.reshape(bp, SSM_HEADS, SSM_HEAD_DIM, SSM_STATE))
        outs["p_mconv"].append(jnp.stack([proj[(b + 1) * tp - (SSM_CONV - 1):(b + 1) * tp, COL_X:COL_X + cdim]
                                          for b in range(bp)]))
        outs["p_sconv"].append(u_tail_p[:, 8 - (SC_CONV - 1):])
        outs["s_ckv"].append(ckv_s)
        outs["s_kpe"].append(kpe_s[:, :, :QK_ROPE])
        outs["s_ssm"].append(ssm_s.reshape(bs, SSM_HEADS, SSM_HEAD_DIM, SSM_STATE))
        outs["s_mconv"].append(xp_s[:, 8 + ts - (SSM_CONV - 1):8 + ts])
        outs["s_sconv"].append(u_new_s[:, ts - (SC_CONV - 1):ts])

        i = l // 2
        if l % 2 == 0:
            h2 = rmsnorm(x, g_ffn_norm[l], BF16, tm)
            hdn = swiglu_up(h2, w_ff_gate[i], w_ff_up[i], tm=tmw, tf=512)
            x = matmul(hdn, w_ff_down[i].astype(BF16), tm=tm, tn=256, res=x, rows_outer=True)
        else:
            h2, gate = rmsnorm_router(x, g_ffn_norm[l], w_router[i], tm)
            hdn = moe_up(h2, w_e_gate[i].astype(BF16), w_e_up[i].astype(BF16), gate, tm=tm)
            wd = w_e_down[i].reshape(-1, d).astype(BF16)
            x = matmul(hdn, wd, tm=tmh, tn=256, res=x, rows_outer=True)

    y_p, y_s = rmsnorm_split(x, g_final, rows_prompt=mp, tm=ms)
    st = {k: jnp.stack(v, axis=0) for k, v in outs.items()}
    return (y_p.reshape(bp, tp, d), y_s.reshape(bs, ts, d),
            st["p_ckv"], st["p_kpe"], st["p_ssm"], st["p_mconv"], st["p_sconv"],
            st["s_ckv"], st["s_kpe"], st["s_ssm"], st["s_mconv"], st["s_sconv"])
```

```python
import functools

import jax
import jax.numpy as jnp
import numpy as np
from jax import lax
from jax.experimental import pallas as pl
from jax.experimental.pallas import tpu as pltpu

F32 = jnp.float32
BF16 = jnp.bfloat16
EPS = 1e-6
ROPE_THETA = 10000.0
LANES = 128
SUBLANES_BF16 = 16
MLA_HEADS = 8
QK_NOPE = 128
QK_ROPE = 64
V_HEAD = 128
KV_LORA = 512
Q_LORA = 512
Q_SLOT = 256
SSM_HEADS = 16
SSM_HEAD_DIM = 64
SSM_INNER = 1024
SSM_GROUPS = 2
SSM_STATE = 128
SSM_CONV = 4
SSM_CHUNK = 128
SC_WIDTH = 1024
SC_CONV = 3
N_EXPERTS = 8
ATTN_SCALE = (QK_NOPE + QK_ROPE) ** -0.5
Q_SCALE = ATTN_SCALE * float(np.log2(np.e))
VMEM_LIMIT = 56 * 1024 * 1024

COL_SCB, COL_SCC, COL_SCV, COL_Z, COL_X, COL_BC, COL_QC, COL_KVC = (
    0, 1024, 2048, 3072, 4096, 5120, 5632, 6144)
PROJ_TILE = 512
D_IN_PAD = 6656
SLAB_KPE, SLAB_DT = 0, 1
D_IN_SLABS = 2 * LANES


def _cparams(*sem):
    return pltpu.CompilerParams(dimension_semantics=sem, vmem_limit_bytes=VMEM_LIMIT)


def _sigmoid(x):
    return 1.0 / (1.0 + jnp.exp(-x))


def _silu(x):
    return x * _sigmoid(x)


def _softplus(x):
    return jnp.maximum(x, 0.0) + jnp.log1p(jnp.exp(-jnp.abs(x)))


def _dot(a, b, **kw):
    return jnp.dot(a, b, preferred_element_type=F32, **kw)


def _dot_nt(a, b):
    return lax.dot_general(a, b, (((1,), (1,)), ((), ())), preferred_element_type=F32)


def _dot_tn(a, b):
    return lax.dot_general(a, b, (((0,), (0,)), ((), ())), preferred_element_type=F32)


def _rms(x, g):
    r = lax.rsqrt(jnp.mean(x * x, axis=-1, keepdims=True) + EPS)
    return (x * r) * g


def _rmsnorm_body(x_ref, g_ref, o_ref):
    o_ref[...] = _rms(x_ref[...], g_ref[...]).astype(o_ref.dtype)


def rmsnorm(x, g, out_dtype, tm):
    m, d = x.shape
    return pl.pallas_call(
        _rmsnorm_body,
        grid=(m // tm,),
        in_specs=[pl.BlockSpec((tm, d), lambda i: (i, 0)), pl.BlockSpec((1, d), lambda i: (0, 0))],
        out_specs=pl.BlockSpec((tm, d), lambda i: (i, 0)),
        out_shape=jax.ShapeDtypeStruct((m, d), out_dtype),
        compiler_params=_cparams("parallel"),
        name="rmsnorm",
    )(x, g.reshape(1, d))


def _rmsnorm_split_body(x_ref, g_ref, op_ref, os_ref, *, n_prompt_tiles):
    i = pl.program_id(0)
    y = _rms(x_ref[...], g_ref[...])

    @pl.when(i < n_prompt_tiles)
    def _():
        op_ref[...] = y

    @pl.when(i >= n_prompt_tiles)
    def _():
        os_ref[...] = y


def rmsnorm_split(x, g, *, rows_prompt, tm):
    m, d = x.shape
    npt = rows_prompt // tm
    return pl.pallas_call(
        functools.partial(_rmsnorm_split_body, n_prompt_tiles=npt),
        grid=(m // tm,),
        in_specs=[pl.BlockSpec((tm, d), lambda i: (i, 0)), pl.BlockSpec((1, d), lambda i: (0, 0))],
        out_specs=[pl.BlockSpec((tm, d), lambda i: (jnp.minimum(i, npt - 1), 0)),
                   pl.BlockSpec((tm, d), lambda i: (jnp.maximum(i - npt, 0), 0))],
        out_shape=[jax.ShapeDtypeStruct((rows_prompt, d), F32), jax.ShapeDtypeStruct((m - rows_prompt, d), F32)],
        compiler_params=_cparams("arbitrary"),
        name="rmsnorm_split",
    )(x, g.reshape(1, d))


def _rmsnorm_router_body(x_ref, g_ref, wr_ref, h_ref, gate_ref):
    h = _rms(x_ref[...], g_ref[...])
    h_ref[...] = h.astype(h_ref.dtype)
    lg = _dot(h, wr_ref[...], precision=lax.Precision.HIGHEST)
    lane = lax.broadcasted_iota(jnp.int32, lg.shape, 1).astype(F32)
    lg = jnp.where(lane < N_EXPERTS, lg, -jnp.inf)
    m1 = jnp.max(lg, axis=1, keepdims=True)
    i1 = jnp.min(jnp.where(lg == m1, lane, float(LANES)), axis=1, keepdims=True)
    oh1 = lane == i1
    lg2 = jnp.where(oh1, -jnp.inf, lg)
    m2 = jnp.max(lg2, axis=1, keepdims=True)
    i2 = jnp.min(jnp.where(lg2 == m2, lane, float(LANES)), axis=1, keepdims=True)
    oh2 = lane == i2
    e = jnp.exp(m2 - m1)
    w1 = 1.0 / (1.0 + e)
    w2 = e / (1.0 + e)
    gate_ref[...] = jnp.where(oh1, w1, 0.0) + jnp.where(oh2, w2, 0.0)


def rmsnorm_router(x, g, w_router, tm):
    m, d = x.shape
    wr = jnp.pad(w_router, ((0, 0), (0, LANES - w_router.shape[1])))
    return pl.pallas_call(
        _rmsnorm_router_body,
        grid=(m // tm,),
        in_specs=[pl.BlockSpec((tm, d), lambda i: (i, 0)), pl.BlockSpec((1, d), lambda i: (0, 0)),
                  pl.BlockSpec((d, LANES), lambda i: (0, 0))],
        out_specs=[pl.BlockSpec((tm, d), lambda i: (i, 0)), pl.BlockSpec((tm, LANES), lambda i: (i, 0))],
        out_shape=[jax.ShapeDtypeStruct((m, d), BF16), jax.ShapeDtypeStruct((m, LANES), F32)],
        compiler_params=_cparams("parallel"),
        name="rmsnorm_router",
    )(x, g.reshape(1, d), wr)


def _cast_once(src_refs, dst_refs):
    @pl.when(pl.program_id(1) == 0)
    def _():
        for s, d in zip(src_refs, dst_refs):
            d[...] = s[...].astype(d.dtype)


def _mm_body(x_ref, w_ref, *rest, has_res, cast_w):
    rest = list(rest)
    r_ref = rest.pop(0) if has_res else None
    o_ref = rest.pop(0)
    if cast_w:
        (w_sc,) = rest
        _cast_once([w_ref], [w_sc])
        w_ref = w_sc
    acc = _dot(x_ref[...], w_ref[...])
    o_ref[...] = ((r_ref[...] + acc) if has_res else acc).astype(o_ref.dtype)


def matmul(x, w, *, tm, tn, res=None, out_dtype=F32, rows_outer=False, w_index=None):
    m, kd = x.shape
    n = w.shape[-1]
    cast_w = w.dtype != BF16
    assert not (cast_w and rows_outer)

    def ij(a, b):
        return (a, b) if rows_outer else (b, a)

    if w_index is None:
        w_spec = pl.BlockSpec((kd, tn), lambda a, b: (0, ij(a, b)[1]))
    else:
        w_spec = pl.BlockSpec((None, kd, tn), lambda a, b: (w_index, 0, ij(a, b)[1]))
    in_specs = [pl.BlockSpec((tm, kd), lambda a, b: (ij(a, b)[0], 0)), w_spec]
    args = [x, w]
    if res is not None:
        in_specs.append(pl.BlockSpec((tm, tn), lambda a, b: ij(a, b)))
        args.append(res)
    return pl.pallas_call(
        functools.partial(_mm_body, has_res=res is not None, cast_w=cast_w),
        grid=(m // tm, n // tn) if rows_outer else (n // tn, m // tm),
        in_specs=in_specs,
        out_specs=pl.BlockSpec((tm, tn), lambda a, b: ij(a, b)),
        out_shape=jax.ShapeDtypeStruct((m, n), out_dtype),
        scratch_shapes=[pltpu.VMEM((kd, tn), BF16)] if cast_w else [],
        compiler_params=_cparams("parallel", "arbitrary"),
        name="matmul",
    )(*args)


def _in_proj_body(offs_ref, x_ref, wt_ref, o_ref, w_sc):
    del offs_ref

    @pl.when(pl.program_id(1) == 0)
    def _():
        w_sc[...] = wt_ref[...].T.astype(w_sc.dtype)

    o_ref[...] = _dot(x_ref[...], w_sc[...])


def in_proj(x, w_t, row_offsets, *, tm, tn):
    m, kd = x.shape
    nt = len(row_offsets)
    assert all(o % SUBLANES_BF16 == 0 for o in row_offsets)
    grid_spec = pltpu.PrefetchScalarGridSpec(
        num_scalar_prefetch=1,
        grid=(nt, m // tm),
        in_specs=[pl.BlockSpec((tm, kd), lambda j, i, offs: (i, 0)),
                  pl.BlockSpec((pl.Element(tn), pl.Element(kd)),
                               lambda j, i, offs: (pl.multiple_of(offs[j], SUBLANES_BF16), 0))],
        out_specs=pl.BlockSpec((tm, tn), lambda j, i, offs: (i, j)),
        scratch_shapes=[pltpu.VMEM((kd, tn), BF16)],
    )
    return pl.pallas_call(
        _in_proj_body,
        grid_spec=grid_spec,
        out_shape=jax.ShapeDtypeStruct((m, nt * tn), F32),
        compiler_params=_cparams("parallel", "arbitrary"),
        name="in_proj",
    )(jnp.asarray(row_offsets, jnp.int32), x, w_t)


def _swiglu_body(x_ref, wg_ref, wu_ref, *rest, scaled, cast_w):
    rest = list(rest)
    gate_ref = rest.pop(0) if scaled else None
    o_ref = rest.pop(0)
    if cast_w:
        _cast_once([wg_ref, wu_ref], rest)
        wg_ref, wu_ref = rest
    x = x_ref[...]
    g = _dot(x, wg_ref[...])
    u = _dot(x, wu_ref[...])
    hdn = _silu(g) * u
    if scaled:
        gate = gate_ref[...]
        lane = lax.broadcasted_iota(jnp.int32, gate.shape, 1)
        sc = jnp.sum(jnp.where(lane == pl.program_id(0), gate, 0.0), axis=1, keepdims=True)
        hdn = hdn * sc
    o_ref[...] = hdn.astype(o_ref.dtype)


def swiglu_up(x, wg, wu, *, tm, tf):
    m, d = x.shape
    f = wg.shape[1]
    return pl.pallas_call(
        functools.partial(_swiglu_body, scaled=False, cast_w=True),
        grid=(f // tf, m // tm),
        in_specs=[pl.BlockSpec((tm, d), lambda j, i: (i, 0)), pl.BlockSpec((d, tf), lambda j, i: (0, j)),
                  pl.BlockSpec((d, tf), lambda j, i: (0, j))],
        out_specs=pl.BlockSpec((tm, tf), lambda j, i: (i, j)),
        out_shape=jax.ShapeDtypeStruct((m, f), BF16),
        scratch_shapes=[pltpu.VMEM((d, tf), BF16), pltpu.VMEM((d, tf), BF16)],
        compiler_params=_cparams("parallel", "arbitrary"),
        name="swiglu_up",
    )(x, wg, wu)


def moe_up(x, wg, wu, gate, *, tm):
    m, d = x.shape
    ne, _, f = wg.shape
    return pl.pallas_call(
        functools.partial(_swiglu_body, scaled=True, cast_w=False),
        grid=(ne, m // tm),
        in_specs=[pl.BlockSpec((tm, d), lambda j, i: (i, 0)), pl.BlockSpec((None, d, f), lambda j, i: (j, 0, 0)),
                  pl.BlockSpec((None, d, f), lambda j, i: (j, 0, 0)), pl.BlockSpec((tm, LANES), lambda j, i: (i, 0))],
        out_specs=pl.BlockSpec((tm, f), lambda j, i: (i, j)),
        out_shape=jax.ShapeDtypeStruct((m, ne * f), BF16),
        compiler_params=_cparams("parallel", "parallel"),
        name="moe_up",
    )(x, wg, wu, gate)


def _blockdiag_body(x_ref, w_ref, o_ref):
    o_ref[...] = _dot(x_ref[...], w_ref[...]).astype(o_ref.dtype)


def blockdiag_matmul(x, w, *, tm, row_block, col_block0, col_stride, out_dtype=BF16):
    nh, ki, no = w.shape
    return pl.pallas_call(
        _blockdiag_body,
        grid=(nh,),
        in_specs=[pl.BlockSpec((tm, ki), lambda h: (row_block, col_block0 + h * col_stride)),
                  pl.BlockSpec((None, ki, no), lambda h: (h, 0, 0))],
        out_specs=pl.BlockSpec((tm, no), lambda h: (0, h)),
        out_shape=jax.ShapeDtypeStruct((tm, nh * no), out_dtype),
        compiler_params=_cparams("parallel"),
        name="blockdiag_matmul",
    )(x, w)


def _rope_slab(x, cos, sin):
    half = QK_ROPE // 2
    lane = lax.broadcasted_iota(jnp.int32, x.shape, 1)
    swapped = jnp.where(lane < half, pltpu.roll(x, LANES - half, 1), pltpu.roll(x, half, 1))
    return x * cos + swapped * sin


def _mla_prep_body(qc_ref, kvc_ref, kpe_ref, gq_ref, gkv_ref, cos_ref, sin_ref, qn_ref, ckv_ref, kpe_out_ref):
    qn_ref[...] = _rms(qc_ref[...], gq_ref[...]).astype(qn_ref.dtype)
    ckv_ref[...] = _rms(kvc_ref[...], gkv_ref[...])
    kpe_out_ref[...] = _rope_slab(kpe_ref[...], cos_ref[...], sin_ref[...])


def mla_prep(proj, slabs, g_q, g_kv, cos, sin, *, tm):
    m = proj.shape[0]
    return pl.pallas_call(
        _mla_prep_body,
        grid=(m // tm,),
        in_specs=[pl.BlockSpec((tm, Q_LORA), lambda i: (i, COL_QC // Q_LORA)),
                  pl.BlockSpec((tm, KV_LORA), lambda i: (i, COL_KVC // KV_LORA)),
                  pl.BlockSpec((tm, LANES), lambda i: (i, SLAB_KPE)),
                  pl.BlockSpec((1, Q_LORA), lambda i: (0, 0)), pl.BlockSpec((1, KV_LORA), lambda i: (0, 0)),
                  pl.BlockSpec((tm, LANES), lambda i: (i, 0)), pl.BlockSpec((tm, LANES), lambda i: (i, 0))],
        out_specs=[pl.BlockSpec((tm, Q_LORA), lambda i: (i, 0)), pl.BlockSpec((tm, KV_LORA), lambda i: (i, 0)),
                   pl.BlockSpec((tm, LANES), lambda i: (i, 0))],
        out_shape=[jax.ShapeDtypeStruct((m, Q_LORA), BF16), jax.ShapeDtypeStruct((m, KV_LORA), F32),
                   jax.ShapeDtypeStruct((m, LANES), F32)],
        compiler_params=_cparams("parallel"),
        name="mla_prep",
    )(proj, proj, slabs, g_q.reshape(1, -1), g_kv.reshape(1, -1), cos, sin)


def _qproj_body(x_ref, w_ref, cos_ref, sin_ref, o_ref):
    acc = _dot(x_ref[...], w_ref[...])
    cos = cos_ref[...] * Q_SCALE
    sin = sin_ref[...] * Q_SCALE
    for h in range(MLA_HEADS):
        base = h * Q_SLOT
        o_ref[:, base:base + QK_NOPE] = (acc[:, base:base + QK_NOPE] * Q_SCALE).astype(o_ref.dtype)
        o_ref[:, base + QK_NOPE:base + Q_SLOT] = _rope_slab(acc[:, base + QK_NOPE:base + Q_SLOT], cos, sin).astype(o_ref.dtype)


def qproj(qn, w_q_slots, cos, sin, *, tm):
    m = qn.shape[0]
    n = w_q_slots.shape[1]
    return pl.pallas_call(
        _qproj_body,
        grid=(m // tm,),
        in_specs=[pl.BlockSpec((tm, Q_LORA), lambda i: (i, 0)), pl.BlockSpec((Q_LORA, n), lambda i: (0, 0)),
                  pl.BlockSpec((tm, LANES), lambda i: (i, 0)), pl.BlockSpec((tm, LANES), lambda i: (i, 0))],
        out_specs=pl.BlockSpec((tm, n), lambda i: (i, 0)),
        out_shape=jax.ShapeDtypeStruct((m, n), BF16),
        compiler_params=_cparams("parallel"),
        name="qproj",
    )(qn, w_q_slots, cos, sin)


def _kv_expand_body(ckv_ref, kpe_ref, w_ref, k_ref, v_ref):
    acc = _dot(ckv_ref[...].astype(BF16), w_ref[...])
    kpe = kpe_ref[...].astype(k_ref.dtype)
    for h in range(MLA_HEADS):
        base = h * Q_SLOT
        k_ref[:, base:base + QK_NOPE] = acc[:, h * QK_NOPE:(h + 1) * QK_NOPE].astype(k_ref.dtype)
        k_ref[:, base + QK_NOPE:base + Q_SLOT] = kpe
    v_ref[...] = acc[:, MLA_HEADS * QK_NOPE:].astype(v_ref.dtype)


def kv_expand(ckv, kpe, w_kv_flat, *, rows, tm):
    n = w_kv_flat.shape[1]
    return pl.pallas_call(
        _kv_expand_body,
        grid=(rows // tm,),
        in_specs=[pl.BlockSpec((tm, KV_LORA), lambda i: (i, 0)), pl.BlockSpec((tm, LANES), lambda i: (i, 0)),
                  pl.BlockSpec((KV_LORA, n), lambda i: (0, 0))],
        out_specs=[pl.BlockSpec((tm, MLA_HEADS * Q_SLOT), lambda i: (i, 0)),
                   pl.BlockSpec((tm, MLA_HEADS * V_HEAD), lambda i: (i, 0))],
        out_shape=[jax.ShapeDtypeStruct((rows, MLA_HEADS * Q_SLOT), BF16),
                   jax.ShapeDtypeStruct((rows, MLA_HEADS * V_HEAD), BF16)],
        compiler_params=_cparams("parallel"),
        name="kv_expand",
    )(ckv, kpe, w_kv_flat)


def _flash_body(q_ref, k_ref, v_ref, o_ref, *, blk, heads):
    qi = pl.program_id(2)
    qs = [q_ref[:, h * Q_SLOT:(h + 1) * Q_SLOT] for h in range(heads)]

    def update(j, diagonal, h, m, l, acc):
        start = pl.multiple_of(j * blk, blk)
        k = k_ref[pl.ds(start, blk), h * Q_SLOT:(h + 1) * Q_SLOT]
        v = v_ref[pl.ds(start, blk), h * V_HEAD:(h + 1) * V_HEAD]
        s = _dot_nt(qs[h], k)
        if diagonal:
            row = lax.broadcasted_iota(jnp.int32, s.shape, 0)
            col = lax.broadcasted_iota(jnp.int32, s.shape, 1)
            s = jnp.where(row >= col, s, -jnp.inf)
        m_new = jnp.maximum(m, jnp.max(s, axis=1, keepdims=True))
        alpha = jnp.exp2(m - m_new)
        p = jnp.exp2(s - m_new)
        l = alpha * l + jnp.sum(p, axis=1, keepdims=True)
        acc = alpha * acc + _dot(p.astype(BF16), v)
        return m_new, l, acc

    def step(j, carry, diagonal):
        return tuple(update(j, diagonal, h, *carry[h]) for h in range(heads))

    init = tuple((jnp.full((blk, 1), -jnp.inf, F32), jnp.zeros((blk, 1), F32), jnp.zeros((blk, V_HEAD), F32))
                 for _ in range(heads))
    carry = lax.fori_loop(0, qi, lambda j, c: step(j, c, False), init)
    carry = step(qi, carry, True)
    for h in range(heads):
        _, l, acc = carry[h]
        o_ref[:, h * V_HEAD:(h + 1) * V_HEAD] = (acc / l).astype(o_ref.dtype)


def flash_attention(q, k, v, *, batch, seq, blk, heads):
    nq = seq // blk
    return pl.pallas_call(
        functools.partial(_flash_body, blk=blk, heads=heads),
        grid=(batch, MLA_HEADS // heads, nq),
        in_specs=[pl.BlockSpec((blk, heads * Q_SLOT), lambda b, h, i: (b * nq + i, h)),
                  pl.BlockSpec((seq, heads * Q_SLOT), lambda b, h, i: (b, h)),
                  pl.BlockSpec((seq, heads * V_HEAD), lambda b, h, i: (b, h))],
        out_specs=pl.BlockSpec((blk, heads * V_HEAD), lambda b, h, i: (b * nq + i, h)),
        out_shape=jax.ShapeDtypeStruct((batch * seq, MLA_HEADS * V_HEAD), BF16),
        compiler_params=_cparams("parallel", "parallel", "parallel"),
        name="flash_attention",
    )(q, k, v)


def _decode_body(pt_ref, q_ref, qpe_ref, newc_ref, newk_ref, ckv_hbm, kpe_hbm, o_ref,
                 kbuf, pbuf, sems, m_sc, l_sc, acc_sc, *, layer, pages_per_step, n_steps, n_new):
    pps = pages_per_step
    c = pl.program_id(1)
    n_total = pl.num_programs(0) * n_steps
    g = pl.program_id(0) * n_steps + c
    slot = lax.rem(g, 2)
    nxt = lax.rem(g + 1, n_total)

    def page_copies(chunk, slot_, i):
        pg = pt_ref[chunk * pps + i]
        return (pltpu.make_async_copy(ckv_hbm.at[layer, pg], kbuf.at[slot_, i], sems.at[slot_, 0]),
                pltpu.make_async_copy(kpe_hbm.at[layer, pg], pbuf.at[slot_, i], sems.at[slot_, 1]))

    @pl.when(g == 0)
    def _():
        for i in range(pps):
            for cp in page_copies(0, 0, i):
                cp.start()

    for i in range(pps):
        for cp in page_copies(g, slot, i):
            cp.wait()

    @pl.when(c == 0)
    def _():
        m_sc[...] = jnp.full(m_sc.shape, -jnp.inf, F32)
        l_sc[...] = jnp.zeros(l_sc.shape, F32)
        acc_sc[...] = jnp.zeros(acc_sc.shape, F32)

    q = q_ref[...]
    qp = qpe_ref[:, :QK_ROPE]
    ks, ss = [], []
    for i in range(pps):
        for cp in page_copies(nxt, 1 - slot, i):
            cp.start()
        k = kbuf[slot, i].astype(BF16)
        kp_t = pbuf[slot, i].astype(BF16)
        ks.append(k)
        ss.append(_dot_nt(q, k) + _dot(qp, kp_t))
    s = jnp.concatenate(ss, axis=1)
    m_prev = m_sc[:, :1]
    l_prev = l_sc[:, :1]
    m_new = jnp.maximum(m_prev, jnp.max(s, axis=1, keepdims=True))
    alpha = jnp.exp2(m_prev - m_new)
    p = jnp.exp2(s - m_new)
    l_new = alpha * l_prev + jnp.sum(p, axis=1, keepdims=True)
    page = ks[0].shape[0]
    pv = _dot(p[:, :page].astype(BF16), ks[0])
    for i in range(1, pps):
        pv = pv + _dot(p[:, i * page:(i + 1) * page].astype(BF16), ks[i])
    acc_new = alpha * acc_sc[...] + pv
    m_sc[...] = jnp.broadcast_to(m_new, m_sc.shape)
    l_sc[...] = jnp.broadcast_to(l_new, l_sc.shape)
    acc_sc[...] = acc_new

    @pl.when(c == n_steps - 1)
    def _():
        qf = q.astype(F32)
        qpf = qp.astype(F32)
        kn = newc_ref[...]
        kpn = newk_ref[:, :QK_ROPE]
        row = lax.broadcasted_iota(jnp.int32, (q.shape[0], 1), 0)
        sj = []
        for j in range(n_new):
            v = (jnp.sum(qf * kn[j:j + 1, :], axis=1, keepdims=True)
                 + jnp.sum(qpf * kpn[j:j + 1, :], axis=1, keepdims=True))
            sj.append(jnp.where(row >= j * MLA_HEADS, v, -jnp.inf))
        m_fin = m_new
        for v in sj:
            m_fin = jnp.maximum(m_fin, v)
        a2 = jnp.exp2(m_new - m_fin)
        l_fin = a2 * l_new
        acc_fin = a2 * acc_new
        for j in range(n_new):
            pj = jnp.exp2(sj[j] - m_fin)
            l_fin = l_fin + pj
            acc_fin = acc_fin + pj * kn[j:j + 1, :]
        o_ref[...] = (acc_fin / l_fin).astype(o_ref.dtype)

    @pl.when(g == n_total - 1)
    def _():
        for i in range(pps):
            for cp in page_copies(nxt, 1 - slot, i):
                cp.wait()


def decode_attention(q_lat, q_pe, cache_ckv, cache_kpe, layer, page_table, new_ckv, new_kpe, *, pages_per_step):
    nb, rows, _ = q_lat.shape
    n_pages = page_table.shape[1]
    page = cache_ckv.shape[2]
    pps = pages_per_step
    n_steps = n_pages // pps
    n_new = rows // MLA_HEADS

    assert n_pages == n_steps * pps
    in_specs = [pl.BlockSpec((None, rows, KV_LORA), lambda b, c, pt: (b, 0, 0)),
                pl.BlockSpec((None, rows, LANES), lambda b, c, pt: (b, 0, 0)),
                pl.BlockSpec((None, 8, KV_LORA), lambda b, c, pt: (b, 0, 0)),
                pl.BlockSpec((None, 8, LANES), lambda b, c, pt: (b, 0, 0)),
                pl.BlockSpec(memory_space=pl.ANY), pl.BlockSpec(memory_space=pl.ANY)]
    grid_spec = pltpu.PrefetchScalarGridSpec(
        num_scalar_prefetch=1,
        grid=(nb, n_steps),
        in_specs=in_specs,
        out_specs=pl.BlockSpec((None, rows, KV_LORA), lambda b, c, pt: (b, 0, 0)),
        scratch_shapes=[pltpu.VMEM((2, pps, page, KV_LORA), F32), pltpu.VMEM((2, pps, QK_ROPE, page), F32),
                        pltpu.SemaphoreType.DMA((2, 2)),
                        pltpu.VMEM((rows, LANES), F32), pltpu.VMEM((rows, LANES), F32),
                        pltpu.VMEM((rows, KV_LORA), F32)],
    )
    return pl.pallas_call(
        functools.partial(_decode_body, layer=layer, pages_per_step=pps, n_steps=n_steps, n_new=n_new),
        grid_spec=grid_spec,
        out_shape=jax.ShapeDtypeStruct((nb, rows, KV_LORA), BF16),
        compiler_params=_cparams("arbitrary", "arbitrary"),
        name="decode_attention",
    )(page_table.reshape(-1), q_lat, q_pe, new_ckv, new_kpe, cache_ckv, cache_kpe)


def _group_rmsnorm(y, g):
    gw = SSM_INNER // SSM_GROUPS
    parts = []
    for i in range(SSM_GROUPS):
        yg = y[:, i * gw:(i + 1) * gw]
        parts.append(yg * lax.rsqrt(jnp.mean(yg * yg, axis=-1, keepdims=True) + EPS))
    return jnp.concatenate(parts, axis=1) * g


def _ssd_state_update(st, bm, xd, acs_x):
    last = acs_x.shape[0] - 1
    xde = (xd * jnp.exp(acs_x[last:last + 1, :] - acs_x)).astype(BF16)
    gw = SSM_INNER // SSM_GROUPS
    upd = [_dot_tn(bm[:, g * SSM_STATE:(g + 1) * SSM_STATE].astype(BF16), xde[:, g * gw:(g + 1) * gw])
           for g in range(SSM_GROUPS)]
    return st * jnp.exp(acs_x[last:last + 1, :]) + jnp.concatenate(upd, axis=1)


def _ssd_y_off(st, cm, acs_x):
    gw = SSM_INNER // SSM_GROUPS
    parts = [_dot(cm[:, g * SSM_STATE:(g + 1) * SSM_STATE].astype(BF16), st[:, g * gw:(g + 1) * gw].astype(BF16))
             for g in range(SSM_GROUPS)]
    return jnp.concatenate(parts, axis=1) * jnp.exp(acs_x)


def _ssd_prompt_body(z_ref, x_ref, bc_ref, dt_ref, cw_ref, cb_ref, dtb_ref, alog_ref, alogx_ref, dvec_ref, gn_ref,
                     e_ref, tril_ref, y_ref, st_ref, xp_sc, st_sc, *, n_chunks):
    L = SSM_CHUNK
    c = pl.program_id(1)

    @pl.when(c == 0)
    def _():
        xp_sc[0:8, :] = jnp.zeros((8, xp_sc.shape[1]), F32)
        st_sc[...] = jnp.zeros(st_sc.shape, F32)

    @pl.when(c > 0)
    def _():
        xp_sc[0:8, :] = xp_sc[L:L + 8, :]

    xp_sc[8:8 + L, 0:SSM_INNER] = x_ref[...]
    xp_sc[8:8 + L, SSM_INNER:] = bc_ref[...]
    conv = xp_sc[pl.ds(8 - (SSM_CONV - 1), L), :] * cw_ref[0:1, :]
    for k in range(1, SSM_CONV):
        conv = conv + xp_sc[pl.ds(8 - (SSM_CONV - 1) + k, L), :] * cw_ref[k:k + 1, :]
    xbc = _silu(conv + cb_ref[...])
    xs = xbc[:, :SSM_INNER]
    bm = xbc[:, SSM_INNER:SSM_INNER + SSM_GROUPS * SSM_STATE]
    cm = xbc[:, SSM_INNER + SSM_GROUPS * SSM_STATE:]

    hi = lax.Precision.HIGHEST
    dt = _softplus(dt_ref[...] + dtb_ref[...])
    tril = tril_ref[...]
    acs = _dot(tril, dt * (-jnp.exp(alog_ref[...])), precision=hi)
    acs_t = acs.T
    dt_x = _dot(dt, e_ref[...], precision=hi)
    acs_x = _dot(tril, dt_x * (-jnp.exp(alogx_ref[...])), precision=hi)
    xd = xs * dt_x
    xd_b = xd.astype(BF16)

    row = lax.broadcasted_iota(jnp.int32, (L, L), 0)
    col = lax.broadcasted_iota(jnp.int32, (L, L), 1)
    causal = row >= col
    lane = lax.broadcasted_iota(jnp.int32, (L, LANES), 1)
    heads_per_group = SSM_HEADS // SSM_GROUPS
    y_parts = []
    cb = [_dot_nt(cm[:, g * SSM_STATE:(g + 1) * SSM_STATE].astype(BF16),
                  bm[:, g * SSM_STATE:(g + 1) * SSM_STATE].astype(BF16)) for g in range(SSM_GROUPS)]
    for pair in range(SSM_HEADS // 2):
        xd_pair = xd_b[:, pair * LANES:(pair + 1) * LANES]
        outs = []
        for h in (2 * pair, 2 * pair + 1):
            decay = jnp.exp(jnp.where(causal, acs[:, h:h + 1] - acs_t[h:h + 1, :], -jnp.inf))
            outs.append(_dot((cb[h // heads_per_group] * decay).astype(BF16), xd_pair))
        y_parts.append(jnp.where(lane < SSM_HEAD_DIM, outs[0], outs[1]))
    y_diag = jnp.concatenate(y_parts, axis=1)

    st = st_sc[...]
    y = (y_diag + _ssd_y_off(st, cm, acs_x)) + dvec_ref[...] * xs
    y = y * _silu(z_ref[...])
    y_ref[...] = _group_rmsnorm(y, gn_ref[...]).astype(y_ref.dtype)
    st_new = _ssd_state_update(st, bm, xd, acs_x)
    st_sc[...] = st_new

    @pl.when(c == n_chunks - 1)
    def _():
        st_ref[...] = st_new.T


def _ssd_consts(conv_w, conv_b, dt_bias, a_log, d_vec, g_norm):
    pad = LANES - SSM_HEADS
    e_np = np.zeros((LANES, SSM_INNER), np.float32)
    for hh in range(SSM_HEADS):
        e_np[hh, hh * SSM_HEAD_DIM:(hh + 1) * SSM_HEAD_DIM] = 1.0
    e_mat = jnp.asarray(e_np)
    return dict(
        cw=conv_w, cb=conv_b.reshape(1, -1),
        dtb=jnp.pad(dt_bias, (0, pad)).reshape(1, LANES),
        alog=jnp.pad(a_log, (0, pad)).reshape(1, LANES),
        alogx=jnp.repeat(a_log, SSM_HEAD_DIM).reshape(1, SSM_INNER),
        dvec=jnp.repeat(d_vec, SSM_HEAD_DIM).reshape(1, SSM_INNER),
        gn=g_norm.reshape(1, SSM_INNER), e=e_mat)


def _full(shape):
    nd = len(shape)
    return pl.BlockSpec(shape, lambda *_: (0,) * nd)


def ssd_prompt(proj, slabs, consts, *, batch, seq):
    L = SSM_CHUNK
    nc = seq // L
    cdim = SSM_INNER + 2 * SSM_GROUPS * SSM_STATE
    tril = jnp.asarray(np.tril(np.ones((L, L), np.float32)))

    def rows(width, col):
        return pl.BlockSpec((L, width), lambda b, c: (b * nc + c, col // width))

    k = consts
    return pl.pallas_call(
        functools.partial(_ssd_prompt_body, n_chunks=nc),
        grid=(batch, nc),
        in_specs=[rows(SSM_INNER, COL_Z), rows(SSM_INNER, COL_X), rows(2 * SSM_GROUPS * SSM_STATE, COL_BC),
                  pl.BlockSpec((L, LANES), lambda b, c: (b * nc + c, SLAB_DT)),
                  _full((SSM_CONV, cdim)), _full((1, cdim)), _full((1, LANES)), _full((1, LANES)),
                  _full((1, SSM_INNER)), _full((1, SSM_INNER)), _full((1, SSM_INNER)),
                  _full((LANES, SSM_INNER)), _full((L, L))],
        out_specs=[pl.BlockSpec((L, SSM_INNER), lambda b, c: (b * nc + c, 0)),
                   pl.BlockSpec((None, SSM_INNER, SSM_STATE), lambda b, c: (b, 0, 0))],
        out_shape=[jax.ShapeDtypeStruct((batch * seq, SSM_INNER), BF16),
                   jax.ShapeDtypeStruct((batch, SSM_INNER, SSM_STATE), F32)],
        scratch_shapes=[pltpu.VMEM((L + 8, cdim), F32), pltpu.VMEM((SSM_STATE, SSM_INNER), F32)],
        compiler_params=_cparams("parallel", "arbitrary"),
        name="ssd_prompt",
    )(proj, proj, proj, slabs, k["cw"], k["cb"], k["dtb"], k["alog"], k["alogx"], k["dvec"], k["gn"], k["e"], tril)


def _sample_mixer_body(z_ref, dt_ref, scb_ref, xp_ref, up_c_ref, up_v_ref, st_in_ref,
                       cw_ref, cb_ref, dtb_ref, alogx_ref, dvec_ref, gn_ref, e_ref, scw_ref,
                       y_ref, ysc_ref, st_ref, u_out_ref, bpad_sc, xdpad_sc, u_sc, *, n_new, group):
    @pl.when(pl.program_id(0) == 0)
    def _():
        bpad_sc[...] = jnp.zeros(bpad_sc.shape, F32)
        xdpad_sc[...] = jnp.zeros(xdpad_sc.shape, F32)

    for i in range(group):
        _sample_mixer_one(z_ref.at[i], dt_ref.at[i], scb_ref.at[i], xp_ref.at[i], up_c_ref.at[i], up_v_ref.at[i],
                          st_in_ref.at[i], cw_ref, cb_ref, dtb_ref, alogx_ref, dvec_ref, gn_ref, e_ref, scw_ref,
                          y_ref.at[i], ysc_ref.at[i], st_ref.at[i], u_out_ref.at[i],
                          bpad_sc.at[i], xdpad_sc.at[i], u_sc.at[i], n_new=n_new)


def _sample_mixer_one(z_ref, dt_ref, scb_ref, xp_ref, up_c_ref, up_v_ref, st_in_ref,
                      cw_ref, cb_ref, dtb_ref, alogx_ref, dvec_ref, gn_ref, e_ref, scw_ref,
                      y_ref, ysc_ref, st_ref, u_out_ref, bpad_sc, xdpad_sc, u_sc, *, n_new):
    R = 8
    conv = xp_ref[pl.ds(8 - (SSM_CONV - 1), R), :] * cw_ref[0:1, :]
    for k in range(1, SSM_CONV):
        conv = conv + xp_ref[pl.ds(8 - (SSM_CONV - 1) + k, R), :] * cw_ref[k:k + 1, :]
    xbc = _silu(conv + cb_ref[...])
    xs = xbc[:, :SSM_INNER]
    bm = xbc[:, SSM_INNER:SSM_INNER + SSM_GROUPS * SSM_STATE]
    cm = xbc[:, SSM_INNER + SSM_GROUPS * SSM_STATE:]

    hi = lax.Precision.HIGHEST
    rowl = lax.broadcasted_iota(jnp.int32, (R, LANES), 0)
    dt = jnp.where(rowl < n_new, _softplus(dt_ref[...] + dtb_ref[...]), 0.0)
    dt_x = _dot(dt, e_ref[...], precision=hi)
    da_x = dt_x * (-jnp.exp(alogx_ref[...]))
    rowx = lax.broadcasted_iota(jnp.int32, (R, SSM_INNER), 0)
    acs_x = jnp.zeros((R, SSM_INNER), F32)
    for s in range(n_new):
        acs_x = acs_x + jnp.where(rowx >= s, da_x[s:s + 1, :], 0.0)
    xd = xs * dt_x

    gw = SSM_INNER // SSM_GROUPS
    lanex = lax.broadcasted_iota(jnp.int32, (R, SSM_INNER), 1)
    y_diag = jnp.zeros((R, SSM_INNER), F32)
    for s in range(n_new):
        cbs = [jnp.sum(cm[:, g * SSM_STATE:(g + 1) * SSM_STATE] * bm[s:s + 1, g * SSM_STATE:(g + 1) * SSM_STATE],
                       axis=1, keepdims=True) for g in range(SSM_GROUPS)]
        cb_x = jnp.where(lanex < gw, cbs[0], cbs[1])
        decay = jnp.exp(jnp.where(rowx >= s, acs_x - acs_x[s:s + 1, :], -jnp.inf))
        y_diag = y_diag + (cb_x * decay) * xd[s:s + 1, :]

    st = st_in_ref[...].T
    y_off_parts = [_dot(cm[:, g * SSM_STATE:(g + 1) * SSM_STATE], st[:, g * gw:(g + 1) * gw])
                   for g in range(SSM_GROUPS)]
    y_off = jnp.concatenate(y_off_parts, axis=1) * jnp.exp(acs_x)
    y = (y_diag + y_off) + dvec_ref[...] * xs
    y = y * _silu(z_ref[...])
    y_ref[...] = _group_rmsnorm(y, gn_ref[...])

    bpad_sc[0:R, :] = bm
    xdpad_sc[0:R, :] = xd * jnp.exp(acs_x[R - 1:R, :] - acs_x)
    upd = [_dot_tn(bpad_sc[:, g * SSM_STATE:(g + 1) * SSM_STATE], xdpad_sc[:, g * gw:(g + 1) * gw])
           for g in range(SSM_GROUPS)]
    st_new = st * jnp.exp(acs_x[R - 1:R, :]) + jnp.concatenate(upd, axis=1)
    st_ref[...] = st_new.T

    u_sc[...] = up_c_ref[...] * up_v_ref[...]
    sconv = u_sc[pl.ds(8 - (SC_CONV - 1), R), :] * scw_ref[0:1, :]
    for k in range(1, SC_CONV):
        sconv = sconv + u_sc[pl.ds(8 - (SC_CONV - 1) + k, R), :] * scw_ref[k:k + 1, :]
    ysc_ref[...] = scb_ref[...] * sconv
    u_out_ref[...] = u_sc[8:16, :]


def sample_mixer(proj_s, slabs_s, xp_s, up_c, up_v, st_all, layer, consts, sc_conv_w, *, n_new, group):
    nb = proj_s.shape[0]
    cdim = SSM_INNER + 2 * SSM_GROUPS * SSM_STATE
    k = consts
    gs = group

    def rows(width, col):
        return pl.BlockSpec((gs, 8, width), lambda b: (b, 0, col // width))

    def per_sample(*tail):
        return pl.BlockSpec((gs,) + tail, lambda b: (b,) + (0,) * len(tail))

    st_spec = pl.BlockSpec((None, gs, SSM_INNER, SSM_STATE), lambda b: (layer, b, 0, 0))
    args = [proj_s, slabs_s, proj_s, xp_s, up_c, up_v, st_all,
            k["cw"], k["cb"], k["dtb"], k["alogx"], k["dvec"], k["gn"], k["e"], sc_conv_w]
    in_specs = [rows(SSM_INNER, COL_Z), pl.BlockSpec((gs, 8, LANES), lambda b: (b, 0, SLAB_DT)),
                rows(SC_WIDTH, COL_SCB),
                per_sample(16, cdim), per_sample(16, SC_WIDTH), per_sample(16, SC_WIDTH), st_spec,
                _full((SSM_CONV, cdim)), _full((1, cdim)), _full((1, LANES)),
                _full((1, SSM_INNER)), _full((1, SSM_INNER)), _full((1, SSM_INNER)),
                _full((LANES, SSM_INNER)), _full((SC_CONV, SC_WIDTH))]
    return pl.pallas_call(
        functools.partial(_sample_mixer_body, n_new=n_new, group=gs),
        grid=(nb // gs,),
        in_specs=in_specs,
        out_specs=[per_sample(8, SSM_INNER), per_sample(8, SC_WIDTH), per_sample(SSM_INNER, SSM_STATE),
                   per_sample(8, SC_WIDTH)],
        out_shape=[jax.ShapeDtypeStruct((nb, 8, SSM_INNER), F32), jax.ShapeDtypeStruct((nb, 8, SC_WIDTH), F32),
                   jax.ShapeDtypeStruct((nb, SSM_INNER, SSM_STATE), F32),
                   jax.ShapeDtypeStruct((nb, 8, SC_WIDTH), F32)],
        scratch_shapes=[pltpu.VMEM((gs, LANES, SSM_GROUPS * SSM_STATE), F32), pltpu.VMEM((gs, LANES, SSM_INNER), F32),
                        pltpu.VMEM((gs, 16, SC_WIDTH), F32)],
        compiler_params=_cparams("arbitrary"),
        name="sample_mixer",
    )(*args)


def _sconv_prompt_body(scb_ref, scc_ref, scv_ref, pc_ref, pv_ref, w_ref, y_ref, tail_ref, u_sc, *, ts):
    i = pl.program_id(1)
    hist = pc_ref[...] * pv_ref[...]
    u_sc[0:8, :] = jnp.where(i == 0, 0.0, hist)
    u_sc[8:8 + ts, :] = scc_ref[...] * scv_ref[...]
    conv = u_sc[pl.ds(8 - (SC_CONV - 1), ts), :] * w_ref[0:1, :]
    for k in range(1, SC_CONV):
        conv = conv + u_sc[pl.ds(8 - (SC_CONV - 1) + k, ts), :] * w_ref[k:k + 1, :]
    y_ref[...] = (scb_ref[...] * conv).astype(y_ref.dtype)
    tail_ref[...] = u_sc[ts:ts + 8, :]


def sconv_prompt(proj, sc_conv_w, *, batch, seq, ts):
    nt = seq // ts
    w = SC_WIDTH

    def rows(col):
        return pl.BlockSpec((ts, w), lambda b, i: (b * nt + i, col // w))

    def prev(col):
        return pl.BlockSpec((8, w), lambda b, i: (jnp.maximum((b * nt + i) * (ts // 8) - 1, 0), col // w))

    return pl.pallas_call(
        functools.partial(_sconv_prompt_body, ts=ts),
        grid=(batch, nt),
        in_specs=[rows(COL_SCB), rows(COL_SCC), rows(COL_SCV), prev(COL_SCC), prev(COL_SCV), _full((SC_CONV, w))],
        out_specs=[pl.BlockSpec((ts, w), lambda b, i: (b * nt + i, 0)),
                   pl.BlockSpec((None, 8, w), lambda b, i: (b, 0, 0))],
        out_shape=[jax.ShapeDtypeStruct((batch * seq, w), BF16), jax.ShapeDtypeStruct((batch, 8, w), F32)],
        scratch_shapes=[pltpu.VMEM((ts + 8, w), F32)],
        compiler_params=_cparams("parallel", "arbitrary"),
        name="sconv_prompt",
    )(proj, proj, proj, proj, proj, sc_conv_w)


def _merge_body(h_ref, yap_ref, ysp_ref, ycp_ref, yas_ref, yss_ref, ycs_ref,
                wga_ref, wgb_ref, wgc_ref, ba_ref, bb_ref, bc_ref, wa_ref, wb_ref, wc_ref, o_ref,
                wg_sc, wb_sc, *, n_prompt_tiles):
    i = pl.program_id(1)

    @pl.when(i == 0)
    def _():
        for n, r in enumerate((wga_ref, wgb_ref, wgc_ref)):
            wg_sc[n] = r[...].astype(BF16)
        for n, r in enumerate((wa_ref, wb_ref, wc_ref)):
            wb_sc[n] = r[...].astype(BF16)

    def merge(y_refs):
        h = h_ref[...]
        out = None
        for n, (y_ref, b_ref) in enumerate(zip(y_refs, (ba_ref, bb_ref, bc_ref))):
            term = _sigmoid(_dot(h, wg_sc[n]) + b_ref[...]) * _dot(y_ref[...], wb_sc[n])
            out = term if out is None else out + term
        o_ref[...] = out.astype(o_ref.dtype)

    @pl.when(i < n_prompt_tiles)
    def _():
        merge((yap_ref, ysp_ref, ycp_ref))

    @pl.when(i >= n_prompt_tiles)
    def _():
        merge((yas_ref, yss_ref, ycs_ref))


def gated_merge(h, y_prompt, y_sample, w_gate, b_gate, w_branches, layer, *, tm, tn):
    m, d = h.shape
    mp, kb = y_prompt[0].shape
    nb = d // tn
    npt = mp // tm
    b2 = b_gate.reshape(b_gate.shape[0], 1, -1)
    hspec = pl.BlockSpec((tm, d), lambda j, i: (i, 0))
    pspec = pl.BlockSpec((tm, kb), lambda j, i: (jnp.minimum(i, npt - 1), 0))
    once = pl.Buffered(1)
    sspec = pl.BlockSpec((tm, kb), lambda j, i: (jnp.maximum(i - npt, 0), 0), pipeline_mode=once)

    def wcol(rows_, off):
        return pl.BlockSpec((None, rows_, tn), lambda j, i: (layer, 0, off * nb + j), pipeline_mode=once)

    return pl.pallas_call(
        functools.partial(_merge_body, n_prompt_tiles=npt),
        grid=(nb, m // tm),
        in_specs=[hspec, pspec, pspec, pspec, sspec, sspec, sspec,
                  wcol(d, 0), wcol(d, 1), wcol(d, 2), wcol(1, 0), wcol(1, 1), wcol(1, 2),
                  wcol(kb, 0), wcol(kb, 0), wcol(kb, 0)],
        out_specs=pl.BlockSpec((tm, tn), lambda j, i: (i, j)),
        out_shape=jax.ShapeDtypeStruct((m, d), BF16),
        scratch_shapes=[pltpu.VMEM((3, d, tn), BF16), pltpu.VMEM((3, kb, tn), BF16)],
        compiler_params=_cparams("parallel", "arbitrary"),
        name="gated_merge",
    )(h, *y_prompt, *y_sample, w_gate, w_gate, w_gate, b2, b2, b2, *w_branches)


def _rope_tables(pos):
    half = QK_ROPE // 2
    inv = ROPE_THETA ** (-jnp.arange(half, dtype=F32) / half)
    ang = pos.astype(F32)[:, None] * inv[None, :]
    c, s = jnp.cos(ang), jnp.sin(ang)
    z = jnp.zeros((pos.shape[0], LANES - QK_ROPE), F32)
    return jnp.concatenate([c, c, z], axis=1), jnp.concatenate([-s, s, z], axis=1)


def _in_proj_row_offsets(layer):
    sizes = (Q_LORA, KV_LORA, QK_ROPE, SSM_INNER, SSM_INNER + 2 * SSM_GROUPS * SSM_STATE, SSM_HEADS,
             SC_WIDTH, SC_WIDTH, SC_WIDTH)
    q_c, kv_c, k_pe, z, xbc, dt, sc_b, sc_c, sc_v = (int(v) for v in np.cumsum((0,) + sizes)[:-1])
    d_in = int(sum(sizes))
    main = []
    for first, width in ((sc_b, SC_WIDTH), (sc_c, SC_WIDTH), (sc_v, SC_WIDTH), (z, SSM_INNER),
                         (xbc, SSM_INNER + 2 * SSM_GROUPS * SSM_STATE), (q_c, Q_LORA), (kv_c, KV_LORA)):
        main += [first + t for t in range(0, width, PROJ_TILE)]
    assert len(main) * PROJ_TILE == D_IN_PAD
    return tuple(layer * d_in + r for r in main), tuple(layer * d_in + r for r in (k_pe, dt))


def kernel(x_prompt, x_sample, cache_ckv, cache_kpe, state_ssm, state_mconv, state_sconv, page_table, g_attn_norm, w_in, g_q_a, w_q_b, g_kv_a, w_kv_b, ssm_conv_w, ssm_conv_b, ssm_dt_bias, ssm_a_log, ssm_d, g_ssm_norm, sc_conv_w, w_gate, b_gate, w_br_attn, w_br_ssm, w_br_sc, w_o, g_ffn_norm, w_ff_gate, w_ff_up, w_ff_down, w_router, w_e_gate, w_e_up, w_e_down, g_final):
    bp, tp, d = x_prompt.shape
    bs, ts, _ = x_sample.shape
    depth = w_in.shape[0]
    mp, ms = bp * tp, bs * ts
    m = mp + ms
    n_past = page_table.shape[1] * cache_ckv.shape[2]
    tm = m // 8
    tmh = m // 16
    tmw = m // 4
    cdim = SSM_INNER + 2 * SSM_GROUPS * SSM_STATE

    x = jnp.concatenate([x_prompt.reshape(mp, d), x_sample.reshape(ms, d)], axis=0)
    pos = jnp.concatenate([jnp.tile(jnp.arange(tp), bp), jnp.tile(n_past + jnp.arange(ts), bs)])
    cos, sin = _rope_tables(pos)
    cache_kpe_t = jnp.swapaxes(cache_kpe, 2, 3)
    st_all = state_ssm.reshape(depth, bs, SSM_INNER, SSM_STATE)
    w_in_t = jnp.swapaxes(w_in, 1, 2).reshape(-1, d)

    outs = {k: [] for k in ("p_ckv", "p_kpe", "p_ssm", "p_mconv", "p_sconv", "s_ckv", "s_kpe", "s_ssm", "s_mconv", "s_sconv")}
    for l in range(depth):
        wq = w_q_b[l]
        w_q_slots = jnp.concatenate([wq, jnp.zeros(wq.shape[:2] + (Q_SLOT - wq.shape[2],), wq.dtype)], axis=2)
        w_q_slots = w_q_slots.reshape(Q_LORA, MLA_HEADS * Q_SLOT).astype(BF16)
        wkv = w_kv_b[l]
        w_kv_flat = jnp.concatenate([wkv[..., :QK_NOPE].reshape(KV_LORA, -1), wkv[..., QK_NOPE:].reshape(KV_LORA, -1)],
                                    axis=1).astype(BF16)
        w_uk_t = jnp.transpose(wkv[..., :QK_NOPE], (1, 2, 0)).astype(BF16)
        w_uv = jnp.transpose(wkv[..., QK_NOPE:], (1, 0, 2)).astype(BF16)
        consts = _ssd_consts(ssm_conv_w[l], ssm_conv_b[l], ssm_dt_bias[l], ssm_a_log[l], ssm_d[l], g_ssm_norm[l])

        h = rmsnorm(x, g_attn_norm[l], BF16, tm)
        rows_main, rows_slabs = _in_proj_row_offsets(l)
        proj = in_proj(h, w_in_t, rows_main, tm=tmw, tn=PROJ_TILE)
        slabs = in_proj(h, w_in_t, rows_slabs, tm=tmw, tn=LANES)
        qn, ckv, kpe = mla_prep(proj, slabs, g_q_a[l], g_kv_a[l], cos, sin, tm=tm)
        q = qproj(qn, w_q_slots, cos, sin, tm=tm)

        k_full, v_full = kv_expand(ckv, kpe, w_kv_flat, rows=mp, tm=1024)
        ya_p = flash_attention(q, k_full, v_full, batch=bp, seq=tp, blk=1024, heads=2)

        q_lat = blockdiag_matmul(q, w_uk_t, tm=ms, row_block=mp // ms, col_block0=0, col_stride=2)
        q_pe_s = q[mp:].reshape(ms, MLA_HEADS, Q_SLOT)[:, :, QK_NOPE:].reshape(bs, ts * MLA_HEADS, LANES)
        ckv_s = ckv[mp:].reshape(bs, ts, KV_LORA)
        kpe_s = kpe[mp:].reshape(bs, ts, LANES)
        new_c = jnp.pad(ckv_s, ((0, 0), (0, 8 - ts), (0, 0)))
        new_k = jnp.pad(kpe_s, ((0, 0), (0, 8 - ts), (0, 0)))
        o_lat = decode_attention(q_lat.reshape(bs, ts * MLA_HEADS, KV_LORA), q_pe_s, cache_ckv, cache_kpe_t, l,
                                 page_table, new_c, new_k, pages_per_step=32)
        ya_s = blockdiag_matmul(o_lat.reshape(ms, MLA_HEADS * KV_LORA), w_uv, tm=ms, row_block=0, col_block0=0,
                                col_stride=1)

        ys_p, ssm_p = ssd_prompt(proj, slabs, consts, batch=bp, seq=tp)
        yc_p, u_tail_p = sconv_prompt(proj, sc_conv_w[l], batch=bp, seq=tp, ts=512)

        proj_s = jnp.pad(proj[mp:].reshape(bs, ts, D_IN_PAD), ((0, 0), (0, 8 - ts), (0, 0)))
        slabs_s = jnp.pad(slabs[mp:].reshape(bs, ts, D_IN_SLABS), ((0, 0), (0, 8 - ts), (0, 0)))
        xbc_s = proj_s[:, :ts, COL_X:COL_X + cdim]
        xp_s = jnp.concatenate([jnp.zeros((bs, 8 - (SSM_CONV - 1), cdim), F32), state_mconv[l], xbc_s,
                                jnp.zeros((bs, 8 - ts, cdim), F32)], axis=1)
        zpad = jnp.zeros((bs, 8 - (SC_CONV - 1), SC_WIDTH), F32)
        zend = jnp.zeros((bs, 8 - ts, SC_WIDTH), F32)
        up_c = jnp.concatenate([zpad, state_sconv[l], proj_s[:, :ts, COL_SCC:COL_SCC + SC_WIDTH], zend], axis=1)
        up_v = jnp.concatenate([zpad, jnp.ones_like(state_sconv[l]), proj_s[:, :ts, COL_SCV:COL_SCV + SC_WIDTH], zend], axis=1)
        ys_s, yc_s, ssm_s, u_new_s = sample_mixer(proj_s, slabs_s, xp_s, up_c, up_v, st_all, l, consts, sc_conv_w[l],
                                                  n_new=ts, group=4)

        y_sample = (ya_s, ys_s[:, :ts].reshape(ms, SSM_INNER).astype(BF16), yc_s[:, :ts].reshape(ms, SC_WIDTH).astype(BF16))
        merged = gated_merge(h, (ya_p, ys_p, yc_p), y_sample, w_gate, b_gate, (w_br_attn, w_br_ssm, w_br_sc), l,
                             tm=ms, tn=512)
        x = matmul(merged, w_o, tm=tm, tn=512, res=x, w_index=l)

        outs["p_ckv"].append(ckv[:mp].reshape(bp, tp, KV_LORA))
        outs["p_kpe"].append(kpe[:mp, :QK_ROPE].reshape(bp, tp, QK_ROPE))
        outs["p_ssm"].append(ssm_p.reshape(bp, SSM_HEADS, SSM_HEAD_DIM, SSM_STATE))
        outs["p_mconv"].append(jnp.stack([proj[(b + 1) * tp - (SSM_CONV - 1):(b + 1) * tp, COL_X:COL_X + cdim]
                                          for b in range(bp)]))
        outs["p_sconv"].append(u_tail_p[:, 8 - (SC_CONV - 1):])
        outs["s_ckv"].append(ckv_s)
        outs["s_kpe"].append(kpe_s[:, :, :QK_ROPE])
        outs["s_ssm"].append(ssm_s.reshape(bs, SSM_HEADS, SSM_HEAD_DIM, SSM_STATE))
        outs["s_mconv"].append(xp_s[:, 8 + ts - (SSM_CONV - 1):8 + ts])
        outs["s_sconv"].append(u_new_s[:, ts - (SC_CONV - 1):ts])

        i = l // 2
        if l % 2 == 0:
            h2 = rmsnorm(x, g_ffn_norm[l], BF16, tm)
            hdn = swiglu_up(h2, w_ff_gate[i], w_ff_up[i], tm=tmw, tf=512)
            x = matmul(hdn, w_ff_down[i].astype(BF16), tm=tm, tn=256, res=x, rows_outer=True)
        else:
            h2, gate = rmsnorm_router(x, g_ffn_norm[l], w_router[i], tm)
            hdn = moe_up(h2, w_e_gate[i].astype(BF16), w_e_up[i].astype(BF16), gate, tm=tm)
            wd = w_e_down[i].reshape(-1, d).astype(BF16)
            x = matmul(hdn, wd, tm=tmh, tn=256, res=x, rows_outer=True)

    y_p, y_s = rmsnorm_split(x, g_final, rows_prompt=mp, tm=ms)
    st = {k: jnp.stack(v, axis=0) for k, v in outs.items()}
    return (y_p.reshape(bp, tp, d), y_s.reshape(bs, ts, d),
            st["p_ckv"], st["p_kpe"], st["p_ssm"], st["p_mconv"], st["p_sconv"],
            st["s_ckv"], st["s_kpe"], st["s_ssm"], st["s_mconv"], st["s_sconv"])
```

```python
import functools

import jax
import jax.numpy as jnp
import numpy as np
from jax import lax
from jax.experimental import pallas as pl
from jax.experimental.pallas import tpu as pltpu

F32 = jnp.float32
BF16 = jnp.bfloat16
EPS = 1e-6
ROPE_THETA = 10000.0
LANES = 128
SUBLANES_BF16 = 16
MLA_HEADS = 8
QK_NOPE = 128
QK_ROPE = 64
V_HEAD = 128
KV_LORA = 512
Q_LORA = 512
Q_SLOT = 256
SSM_HEADS = 16
SSM_HEAD_DIM = 64
SSM_INNER = 1024
SSM_GROUPS = 2
SSM_STATE = 128
SSM_CONV = 4
SSM_CHUNK = 128
SC_WIDTH = 1024
SC_CONV = 3
N_EXPERTS = 8
ATTN_SCALE = (QK_NOPE + QK_ROPE) ** -0.5
Q_SCALE = ATTN_SCALE * float(np.log2(np.e))
VMEM_LIMIT = 56 * 1024 * 1024

COL_SCB, COL_SCC, COL_SCV, COL_Z, COL_X, COL_BC, COL_QC, COL_KVC = (
    0, 1024, 2048, 3072, 4096, 5120, 5632, 6144)
PROJ_TILE = 512
D_IN_PAD = 6656
SLAB_KPE, SLAB_DT = 0, 1
D_IN_SLABS = 2 * LANES


def _cparams(*sem):
    return pltpu.CompilerParams(dimension_semantics=sem, vmem_limit_bytes=VMEM_LIMIT)


def _sigmoid(x):
    return 1.0 / (1.0 + jnp.exp(-x))


def _silu(x):
    return x * _sigmoid(x)


def _softplus(x):
    return jnp.maximum(x, 0.0) + jnp.log1p(jnp.exp(-jnp.abs(x)))


def _dot(a, b, **kw):
    return jnp.dot(a, b, preferred_element_type=F32, **kw)


def _dot_nt(a, b):
    return lax.dot_general(a, b, (((1,), (1,)), ((), ())), preferred_element_type=F32)


def _dot_tn(a, b):
    return lax.dot_general(a, b, (((0,), (0,)), ((), ())), preferred_element_type=F32)


def _rms(x, g):
    r = lax.rsqrt(jnp.mean(x * x, axis=-1, keepdims=True) + EPS)
    return (x * r) * g


def _rmsnorm_body(x_ref, g_ref, o_ref):
    o_ref[...] = _rms(x_ref[...], g_ref[...]).astype(o_ref.dtype)


def rmsnorm(x, g, out_dtype, tm):
    m, d = x.shape
    return pl.pallas_call(
        _rmsnorm_body,
        grid=(m // tm,),
        in_specs=[pl.BlockSpec((tm, d), lambda i: (i, 0)), pl.BlockSpec((1, d), lambda i: (0, 0))],
        out_specs=pl.BlockSpec((tm, d), lambda i: (i, 0)),
        out_shape=jax.ShapeDtypeStruct((m, d), out_dtype),
        compiler_params=_cparams("parallel"),
        name="rmsnorm",
    )(x, g.reshape(1, d))


def _rmsnorm_split_body(x_ref, g_ref, op_ref, os_ref, *, n_prompt_tiles):
    i = pl.program_id(0)
    y = _rms(x_ref[...], g_ref[...])

    @pl.when(i < n_prompt_tiles)
    def _():
        op_ref[...] = y

    @pl.when(i >= n_prompt_tiles)
    def _():
        os_ref[...] = y


def rmsnorm_split(x, g, *, rows_prompt, tm):
    m, d = x.shape
    npt = rows_prompt // tm
    return pl.pallas_call(
        functools.partial(_rmsnorm_split_body, n_prompt_tiles=npt),
        grid=(m // tm,),
        in_specs=[pl.BlockSpec((tm, d), lambda i: (i, 0)), pl.BlockSpec((1, d), lambda i: (0, 0))],
        out_specs=[pl.BlockSpec((tm, d), lambda i: (jnp.minimum(i, npt - 1), 0)),
                   pl.BlockSpec((tm, d), lambda i: (jnp.maximum(i - npt, 0), 0))],
        out_shape=[jax.ShapeDtypeStruct((rows_prompt, d), F32), jax.ShapeDtypeStruct((m - rows_prompt, d), F32)],
        compiler_params=_cparams("arbitrary"),
        name="rmsnorm_split",
    )(x, g.reshape(1, d))


def _rmsnorm_router_body(x_ref, g_ref, wr_ref, h_ref, gate_ref):
    h = _rms(x_ref[...], g_ref[...])
    h_ref[...] = h.astype(h_ref.dtype)
    lg = _dot(h, wr_ref[...], precision=lax.Precision.HIGHEST)
    lane = lax.broadcasted_iota(jnp.int32, lg.shape, 1).astype(F32)
    lg = jnp.where(lane < N_EXPERTS, lg, -jnp.inf)
    m1 = jnp.max(lg, axis=1, keepdims=True)
    i1 = jnp.min(jnp.where(lg == m1, lane, float(LANES)), axis=1, keepdims=True)
    oh1 = lane == i1
    lg2 = jnp.where(oh1, -jnp.inf, lg)
    m2 = jnp.max(lg2, axis=1, keepdims=True)
    i2 = jnp.min(jnp.where(lg2 == m2, lane, float(LANES)), axis=1, keepdims=True)
    oh2 = lane == i2
    e = jnp.exp(m2 - m1)
    w1 = 1.0 / (1.0 + e)
    w2 = e / (1.0 + e)
    gate_ref[...] = jnp.where(oh1, w1, 0.0) + jnp.where(oh2, w2, 0.0)


def rmsnorm_router(x, g, w_router, tm):
    m, d = x.shape
    wr = jnp.pad(w_router, ((0, 0), (0, LANES - w_router.shape[1])))
    return pl.pallas_call(
        _rmsnorm_router_body,
        grid=(m // tm,),
        in_specs=[pl.BlockSpec((tm, d), lambda i: (i, 0)), pl.BlockSpec((1, d), lambda i: (0, 0)),
                  pl.BlockSpec((d, LANES), lambda i: (0, 0))],
        out_specs=[pl.BlockSpec((tm, d), lambda i: (i, 0)), pl.BlockSpec((tm, LANES), lambda i: (i, 0))],
        out_shape=[jax.ShapeDtypeStruct((m, d), BF16), jax.ShapeDtypeStruct((m, LANES), F32)],
        compiler_params=_cparams("parallel"),
        name="rmsnorm_router",
    )(x, g.reshape(1, d), wr)


def _cast_once(src_refs, dst_refs):
    @pl.when(pl.program_id(1) == 0)
    def _():
        for s, d in zip(src_refs, dst_refs):
            d[...] = s[...].astype(d.dtype)


def _mm_body(x_ref, w_ref, *rest, has_res, cast_w):
    rest = list(rest)
    r_ref = rest.pop(0) if has_res else None
    o_ref = rest.pop(0)
    if cast_w:
        (w_sc,) = rest
        _cast_once([w_ref], [w_sc])
        w_ref = w_sc
    acc = _dot(x_ref[...], w_ref[...])
    o_ref[...] = ((r_ref[...] + acc) if has_res else acc).astype(o_ref.dtype)


def matmul(x, w, *, tm, tn, res=None, out_dtype=F32, rows_outer=False, w_index=None):
    m, kd = x.shape
    n = w.shape[-1]
    cast_w = w.dtype != BF16
    assert not (cast_w and rows_outer)

    def ij(a, b):
        return (a, b) if rows_outer else (b, a)

    if w_index is None:
        w_spec = pl.BlockSpec((kd, tn), lambda a, b: (0, ij(a, b)[1]))
    else:
        w_spec = pl.BlockSpec((None, kd, tn), lambda a, b: (w_index, 0, ij(a, b)[1]))
    in_specs = [pl.BlockSpec((tm, kd), lambda a, b: (ij(a, b)[0], 0)), w_spec]
    args = [x, w]
    if res is not None:
        in_specs.append(pl.BlockSpec((tm, tn), lambda a, b: ij(a, b)))
        args.append(res)
    return pl.pallas_call(
        functools.partial(_mm_body, has_res=res is not None, cast_w=cast_w),
        grid=(m // tm, n // tn) if rows_outer else (n // tn, m // tm),
        in_specs=in_specs,
        out_specs=pl.BlockSpec((tm, tn), lambda a, b: ij(a, b)),
        out_shape=jax.ShapeDtypeStruct((m, n), out_dtype),
        scratch_shapes=[pltpu.VMEM((kd, tn), BF16)] if cast_w else [],
        compiler_params=_cparams("parallel", "arbitrary"),
        name="matmul",
    )(*args)


def _in_proj_body(offs_ref, x_ref, wt_ref, o_ref, w_sc):
    del offs_ref

    @pl.when(pl.program_id(1) == 0)
    def _():
        w_sc[...] = wt_ref[...].T.astype(w_sc.dtype)

    o_ref[...] = _dot(x_ref[...], w_sc[...])


def in_proj(x, w_t, row_offsets, *, tm, tn):
    m, kd = x.shape
    nt = len(row_offsets)
    assert all(o % SUBLANES_BF16 == 0 for o in row_offsets)
    grid_spec = pltpu.PrefetchScalarGridSpec(
        num_scalar_prefetch=1,
        grid=(nt, m // tm),
        in_specs=[pl.BlockSpec((tm, kd), lambda j, i, offs: (i, 0)),
                  pl.BlockSpec((pl.Element(tn), pl.Element(kd)),
                               lambda j, i, offs: (pl.multiple_of(offs[j], SUBLANES_BF16), 0))],
        out_specs=pl.BlockSpec((tm, tn), lambda j, i, offs: (i, j)),
        scratch_shapes=[pltpu.VMEM((kd, tn), BF16)],
    )
    return pl.pallas_call(
        _in_proj_body,
        grid_spec=grid_spec,
        out_shape=jax.ShapeDtypeStruct((m, nt * tn), F32),
        compiler_params=_cparams("parallel", "arbitrary"),
        name="in_proj",
    )(jnp.asarray(row_offsets, jnp.int32), x, w_t)


def _swiglu_body(x_ref, wg_ref, wu_ref, *rest, scaled, cast_w):
    rest = list(rest)
    gate_ref = rest.pop(0) if scaled else None
    o_ref = rest.pop(0)
    if cast_w:
        _cast_once([wg_ref, wu_ref], rest)
        wg_ref, wu_ref = rest
    x = x_ref[...]
    g = _dot(x, wg_ref[...])
    u = _dot(x, wu_ref[...])
    hdn = _silu(g) * u
    if scaled:
        gate = gate_ref[...]
        lane = lax.broadcasted_iota(jnp.int32, gate.shape, 1)
        sc = jnp.sum(jnp.where(lane == pl.program_id(0), gate, 0.0), axis=1, keepdims=True)
        hdn = hdn * sc
    o_ref[...] = hdn.astype(o_ref.dtype)


def swiglu_up(x, wg, wu, *, tm, tf):
    m, d = x.shape
    f = wg.shape[1]
    return pl.pallas_call(
        functools.partial(_swiglu_body, scaled=False, cast_w=True),
        grid=(f // tf, m // tm),
        in_specs=[pl.BlockSpec((tm, d), lambda j, i: (i, 0)), pl.BlockSpec((d, tf), lambda j, i: (0, j)),
                  pl.BlockSpec((d, tf), lambda j, i: (0, j))],
        out_specs=pl.BlockSpec((tm, tf), lambda j, i: (i, j)),
        out_shape=jax.ShapeDtypeStruct((m, f), BF16),
        scratch_shapes=[pltpu.VMEM((d, tf), BF16), pltpu.VMEM((d, tf), BF16)],
        compiler_params=_cparams("parallel", "arbitrary"),
        name="swiglu_up",
    )(x, wg, wu)


def moe_up(x, wg, wu, gate, *, tm):
    m, d = x.shape
    ne, _, f = wg.shape
    return pl.pallas_call(
        functools.partial(_swiglu_body, scaled=True, cast_w=False),
        grid=(ne, m // tm),
        in_specs=[pl.BlockSpec((tm, d), lambda j, i: (i, 0)), pl.BlockSpec((None, d, f), lambda j, i: (j, 0, 0)),
                  pl.BlockSpec((None, d, f), lambda j, i: (j, 0, 0)), pl.BlockSpec((tm, LANES), lambda j, i: (i, 0))],
        out_specs=pl.BlockSpec((tm, f), lambda j, i: (i, j)),
        out_shape=jax.ShapeDtypeStruct((m, ne * f), BF16),
        compiler_params=_cparams("parallel", "parallel"),
        name="moe_up",
    )(x, wg, wu, gate)


def _blockdiag_body(x_ref, w_ref, o_ref):
    o_ref[...] = _dot(x_ref[...], w_ref[...]).astype(o_ref.dtype)


def blockdiag_matmul(x, w, *, tm, row_block, col_block0, col_stride, out_dtype=BF16):
    nh, ki, no = w.shape
    return pl.pallas_call(
        _blockdiag_body,
        grid=(nh,),
        in_specs=[pl.BlockSpec((tm, ki), lambda h: (row_block, col_block0 + h * col_stride)),
                  pl.BlockSpec((None, ki, no), lambda h: (h, 0, 0))],
        out_specs=pl.BlockSpec((tm, no), lambda h: (0, h)),
        out_shape=jax.ShapeDtypeStruct((tm, nh * no), out_dtype),
        compiler_params=_cparams("parallel"),
        name="blockdiag_matmul",
    )(x, w)


def _rope_slab(x, cos, sin):
    half = QK_ROPE // 2
    lane = lax.broadcasted_iota(jnp.int32, x.shape, 1)
    swapped = jnp.where(lane < half, pltpu.roll(x, LANES - half, 1), pltpu.roll(x, half, 1))
    return x * cos + swapped * sin


def _mla_prep_body(qc_ref, kvc_ref, kpe_ref, gq_ref, gkv_ref, cos_ref, sin_ref, qn_ref, ckv_ref, kpe_out_ref):
    qn_ref[...] = _rms(qc_ref[...], gq_ref[...]).astype(qn_ref.dtype)
    ckv_ref[...] = _rms(kvc_ref[...], gkv_ref[...])
    kpe_out_ref[...] = _rope_slab(kpe_ref[...], cos_ref[...], sin_ref[...])


def mla_prep(proj, slabs, g_q, g_kv, cos, sin, *, tm):
    m = proj.shape[0]
    return pl.pallas_call(
        _mla_prep_body,
        grid=(m // tm,),
        in_specs=[pl.BlockSpec((tm, Q_LORA), lambda i: (i, COL_QC // Q_LORA)),
                  pl.BlockSpec((tm, KV_LORA), lambda i: (i, COL_KVC // KV_LORA)),
                  pl.BlockSpec((tm, LANES), lambda i: (i, SLAB_KPE)),
                  pl.BlockSpec((1, Q_LORA), lambda i: (0, 0)), pl.BlockSpec((1, KV_LORA), lambda i: (0, 0)),
                  pl.BlockSpec((tm, LANES), lambda i: (i, 0)), pl.BlockSpec((tm, LANES), lambda i: (i, 0))],
        out_specs=[pl.BlockSpec((tm, Q_LORA), lambda i: (i, 0)), pl.BlockSpec((tm, KV_LORA), lambda i: (i, 0)),
                   pl.BlockSpec((tm, LANES), lambda i: (i, 0))],
        out_shape=[jax.ShapeDtypeStruct((m, Q_LORA), BF16), jax.ShapeDtypeStruct((m, KV_LORA), F32),
                   jax.ShapeDtypeStruct((m, LANES), F32)],
        compiler_params=_cparams("parallel"),
        name="mla_prep",
    )(proj, proj, slabs, g_q.reshape(1, -1), g_kv.reshape(1, -1), cos, sin)


def _qproj_body(x_ref, w_ref, cos_ref, sin_ref, o_ref):
    acc = _dot(x_ref[...], w_ref[...])
    cos = cos_ref[...] * Q_SCALE
    sin = sin_ref[...] * Q_SCALE
    for h in range(MLA_HEADS):
        base = h * Q_SLOT
        o_ref[:, base:base + QK_NOPE] = (acc[:, base:base + QK_NOPE] * Q_SCALE).astype(o_ref.dtype)
        o_ref[:, base + QK_NOPE:base + Q_SLOT] = _rope_slab(acc[:, base + QK_NOPE:base + Q_SLOT], cos, sin).astype(o_ref.dtype)


def qproj(qn, w_q_slots, cos, sin, *, tm):
    m = qn.shape[0]
    n = w_q_slots.shape[1]
    return pl.pallas_call(
        _qproj_body,
        grid=(m // tm,),
        in_specs=[pl.BlockSpec((tm, Q_LORA), lambda i: (i, 0)), pl.BlockSpec((Q_LORA, n), lambda i: (0, 0)),
                  pl.BlockSpec((tm, LANES), lambda i: (i, 0)), pl.BlockSpec((tm, LANES), lambda i: (i, 0))],
        out_specs=pl.BlockSpec((tm, n), lambda i: (i, 0)),
        out_shape=jax.ShapeDtypeStruct((m, n), BF16),
        compiler_params=_cparams("parallel"),
        name="qproj",
    )(qn, w_q_slots, cos, sin)


def _kv_expand_body(ckv_ref, kpe_ref, w_ref, k_ref, v_ref):
    acc = _dot(ckv_ref[...].astype(BF16), w_ref[...])
    kpe = kpe_ref[...].astype(k_ref.dtype)
    for h in range(MLA_HEADS):
        base = h * Q_SLOT
        k_ref[:, base:base + QK_NOPE] = acc[:, h * QK_NOPE:(h + 1) * QK_NOPE].astype(k_ref.dtype)
        k_ref[:, base + QK_NOPE:base + Q_SLOT] = kpe
    v_ref[...] = acc[:, MLA_HEADS * QK_NOPE:].astype(v_ref.dtype)


def kv_expand(ckv, kpe, w_kv_flat, *, rows, tm):
    n = w_kv_flat.shape[1]
    return pl.pallas_call(
        _kv_expand_body,
        grid=(rows // tm,),
        in_specs=[pl.BlockSpec((tm, KV_LORA), lambda i: (i, 0)), pl.BlockSpec((tm, LANES), lambda i: (i, 0)),
                  pl.BlockSpec((KV_LORA, n), lambda i: (0, 0))],
        out_specs=[pl.BlockSpec((tm, MLA_HEADS * Q_SLOT), lambda i: (i, 0)),
                   pl.BlockSpec((tm, MLA_HEADS * V_HEAD), lambda i: (i, 0))],
        out_shape=[jax.ShapeDtypeStruct((rows, MLA_HEADS * Q_SLOT), BF16),
                   jax.ShapeDtypeStruct((rows, MLA_HEADS * V_HEAD), BF16)],
        compiler_params=_cparams("parallel"),
        name="kv_expand",
    )(ckv, kpe, w_kv_flat)


def _flash_body(q_ref, k_ref, v_ref, o_ref, *, blk, heads):
    qi = pl.program_id(2)
    qs = [q_ref[:, h * Q_SLOT:(h + 1) * Q_SLOT] for h in range(heads)]

    def update(j, diagonal, h, m, l, acc):
        start = pl.multiple_of(j * blk, blk)
        k = k_ref[pl.ds(start, blk), h * Q_SLOT:(h + 1) * Q_SLOT]
        v = v_ref[pl.ds(start, blk), h * V_HEAD:(h + 1) * V_HEAD]
        s = _dot_nt(qs[h], k)
        if diagonal:
            row = lax.broadcasted_iota(jnp.int32, s.shape, 0)
            col = lax.broadcasted_iota(jnp.int32, s.shape, 1)
            s = jnp.where(row >= col, s, -jnp.inf)
        m_new = jnp.maximum(m, jnp.max(s, axis=1, keepdims=True))
        alpha = jnp.exp2(m - m_new)
        p = jnp.exp2(s - m_new)
        l = alpha * l + jnp.sum(p, axis=1, keepdims=True)
        acc = alpha * acc + _dot(p.astype(BF16), v)
        return m_new, l, acc

    def step(j, carry, diagonal):
        return tuple(update(j, diagonal, h, *carry[h]) for h in range(heads))

    init = tuple((jnp.full((blk, 1), -jnp.inf, F32), jnp.zeros((blk, 1), F32), jnp.zeros((blk, V_HEAD), F32))
                 for _ in range(heads))
    carry = lax.fori_loop(0, qi, lambda j, c: step(j, c, False), init)
    carry = step(qi, carry, True)
    for h in range(heads):
        _, l, acc = carry[h]
        o_ref[:, h * V_HEAD:(h + 1) * V_HEAD] = (acc / l).astype(o_ref.dtype)


def flash_attention(q, k, v, *, batch, seq, blk, heads):
    nq = seq // blk
    return pl.pallas_call(
        functools.partial(_flash_body, blk=blk, heads=heads),
        grid=(batch, MLA_HEADS // heads, nq),
        in_specs=[pl.BlockSpec((blk, heads * Q_SLOT), lambda b, h, i: (b * nq + i, h)),
                  pl.BlockSpec((seq, heads * Q_SLOT), lambda b, h, i: (b, h)),
                  pl.BlockSpec((seq, heads * V_HEAD), lambda b, h, i: (b, h))],
        out_specs=pl.BlockSpec((blk, heads * V_HEAD), lambda b, h, i: (b * nq + i, h)),
        out_shape=jax.ShapeDtypeStruct((batch * seq, MLA_HEADS * V_HEAD), BF16),
        compiler_params=_cparams("parallel", "parallel", "parallel"),
        name="flash_attention",
    )(q, k, v)


def _decode_body(pt_ref, q_ref, qpe_ref, newc_ref, newk_ref, ckv_hbm, kpe_hbm, o_ref,
                 kbuf, pbuf, sems, m_sc, l_sc, acc_sc, *, layer, pages_per_step, n_steps, n_new):
    pps = pages_per_step
    c = pl.program_id(1)
    n_total = pl.num_programs(0) * n_steps
    g = pl.program_id(0) * n_steps + c
    slot = lax.rem(g, 2)
    nxt = lax.rem(g + 1, n_total)

    def page_copies(chunk, slot_, i):
        pg = pt_ref[chunk * pps + i]
        return (pltpu.make_async_copy(ckv_hbm.at[layer, pg], kbuf.at[slot_, i], sems.at[slot_, 0]),
                pltpu.make_async_copy(kpe_hbm.at[layer, pg], pbuf.at[slot_, i], sems.at[slot_, 1]))

    @pl.when(g == 0)
    def _():
        for i in range(pps):
            for cp in page_copies(0, 0, i):
                cp.start(priority=i % 2)

    for i in range(pps):
        for cp in page_copies(g, slot, i):
            cp.wait()

    @pl.when(c == 0)
    def _():
        m_sc[...] = jnp.full(m_sc.shape, -jnp.inf, F32)
        l_sc[...] = jnp.zeros(l_sc.shape, F32)
        acc_sc[...] = jnp.zeros(acc_sc.shape, F32)

    q = q_ref[...]
    qp = qpe_ref[:, :QK_ROPE]
    ks, ss = [], []
    for i in range(pps):
        for cp in page_copies(nxt, 1 - slot, i):
            cp.start(priority=i % 2)
        k = kbuf[slot, i].astype(BF16)
        kp_t = pbuf[slot, i].astype(BF16)
        ks.append(k)
        ss.append(_dot_nt(q, k) + _dot(qp, kp_t))
    s = jnp.concatenate(ss, axis=1)
    m_prev = m_sc[:, :1]
    l_prev = l_sc[:, :1]
    m_new = jnp.maximum(m_prev, jnp.max(s, axis=1, keepdims=True))
    alpha = jnp.exp2(m_prev - m_new)
    p = jnp.exp2(s - m_new)
    l_new = alpha * l_prev + jnp.sum(p, axis=1, keepdims=True)
    page = ks[0].shape[0]
    pv = _dot(p[:, :page].astype(BF16), ks[0])
    for i in range(1, pps):
        pv = pv + _dot(p[:, i * page:(i + 1) * page].astype(BF16), ks[i])
    acc_new = alpha * acc_sc[...] + pv
    m_sc[...] = jnp.broadcast_to(m_new, m_sc.shape)
    l_sc[...] = jnp.broadcast_to(l_new, l_sc.shape)
    acc_sc[...] = acc_new

    @pl.when(c == n_steps - 1)
    def _():
        qf = q.astype(F32)
        qpf = qp.astype(F32)
        kn = newc_ref[...]
        kpn = newk_ref[:, :QK_ROPE]
        row = lax.broadcasted_iota(jnp.int32, (q.shape[0], 1), 0)
        sj = []
        for j in range(n_new):
            v = (jnp.sum(qf * kn[j:j + 1, :], axis=1, keepdims=True)
                 + jnp.sum(qpf * kpn[j:j + 1, :], axis=1, keepdims=True))
            sj.append(jnp.where(row >= j * MLA_HEADS, v, -jnp.inf))
        m_fin = m_new
        for v in sj:
            m_fin = jnp.maximum(m_fin, v)
        a2 = jnp.exp2(m_new - m_fin)
        l_fin = a2 * l_new
        acc_fin = a2 * acc_new
        for j in range(n_new):
            pj = jnp.exp2(sj[j] - m_fin)
            l_fin = l_fin + pj
            acc_fin = acc_fin + pj * kn[j:j + 1, :]
        o_ref[...] = (acc_fin / l_fin).astype(o_ref.dtype)

    @pl.when(g == n_total - 1)
    def _():
        for i in range(pps):
            for cp in page_copies(nxt, 1 - slot, i):
                cp.wait()


def decode_attention(q_lat, q_pe, cache_ckv, cache_kpe, layer, page_table, new_ckv, new_kpe, *, pages_per_step):
    nb, rows, _ = q_lat.shape
    n_pages = page_table.shape[1]
    page = cache_ckv.shape[2]
    pps = pages_per_step
    n_steps = n_pages // pps
    n_new = rows // MLA_HEADS

    assert n_pages == n_steps * pps
    in_specs = [pl.BlockSpec((None, rows, KV_LORA), lambda b, c, pt: (b, 0, 0)),
                pl.BlockSpec((None, rows, LANES), lambda b, c, pt: (b, 0, 0)),
                pl.BlockSpec((None, 8, KV_LORA), lambda b, c, pt: (b, 0, 0)),
                pl.BlockSpec((None, 8, LANES), lambda b, c, pt: (b, 0, 0)),
                pl.BlockSpec(memory_space=pl.ANY), pl.BlockSpec(memory_space=pl.ANY)]
    grid_spec = pltpu.PrefetchScalarGridSpec(
        num_scalar_prefetch=1,
        grid=(nb, n_steps),
        in_specs=in_specs,
        out_specs=pl.BlockSpec((None, rows, KV_LORA), lambda b, c, pt: (b, 0, 0)),
        scratch_shapes=[pltpu.VMEM((2, pps, page, KV_LORA), F32), pltpu.VMEM((2, pps, QK_ROPE, page), F32),
                        pltpu.SemaphoreType.DMA((2, 2)),
                        pltpu.VMEM((rows, LANES), F32), pltpu.VMEM((rows, LANES), F32),
                        pltpu.VMEM((rows, KV_LORA), F32)],
    )
    return pl.pallas_call(
        functools.partial(_decode_body, layer=layer, pages_per_step=pps, n_steps=n_steps, n_new=n_new),
        grid_spec=grid_spec,
        out_shape=jax.ShapeDtypeStruct((nb, rows, KV_LORA), BF16),
        compiler_params=_cparams("arbitrary", "arbitrary"),
        name="decode_attention",
    )(page_table.reshape(-1), q_lat, q_pe, new_ckv, new_kpe, cache_ckv, cache_kpe)


def _group_rmsnorm(y, g):
    gw = SSM_INNER // SSM_GROUPS
    parts = []
    for i in range(SSM_GROUPS):
        yg = y[:, i * gw:(i + 1) * gw]
        parts.append(yg * lax.rsqrt(jnp.mean(yg * yg, axis=-1, keepdims=True) + EPS))
    return jnp.concatenate(parts, axis=1) * g


def _ssd_state_update(st, bm, xd, acs_x):
    last = acs_x.shape[0] - 1
    xde = (xd * jnp.exp(acs_x[last:last + 1, :] - acs_x)).astype(BF16)
    gw = SSM_INNER // SSM_GROUPS
    upd = [_dot_tn(bm[:, g * SSM_STATE:(g + 1) * SSM_STATE].astype(BF16), xde[:, g * gw:(g + 1) * gw])
           for g in range(SSM_GROUPS)]
    return st * jnp.exp(acs_x[last:last + 1, :]) + jnp.concatenate(upd, axis=1)


def _ssd_y_off(st, cm, acs_x):
    gw = SSM_INNER // SSM_GROUPS
    parts = [_dot(cm[:, g * SSM_STATE:(g + 1) * SSM_STATE].astype(BF16), st[:, g * gw:(g + 1) * gw].astype(BF16))
             for g in range(SSM_GROUPS)]
    return jnp.concatenate(parts, axis=1) * jnp.exp(acs_x)


def _ssd_prompt_body(z_ref, x_ref, bc_ref, dt_ref, cw_ref, cb_ref, dtb_ref, alog_ref, alogx_ref, dvec_ref, gn_ref,
                     e_ref, tril_ref, y_ref, st_ref, xp_sc, st_sc, *, n_chunks):
    L = SSM_CHUNK
    c = pl.program_id(1)

    @pl.when(c == 0)
    def _():
        xp_sc[0:8, :] = jnp.zeros((8, xp_sc.shape[1]), F32)
        st_sc[...] = jnp.zeros(st_sc.shape, F32)

    @pl.when(c > 0)
    def _():
        xp_sc[0:8, :] = xp_sc[L:L + 8, :]

    xp_sc[8:8 + L, 0:SSM_INNER] = x_ref[...]
    xp_sc[8:8 + L, SSM_INNER:] = bc_ref[...]
    conv = xp_sc[pl.ds(8 - (SSM_CONV - 1), L), :] * cw_ref[0:1, :]
    for k in range(1, SSM_CONV):
        conv = conv + xp_sc[pl.ds(8 - (SSM_CONV - 1) + k, L), :] * cw_ref[k:k + 1, :]
    xbc = _silu(conv + cb_ref[...])
    xs = xbc[:, :SSM_INNER]
    bm = xbc[:, SSM_INNER:SSM_INNER + SSM_GROUPS * SSM_STATE]
    cm = xbc[:, SSM_INNER + SSM_GROUPS * SSM_STATE:]

    hi = lax.Precision.HIGHEST
    dt = _softplus(dt_ref[...] + dtb_ref[...])
    tril = tril_ref[...]
    acs = _dot(tril, dt * (-jnp.exp(alog_ref[...])), precision=hi)
    acs_t = acs.T
    dt_x = _dot(dt, e_ref[...], precision=hi)
    acs_x = _dot(tril, dt_x * (-jnp.exp(alogx_ref[...])), precision=hi)
    xd = xs * dt_x
    xd_b = xd.astype(BF16)

    row = lax.broadcasted_iota(jnp.int32, (L, L), 0)
    col = lax.broadcasted_iota(jnp.int32, (L, L), 1)
    causal = row >= col
    lane = lax.broadcasted_iota(jnp.int32, (L, LANES), 1)
    heads_per_group = SSM_HEADS // SSM_GROUPS
    y_parts = []
    cb = [_dot_nt(cm[:, g * SSM_STATE:(g + 1) * SSM_STATE].astype(BF16),
                  bm[:, g * SSM_STATE:(g + 1) * SSM_STATE].astype(BF16)) for g in range(SSM_GROUPS)]
    for pair in range(SSM_HEADS // 2):
        xd_pair = xd_b[:, pair * LANES:(pair + 1) * LANES]
        outs = []
        for h in (2 * pair, 2 * pair + 1):
            decay = jnp.exp(jnp.where(causal, acs[:, h:h + 1] - acs_t[h:h + 1, :], -jnp.inf))
            outs.append(_dot((cb[h // heads_per_group] * decay).astype(BF16), xd_pair))
        y_parts.append(jnp.where(lane < SSM_HEAD_DIM, outs[0], outs[1]))
    y_diag = jnp.concatenate(y_parts, axis=1)

    st = st_sc[...]
    y = (y_diag + _ssd_y_off(st, cm, acs_x)) + dvec_ref[...] * xs
    y = y * _silu(z_ref[...])
    y_ref[...] = _group_rmsnorm(y, gn_ref[...]).astype(y_ref.dtype)
    st_new = _ssd_state_update(st, bm, xd, acs_x)
    st_sc[...] = st_new

    @pl.when(c == n_chunks - 1)
    def _():
        st_ref[...] = st_new.T


def _ssd_consts(conv_w, conv_b, dt_bias, a_log, d_vec, g_norm):
    pad = LANES - SSM_HEADS
    e_np = np.zeros((LANES, SSM_INNER), np.float32)
    for hh in range(SSM_HEADS):
        e_np[hh, hh * SSM_HEAD_DIM:(hh + 1) * SSM_HEAD_DIM] = 1.0
    e_mat = jnp.asarray(e_np)
    return dict(
        cw=conv_w, cb=conv_b.reshape(1, -1),
        dtb=jnp.pad(dt_bias, (0, pad)).reshape(1, LANES),
        alog=jnp.pad(a_log, (0, pad)).reshape(1, LANES),
        alogx=jnp.repeat(a_log, SSM_HEAD_DIM).reshape(1, SSM_INNER),
        dvec=jnp.repeat(d_vec, SSM_HEAD_DIM).reshape(1, SSM_INNER),
        gn=g_norm.reshape(1, SSM_INNER), e=e_mat)


def _full(shape):
    nd = len(shape)
    return pl.BlockSpec(shape, lambda *_: (0,) * nd)


def ssd_prompt(proj, slabs, consts, *, batch, seq):
    L = SSM_CHUNK
    nc = seq // L
    cdim = SSM_INNER + 2 * SSM_GROUPS * SSM_STATE
    tril = jnp.asarray(np.tril(np.ones((L, L), np.float32)))

    def rows(width, col):
        return pl.BlockSpec((L, width), lambda b, c: (b * nc + c, col // width))

    k = consts
    return pl.pallas_call(
        functools.partial(_ssd_prompt_body, n_chunks=nc),
        grid=(batch, nc),
        in_specs=[rows(SSM_INNER, COL_Z), rows(SSM_INNER, COL_X), rows(2 * SSM_GROUPS * SSM_STATE, COL_BC),
                  pl.BlockSpec((L, LANES), lambda b, c: (b * nc + c, SLAB_DT)),
                  _full((SSM_CONV, cdim)), _full((1, cdim)), _full((1, LANES)), _full((1, LANES)),
                  _full((1, SSM_INNER)), _full((1, SSM_INNER)), _full((1, SSM_INNER)),
                  _full((LANES, SSM_INNER)), _full((L, L))],
        out_specs=[pl.BlockSpec((L, SSM_INNER), lambda b, c: (b * nc + c, 0)),
                   pl.BlockSpec((None, SSM_INNER, SSM_STATE), lambda b, c: (b, 0, 0))],
        out_shape=[jax.ShapeDtypeStruct((batch * seq, SSM_INNER), BF16),
                   jax.ShapeDtypeStruct((batch, SSM_INNER, SSM_STATE), F32)],
        scratch_shapes=[pltpu.VMEM((L + 8, cdim), F32), pltpu.VMEM((SSM_STATE, SSM_INNER), F32)],
        compiler_params=_cparams("parallel", "arbitrary"),
        name="ssd_prompt",
    )(proj, proj, proj, slabs, k["cw"], k["cb"], k["dtb"], k["alog"], k["alogx"], k["dvec"], k["gn"], k["e"], tril)


def _sample_mixer_body(z_ref, dt_ref, scb_ref, xp_ref, up_c_ref, up_v_ref, st_in_ref,
                       cw_ref, cb_ref, dtb_ref, alogx_ref, dvec_ref, gn_ref, e_ref, scw_ref,
                       y_ref, ysc_ref, st_ref, u_out_ref, bpad_sc, xdpad_sc, u_sc, *, n_new, group):
    @pl.when(pl.program_id(0) == 0)
    def _():
        bpad_sc[...] = jnp.zeros(bpad_sc.shape, F32)
        xdpad_sc[...] = jnp.zeros(xdpad_sc.shape, F32)

    for i in range(group):
        _sample_mixer_one(z_ref.at[i], dt_ref.at[i], scb_ref.at[i], xp_ref.at[i], up_c_ref.at[i], up_v_ref.at[i],
                          st_in_ref.at[i], cw_ref, cb_ref, dtb_ref, alogx_ref, dvec_ref, gn_ref, e_ref, scw_ref,
                          y_ref.at[i], ysc_ref.at[i], st_ref.at[i], u_out_ref.at[i],
                          bpad_sc.at[i], xdpad_sc.at[i], u_sc.at[i], n_new=n_new)


def _sample_mixer_one(z_ref, dt_ref, scb_ref, xp_ref, up_c_ref, up_v_ref, st_in_ref,
                      cw_ref, cb_ref, dtb_ref, alogx_ref, dvec_ref, gn_ref, e_ref, scw_ref,
                      y_ref, ysc_ref, st_ref, u_out_ref, bpad_sc, xdpad_sc, u_sc, *, n_new):
    R = 8
    conv = xp_ref[pl.ds(8 - (SSM_CONV - 1), R), :] * cw_ref[0:1, :]
    for k in range(1, SSM_CONV):
        conv = conv + xp_ref[pl.ds(8 - (SSM_CONV - 1) + k, R), :] * cw_ref[k:k + 1, :]
    xbc = _silu(conv + cb_ref[...])
    xs = xbc[:, :SSM_INNER]
    bm = xbc[:, SSM_INNER:SSM_INNER + SSM_GROUPS * SSM_STATE]
    cm = xbc[:, SSM_INNER + SSM_GROUPS * SSM_STATE:]

    hi = lax.Precision.HIGHEST
    rowl = lax.broadcasted_iota(jnp.int32, (R, LANES), 0)
    dt = jnp.where(rowl < n_new, _softplus(dt_ref[...] + dtb_ref[...]), 0.0)
    dt_x = _dot(dt, e_ref[...], precision=hi)
    da_x = dt_x * (-jnp.exp(alogx_ref[...]))
    rowx = lax.broadcasted_iota(jnp.int32, (R, SSM_INNER), 0)
    acs_x = jnp.zeros((R, SSM_INNER), F32)
    for s in range(n_new):
        acs_x = acs_x + jnp.where(rowx >= s, da_x[s:s + 1, :], 0.0)
    xd = xs * dt_x

    gw = SSM_INNER // SSM_GROUPS
    lanex = lax.broadcasted_iota(jnp.int32, (R, SSM_INNER), 1)
    y_diag = jnp.zeros((R, SSM_INNER), F32)
    for s in range(n_new):
        cbs = [jnp.sum(cm[:, g * SSM_STATE:(g + 1) * SSM_STATE] * bm[s:s + 1, g * SSM_STATE:(g + 1) * SSM_STATE],
                       axis=1, keepdims=True) for g in range(SSM_GROUPS)]
        cb_x = jnp.where(lanex < gw, cbs[0], cbs[1])
        decay = jnp.exp(jnp.where(rowx >= s, acs_x - acs_x[s:s + 1, :], -jnp.inf))
        y_diag = y_diag + (cb_x * decay) * xd[s:s + 1, :]

    st = st_in_ref[...].T
    y_off_parts = [_dot(cm[:, g * SSM_STATE:(g + 1) * SSM_STATE], st[:, g * gw:(g + 1) * gw])
                   for g in range(SSM_GROUPS)]
    y_off = jnp.concatenate(y_off_parts, axis=1) * jnp.exp(acs_x)
    y = (y_diag + y_off) + dvec_ref[...] * xs
    y = y * _silu(z_ref[...])
    y_ref[...] = _group_rmsnorm(y, gn_ref[...])

    bpad_sc[0:R, :] = bm
    xdpad_sc[0:R, :] = xd * jnp.exp(acs_x[R - 1:R, :] - acs_x)
    upd = [_dot_tn(bpad_sc[:, g * SSM_STATE:(g + 1) * SSM_STATE], xdpad_sc[:, g * gw:(g + 1) * gw])
           for g in range(SSM_GROUPS)]
    st_new = st * jnp.exp(acs_x[R - 1:R, :]) + jnp.concatenate(upd, axis=1)
    st_ref[...] = st_new.T

    u_sc[...] = up_c_ref[...] * up_v_ref[...]
    sconv = u_sc[pl.ds(8 - (SC_CONV - 1), R), :] * scw_ref[0:1, :]
    for k in range(1, SC_CONV):
        sconv = sconv + u_sc[pl.ds(8 - (SC_CONV - 1) + k, R), :] * scw_ref[k:k + 1, :]
    ysc_ref[...] = scb_ref[...] * sconv
    u_out_ref[...] = u_sc[8:16, :]


def sample_mixer(proj_s, slabs_s, xp_s, up_c, up_v, st_all, layer, consts, sc_conv_w, *, n_new, group):
    nb = proj_s.shape[0]
    cdim = SSM_INNER + 2 * SSM_GROUPS * SSM_STATE
    k = consts
    gs = group

    def rows(width, col):
        return pl.BlockSpec((gs, 8, width), lambda b: (b, 0, col // width))

    def per_sample(*tail):
        return pl.BlockSpec((gs,) + tail, lambda b: (b,) + (0,) * len(tail))

    st_spec = pl.BlockSpec((None, gs, SSM_INNER, SSM_STATE), lambda b: (layer, b, 0, 0))
    args = [proj_s, slabs_s, proj_s, xp_s, up_c, up_v, st_all,
            k["cw"], k["cb"], k["dtb"], k["alogx"], k["dvec"], k["gn"], k["e"], sc_conv_w]
    in_specs = [rows(SSM_INNER, COL_Z), pl.BlockSpec((gs, 8, LANES), lambda b: (b, 0, SLAB_DT)),
                rows(SC_WIDTH, COL_SCB),
                per_sample(16, cdim), per_sample(16, SC_WIDTH), per_sample(16, SC_WIDTH), st_spec,
                _full((SSM_CONV, cdim)), _full((1, cdim)), _full((1, LANES)),
                _full((1, SSM_INNER)), _full((1, SSM_INNER)), _full((1, SSM_INNER)),
                _full((LANES, SSM_INNER)), _full((SC_CONV, SC_WIDTH))]
    return pl.pallas_call(
        functools.partial(_sample_mixer_body, n_new=n_new, group=gs),
        grid=(nb // gs,),
        in_specs=in_specs,
        out_specs=[per_sample(8, SSM_INNER), per_sample(8, SC_WIDTH), per_sample(SSM_INNER, SSM_STATE),
                   per_sample(8, SC_WIDTH)],
        out_shape=[jax.ShapeDtypeStruct((nb, 8, SSM_INNER), F32), jax.ShapeDtypeStruct((nb, 8, SC_WIDTH), F32),
                   jax.ShapeDtypeStruct((nb, SSM_INNER, SSM_STATE), F32),
                   jax.ShapeDtypeStruct((nb, 8, SC_WIDTH), F32)],
        scratch_shapes=[pltpu.VMEM((gs, LANES, SSM_GROUPS * SSM_STATE), F32), pltpu.VMEM((gs, LANES, SSM_INNER), F32),
                        pltpu.VMEM((gs, 16, SC_WIDTH), F32)],
        compiler_params=_cparams("arbitrary"),
        name="sample_mixer",
    )(*args)


def _sconv_prompt_body(scb_ref, scc_ref, scv_ref, pc_ref, pv_ref, w_ref, y_ref, tail_ref, u_sc, *, ts):
    i = pl.program_id(1)
    hist = pc_ref[...] * pv_ref[...]
    u_sc[0:8, :] = jnp.where(i == 0, 0.0, hist)
    u_sc[8:8 + ts, :] = scc_ref[...] * scv_ref[...]
    conv = u_sc[pl.ds(8 - (SC_CONV - 1), ts), :] * w_ref[0:1, :]
    for k in range(1, SC_CONV):
        conv = conv + u_sc[pl.ds(8 - (SC_CONV - 1) + k, ts), :] * w_ref[k:k + 1, :]
    y_ref[...] = (scb_ref[...] * conv).astype(y_ref.dtype)
    tail_ref[...] = u_sc[ts:ts + 8, :]


def sconv_prompt(proj, sc_conv_w, *, batch, seq, ts):
    nt = seq // ts
    w = SC_WIDTH

    def rows(col):
        return pl.BlockSpec((ts, w), lambda b, i: (b * nt + i, col // w))

    def prev(col):
        return pl.BlockSpec((8, w), lambda b, i: (jnp.maximum((b * nt + i) * (ts // 8) - 1, 0), col // w))

    return pl.pallas_call(
        functools.partial(_sconv_prompt_body, ts=ts),
        grid=(batch, nt),
        in_specs=[rows(COL_SCB), rows(COL_SCC), rows(COL_SCV), prev(COL_SCC), prev(COL_SCV), _full((SC_CONV, w))],
        out_specs=[pl.BlockSpec((ts, w), lambda b, i: (b * nt + i, 0)),
                   pl.BlockSpec((None, 8, w), lambda b, i: (b, 0, 0))],
        out_shape=[jax.ShapeDtypeStruct((batch * seq, w), BF16), jax.ShapeDtypeStruct((batch, 8, w), F32)],
        scratch_shapes=[pltpu.VMEM((ts + 8, w), F32)],
        compiler_params=_cparams("parallel", "arbitrary"),
        name="sconv_prompt",
    )(proj, proj, proj, proj, proj, sc_conv_w)


def _merge_body(h_ref, yap_ref, ysp_ref, ycp_ref, yas_ref, yss_ref, ycs_ref,
                wga_ref, wgb_ref, wgc_ref, ba_ref, bb_ref, bc_ref, wa_ref, wb_ref, wc_ref, o_ref,
                wg_sc, wb_sc, *, n_prompt_tiles):
    i = pl.program_id(1)

    @pl.when(i == 0)
    def _():
        for n, r in enumerate((wga_ref, wgb_ref, wgc_ref)):
            wg_sc[n] = r[...].astype(BF16)
        for n, r in enumerate((wa_ref, wb_ref, wc_ref)):
            wb_sc[n] = r[...].astype(BF16)

    def merge(y_refs):
        h = h_ref[...]
        out = None
        for n, (y_ref, b_ref) in enumerate(zip(y_refs, (ba_ref, bb_ref, bc_ref))):
            term = _sigmoid(_dot(h, wg_sc[n]) + b_ref[...]) * _dot(y_ref[...], wb_sc[n])
            out = term if out is None else out + term
        o_ref[...] = out.astype(o_ref.dtype)

    @pl.when(i < n_prompt_tiles)
    def _():
        merge((yap_ref, ysp_ref, ycp_ref))

    @pl.when(i >= n_prompt_tiles)
    def _():
        merge((yas_ref, yss_ref, ycs_ref))


def gated_merge(h, y_prompt, y_sample, w_gate, b_gate, w_branches, layer, *, tm, tn):
    m, d = h.shape
    mp, kb = y_prompt[0].shape
    nb = d // tn
    npt = mp // tm
    b2 = b_gate.reshape(b_gate.shape[0], 1, -1)
    hspec = pl.BlockSpec((tm, d), lambda j, i: (i, 0))
    pspec = pl.BlockSpec((tm, kb), lambda j, i: (jnp.minimum(i, npt - 1), 0))
    once = pl.Buffered(1)
    sspec = pl.BlockSpec((tm, kb), lambda j, i: (jnp.maximum(i - npt, 0), 0), pipeline_mode=once)

    def wcol(rows_, off):
        return pl.BlockSpec((None, rows_, tn), lambda j, i: (layer, 0, off * nb + j), pipeline_mode=once)

    return pl.pallas_call(
        functools.partial(_merge_body, n_prompt_tiles=npt),
        grid=(nb, m // tm),
        in_specs=[hspec, pspec, pspec, pspec, sspec, sspec, sspec,
                  wcol(d, 0), wcol(d, 1), wcol(d, 2), wcol(1, 0), wcol(1, 1), wcol(1, 2),
                  wcol(kb, 0), wcol(kb, 0), wcol(kb, 0)],
        out_specs=pl.BlockSpec((tm, tn), lambda j, i: (i, j)),
        out_shape=jax.ShapeDtypeStruct((m, d), BF16),
        scratch_shapes=[pltpu.VMEM((3, d, tn), BF16), pltpu.VMEM((3, kb, tn), BF16)],
        compiler_params=_cparams("parallel", "arbitrary"),
        name="gated_merge",
    )(h, *y_prompt, *y_sample, w_gate, w_gate, w_gate, b2, b2, b2, *w_branches)


def _rope_tables(pos):
    half = QK_ROPE // 2
    inv = ROPE_THETA ** (-jnp.arange(half, dtype=F32) / half)
    ang = pos.astype(F32)[:, None] * inv[None, :]
    c, s = jnp.cos(ang), jnp.sin(ang)
    z = jnp.zeros((pos.shape[0], LANES - QK_ROPE), F32)
    return jnp.concatenate([c, c, z], axis=1), jnp.concatenate([-s, s, z], axis=1)


def _in_proj_row_offsets(layer):
    sizes = (Q_LORA, KV_LORA, QK_ROPE, SSM_INNER, SSM_INNER + 2 * SSM_GROUPS * SSM_STATE, SSM_HEADS,
             SC_WIDTH, SC_WIDTH, SC_WIDTH)
    q_c, kv_c, k_pe, z, xbc, dt, sc_b, sc_c, sc_v = (int(v) for v in np.cumsum((0,) + sizes)[:-1])
    d_in = int(sum(sizes))
    main = []
    for first, width in ((sc_b, SC_WIDTH), (sc_c, SC_WIDTH), (sc_v, SC_WIDTH), (z, SSM_INNER),
                         (xbc, SSM_INNER + 2 * SSM_GROUPS * SSM_STATE), (q_c, Q_LORA), (kv_c, KV_LORA)):
        main += [first + t for t in range(0, width, PROJ_TILE)]
    assert len(main) * PROJ_TILE == D_IN_PAD
    return tuple(layer * d_in + r for r in main), tuple(layer * d_in + r for r in (k_pe, dt))


def kernel(x_prompt, x_sample, cache_ckv, cache_kpe, state_ssm, state_mconv, state_sconv, page_table, g_attn_norm, w_in, g_q_a, w_q_b, g_kv_a, w_kv_b, ssm_conv_w, ssm_conv_b, ssm_dt_bias, ssm_a_log, ssm_d, g_ssm_norm, sc_conv_w, w_gate, b_gate, w_br_attn, w_br_ssm, w_br_sc, w_o, g_ffn_norm, w_ff_gate, w_ff_up, w_ff_down, w_router, w_e_gate, w_e_up, w_e_down, g_final):
    bp, tp, d = x_prompt.shape
    bs, ts, _ = x_sample.shape
    depth = w_in.shape[0]
    mp, ms = bp * tp, bs * ts
    m = mp + ms
    n_past = page_table.shape[1] * cache_ckv.shape[2]
    tm = m // 8
    tmh = m // 16
    tmw = m // 4
    cdim = SSM_INNER + 2 * SSM_GROUPS * SSM_STATE

    x = jnp.concatenate([x_prompt.reshape(mp, d), x_sample.reshape(ms, d)], axis=0)
    pos = jnp.concatenate([jnp.tile(jnp.arange(tp), bp), jnp.tile(n_past + jnp.arange(ts), bs)])
    cos, sin = _rope_tables(pos)
    cache_kpe_t = jnp.swapaxes(cache_kpe, 2, 3)
    st_all = state_ssm.reshape(depth, bs, SSM_INNER, SSM_STATE)
    w_in_t = jnp.swapaxes(w_in, 1, 2).reshape(-1, d)

    outs = {k: [] for k in ("p_ckv", "p_kpe", "p_ssm", "p_mconv", "p_sconv", "s_ckv", "s_kpe", "s_ssm", "s_mconv", "s_sconv")}
    for l in range(depth):
        wq = w_q_b[l]
        w_q_slots = jnp.concatenate([wq, jnp.zeros(wq.shape[:2] + (Q_SLOT - wq.shape[2],), wq.dtype)], axis=2)
        w_q_slots = w_q_slots.reshape(Q_LORA, MLA_HEADS * Q_SLOT).astype(BF16)
        wkv = w_kv_b[l]
        w_kv_flat = jnp.concatenate([wkv[..., :QK_NOPE].reshape(KV_LORA, -1), wkv[..., QK_NOPE:].reshape(KV_LORA, -1)],
                                    axis=1).astype(BF16)
        w_uk_t = jnp.transpose(wkv[..., :QK_NOPE], (1, 2, 0)).astype(BF16)
        w_uv = jnp.transpose(wkv[..., QK_NOPE:], (1, 0, 2)).astype(BF16)
        consts = _ssd_consts(ssm_conv_w[l], ssm_conv_b[l], ssm_dt_bias[l], ssm_a_log[l], ssm_d[l], g_ssm_norm[l])

        h = rmsnorm(x, g_attn_norm[l], BF16, tm)
        rows_main, rows_slabs = _in_proj_row_offsets(l)
        proj = in_proj(h, w_in_t, rows_main, tm=tmw, tn=PROJ_TILE)
        slabs = in_proj(h, w_in_t, rows_slabs, tm=tmw, tn=LANES)
        qn, ckv, kpe = mla_prep(proj, slabs, g_q_a[l], g_kv_a[l], cos, sin, tm=tm)
        q = qproj(qn, w_q_slots, cos, sin, tm=tm)

        k_full, v_full = kv_expand(ckv, kpe, w_kv_flat, rows=mp, tm=1024)
        ya_p = flash_attention(q, k_full, v_full, batch=bp, seq=tp, blk=1024, heads=2)

        q_lat = blockdiag_matmul(q, w_uk_t, tm=ms, row_block=mp // ms, col_block0=0, col_stride=2)
        q_pe_s = q[mp:].reshape(ms, MLA_HEADS, Q_SLOT)[:, :, QK_NOPE:].reshape(bs, ts * MLA_HEADS, LANES)
        ckv_s = ckv[mp:].reshape(bs, ts, KV_LORA)
        kpe_s = kpe[mp:].reshape(bs, ts, LANES)
        new_c = jnp.pad(ckv_s, ((0, 0), (0, 8 - ts), (0, 0)))
        new_k = jnp.pad(kpe_s, ((0, 0), (0, 8 - ts), (0, 0)))
        o_lat = decode_attention(q_lat.reshape(bs, ts * MLA_HEADS, KV_LORA), q_pe_s, cache_ckv, cache_kpe_t, l,
                                 page_table, new_c, new_k, pages_per_step=32)
        ya_s = blockdiag_matmul(o_lat.reshape(ms, MLA_HEADS * KV_LORA), w_uv, tm=ms, row_block=0, col_block0=0,
                                col_stride=1)

        ys_p, ssm_p = ssd_prompt(proj, slabs, consts, batch=bp, seq=tp)
        yc_p, u_tail_p = sconv_prompt(proj, sc_conv_w[l], batch=bp, seq=tp, ts=512)

        proj_s = jnp.pad(proj[mp:].reshape(bs, ts, D_IN_PAD), ((0, 0), (0, 8 - ts), (0, 0)))
        slabs_s = jnp.pad(slabs[mp:].reshape(bs, ts, D_IN_SLABS), ((0, 0), (0, 8 - ts), (0, 0)))
        xbc_s = proj_s[:, :ts, COL_X:COL_X + cdim]
        xp_s = jnp.concatenate([jnp.zeros((bs, 8 - (SSM_CONV - 1), cdim), F32), state_mconv[l], xbc_s,
                                jnp.zeros((bs, 8 - ts, cdim), F32)], axis=1)
        zpad = jnp.zeros((bs, 8 - (SC_CONV - 1), SC_WIDTH), F32)
        zend = jnp.zeros((bs, 8 - ts, SC_WIDTH), F32)
        up_c = jnp.concatenate([zpad, state_sconv[l], proj_s[:, :ts, COL_SCC:COL_SCC + SC_WIDTH], zend], axis=1)
        up_v = jnp.concatenate([zpad, jnp.ones_like(state_sconv[l]), proj_s[:, :ts, COL_SCV:COL_SCV + SC_WIDTH], zend], axis=1)
        ys_s, yc_s, ssm_s, u_new_s = sample_mixer(proj_s, slabs_s, xp_s, up_c, up_v, st_all, l, consts, sc_conv_w[l],
                                                  n_new=ts, group=4)

        y_sample = (ya_s, ys_s[:, :ts].reshape(ms, SSM_INNER).astype(BF16), yc_s[:, :ts].reshape(ms, SC_WIDTH).astype(BF16))
        merged = gated_merge(h, (ya_p, ys_p, yc_p), y_sample, w_gate, b_gate, (w_br_attn, w_br_ssm, w_br_sc), l,
                             tm=ms, tn=512)
        x = matmul(merged, w_o, tm=tm, tn=512, res=x, w_index=l)

        outs["p_ckv"].append(ckv[:mp].reshape(bp, tp, KV_LORA))
        outs["p_kpe"].append(kpe[:mp, :QK_ROPE].reshape(bp, tp, QK_ROPE))
        outs["p_ssm"].append(ssm_p.reshape(bp, SSM_HEADS, SSM_HEAD_DIM, SSM_STATE))
        outs["p_mconv"].append(jnp.stack([proj[(b + 1) * tp - (SSM_CONV - 1):(b + 1) * tp, COL_X:COL_X + cdim]
                                          for b in range(bp)]))
        outs["p_sconv"].append(u_tail_p[:, 8 - (SC_CONV - 1):])
        outs["s_ckv"].append(ckv_s)
        outs["s_kpe"].append(kpe_s[:, :, :QK_ROPE])
        outs["s_ssm"].append(ssm_s.reshape(bs, SSM_HEADS, SSM_HEAD_DIM, SSM_STATE))
        outs["s_mconv"].append(xp_s[:, 8 + ts - (SSM_CONV - 1):8 + ts])
        outs["s_sconv"].append(u_new_s[:, ts - (SC_CONV - 1):ts])

        i = l // 2
        if l % 2 == 0:
            h2 = rmsnorm(x, g_ffn_norm[l], BF16, tm)
            hdn = swiglu_up(h2, w_ff_gate[i], w_ff_up[i], tm=tmw, tf=512)
            x = matmul(hdn, w_ff_down[i].astype(BF16), tm=tm, tn=256, res=x, rows_outer=True)
        else:
            h2, gate = rmsnorm_router(x, g_ffn_norm[l], w_router[i], tm)
            hdn = moe_up(h2, w_e_gate[i].astype(BF16), w_e_up[i].astype(BF16), gate, tm=tm)
            wd = w_e_down[i].reshape(-1, d).astype(BF16)
            x = matmul(hdn, wd, tm=tmh, tn=256, res=x, rows_outer=True)

    y_p, y_s = rmsnorm_split(x, g_final, rows_prompt=mp, tm=ms)
    st = {k: jnp.stack(v, axis=0) for k, v in outs.items()}
    return (y_p.reshape(bp, tp, d), y_s.reshape(bs, ts, d),
            st["p_ckv"], st["p_kpe"], st["p_ssm"], st["p_mconv"], st["p_sconv"],
            st["s_ckv"], st["s_kpe"], st["s_ssm"], st["s_mconv"], st["s_sconv"])
```
